```python
import math
import jax, jax.numpy as jnp
from jax import lax
import numpy as np

D_MODEL = 1024
BATCH = 4
SEQ = 4096
DEPTH = 2
DEC_BATCH = 128
DEC_SEQ = 1
PAST_LEN = 8192
PAGE_SIZE = 128

N_HEADS = 16
HEAD_DIM = 64
KV_HEADS_A = 4
KV_HEADS_B = 2
G_A = N_HEADS // KV_HEADS_A
G_B = N_HEADS // KV_HEADS_B
IDX_HEADS = 8
IDX_DIM = 64
TOPK_MAX = 256
WINDOW = 128
Q_BLOCK = 128
N_BUCKETS = 32
MAX_DISTANCE = 128
D_FF = 4 * D_MODEL
ALPHA = (2 * DEPTH) ** 0.25
BETA = (8 * DEPTH) ** -0.25
N_A_LAYERS = (DEPTH + 1) // 2
N_B_LAYERS = DEPTH // 2
LN_EPS = 1e-5
NEG = -1e30
ATTN_SCALE = HEAD_DIM ** -0.5
IDX_SCALE = IDX_HEADS ** -0.5 * IDX_DIM ** -0.5
A_SIZES = [N_HEADS * HEAD_DIM, KV_HEADS_A * HEAD_DIM, KV_HEADS_A * HEAD_DIM, IDX_HEADS * IDX_DIM, IDX_DIM, IDX_HEADS]
A_IN = sum(A_SIZES)
A_SPLIT_POINTS = [int(c) for c in np.cumsum(A_SIZES)[:-1]]
B_SIZES = [N_HEADS * HEAD_DIM, KV_HEADS_B * HEAD_DIM, KV_HEADS_B * HEAD_DIM]
B_IN = sum(B_SIZES)
B_SPLIT_POINTS = [int(c) for c in np.cumsum(B_SIZES)[:-1]]

kernel_name = 'dsa_swa_sink_hybrid_step'


def layer_norm(x, g, b):
    xf = x.astype(jnp.float32)
    mu = jnp.mean(xf, axis=-1, keepdims=True)
    var = jnp.mean(jnp.square(xf - mu), axis=-1, keepdims=True)
    y = (xf - mu) * lax.rsqrt(var + LN_EPS) * g.astype(jnp.float32) + b.astype(jnp.float32)
    return y.astype(x.dtype)


def post_norm(x, sub, g, b):
    return layer_norm(ALPHA * x + sub, g, b)


def squared_relu_mlp(x, w1, w2):
    h = jax.nn.relu(x @ w1)
    return (h * h) @ w2


def t5_bias(table, dist):
    n = jnp.maximum(dist, 0)
    max_exact = N_BUCKETS // 2
    nf = jnp.maximum(n, max_exact).astype(jnp.float32)
    large = max_exact + (jnp.log(nf / max_exact) / math.log(MAX_DISTANCE / max_exact) * (N_BUCKETS - max_exact)).astype(jnp.int32)
    large = jnp.minimum(large, N_BUCKETS - 1)
    bucket = jnp.where(n < max_exact, n, large)
    return table.astype(jnp.float32)[bucket]


def dsa_attend(q, qi, wi, q_pos, idx_keys, gather_kv, topk, table):
    B, Q = q.shape[0], q.shape[1]
    L = idx_keys.shape[1]
    s = jnp.einsum('bqhe,bse->bqhs', qi.astype(jnp.float32), idx_keys.astype(jnp.float32))
    score = jnp.einsum('bqhs,bqh->bqs', jax.nn.relu(s), wi.astype(jnp.float32)) * IDX_SCALE
    admissible = jnp.arange(L, dtype=jnp.int32)[None, :] <= q_pos[:, None]
    score = jnp.where(admissible[None], score, NEG)
    _, idx = lax.top_k(score, topk)
    valid = idx <= q_pos[None, :, None]
    k_sel, v_sel = gather_kv(idx)
    qg = q.reshape(B, Q, KV_HEADS_A, G_A, HEAD_DIM)
    logits = jnp.einsum('bqgrd,bqkgd->bqgrk', qg, k_sel).astype(jnp.float32) * ATTN_SCALE
    bias = t5_bias(table, q_pos[None, :, None] - idx)
    bias = bias.reshape(B, Q, topk, KV_HEADS_A, G_A).transpose(0, 1, 3, 4, 2)
    logits = jnp.where(valid[:, :, None, None, :], logits + bias, NEG)
    p = jax.nn.softmax(logits, axis=-1).astype(v_sel.dtype)
    out = jnp.einsum('bqgrk,bqkgd->bqgrd', p, v_sel)
    return out.reshape(B, Q, N_HEADS * HEAD_DIM)


def a_project(x, w_in, idx_g, idx_b):
    B, S, _ = x.shape
    q, k, v, qi, ki, wi = jnp.split(x @ w_in, A_SPLIT_POINTS, axis=-1)
    q = q.reshape(B, S, N_HEADS, HEAD_DIM)
    k = k.reshape(B, S, KV_HEADS_A, HEAD_DIM)
    v = v.reshape(B, S, KV_HEADS_A, HEAD_DIM)
    qi = qi.reshape(B, S, IDX_HEADS, IDX_DIM)
    ki = layer_norm(ki, idx_g, idx_b)
    return q, k, v, qi, ki, wi


def mixer_a_prompt(x, w_in, w_out, idx_g, idx_b, table):
    B, S, _ = x.shape
    q, k, v, qi, ki, wi = a_project(x, w_in, idx_g, idx_b)
    topk = min(TOPK_MAX, S // 4)
    nb = S // Q_BLOCK
    take = jax.vmap(lambda rows, ids: rows[ids])

    def gather_kv(idx):
        return take(k, idx), take(v, idx)

    def to_blocks(a):
        return a.reshape(B, nb, Q_BLOCK, *a.shape[2:]).swapaxes(0, 1)

    def one_block(args):
        n, qb, qib, wib = args
        q_pos = n * Q_BLOCK + jnp.arange(Q_BLOCK, dtype=jnp.int32)
        return dsa_attend(qb, qib, wib, q_pos, ki, gather_kv, topk, table)

    out = lax.map(one_block, (jnp.arange(nb, dtype=jnp.int32), to_blocks(q), to_blocks(qi), to_blocks(wi)))
    out = out.swapaxes(0, 1).reshape(B, S, N_HEADS * HEAD_DIM)
    return out @ w_out, k, v, ki


def mixer_a_sample(x, layer, cache_k, cache_v, cache_idx_k, page_table, w_in, w_out, idx_g, idx_b, table):
    DB, DS, _ = x.shape
    q, k, v, qi, ki, wi = a_project(x, w_in, idx_g, idx_b)
    past = page_table.shape[1] * PAGE_SIZE
    past_ki = cache_idx_k[layer, page_table].reshape(DB, past, IDX_DIM)
    keys_ki = jnp.concatenate([past_ki, ki.astype(past_ki.dtype)], axis=1)
    topk = min(TOPK_MAX, (past + DS) // 4)
    take = jax.vmap(lambda rows, ids: rows[ids])

    def gather_kv(idx):
        from_past = (idx < past)[..., None, None]
        pi = jnp.minimum(idx, past - 1)
        phys = take(page_table, pi // PAGE_SIZE)
        off = pi % PAGE_SIZE
        ni = jnp.clip(idx - past, 0, DS - 1)
        k_sel = jnp.where(from_past, cache_k[layer, phys, off], take(k, ni).astype(cache_k.dtype))
        v_sel = jnp.where(from_past, cache_v[layer, phys, off], take(v, ni).astype(cache_v.dtype))
        return k_sel, v_sel

    q_pos = past + jnp.arange(DS, dtype=jnp.int32)
    out = dsa_attend(q, qi, wi, q_pos, keys_ki, gather_kv, topk, table)
    return out @ w_out, k, v, ki


def sink_softmax(logits, sink):
    s = jnp.broadcast_to(sink.astype(jnp.float32).reshape(KV_HEADS_B, G_B, 1, 1), logits.shape[:-1] + (1,))
    return jax.nn.softmax(jnp.concatenate([logits, s], axis=-1), axis=-1)[..., :-1]


def b_project(x, w_in):
    B, S, _ = x.shape
    q, k, v = jnp.split(x @ w_in, B_SPLIT_POINTS, axis=-1)
    return (q.reshape(B, S, N_HEADS, HEAD_DIM), k.reshape(B, S, KV_HEADS_B, HEAD_DIM), v.reshape(B, S, KV_HEADS_B, HEAD_DIM))


def mixer_b_prompt(x, w_in, w_out, sink, table):
    B, S, _ = x.shape
    q, k, v = b_project(x, w_in)
    W = WINDOW
    nb = S // W
    qb = q.reshape(B, nb, W, KV_HEADS_B, G_B, HEAD_DIM)

    def band(a):
        ab = a.reshape(B, nb, W, KV_HEADS_B, HEAD_DIM)
        prev = jnp.pad(ab, ((0, 0), (1, 0), (0, 0), (0, 0), (0, 0)))[:, :-1]
        return jnp.concatenate([prev, ab], axis=2)

    keys, vals = band(k), band(v)
    logits = jnp.einsum('bnqgrd,bnkgd->bngrqk', qb, keys).astype(jnp.float32) * ATTN_SCALE
    i = jnp.arange(W, dtype=jnp.int32)
    j = jnp.arange(2 * W, dtype=jnp.int32)
    dist = i[:, None] + W - j[None, :]
    in_window = (dist >= 0) & (dist < WINDOW)
    key_pos = jnp.arange(nb, dtype=jnp.int32)[:, None] * W - W + j[None, :]
    mask = in_window[None] & (key_pos >= 0)[:, None, :]
    bias = t5_bias(table, dist).reshape(W, 2 * W, KV_HEADS_B, G_B).transpose(2, 3, 0, 1)
    logits = jnp.where(mask[None, :, None, None], logits + bias[None, None], NEG)
    p = sink_softmax(logits, sink).astype(vals.dtype)
    out = jnp.einsum('bngrqk,bnkgd->bnqgrd', p, vals).reshape(B, S, N_HEADS * HEAD_DIM)
    wb = min(WINDOW, S)
    return out @ w_out, k[:, S - wb:], v[:, S - wb:]


def mixer_b_sample(x, win_k, win_v, w_in, w_out, sink, table):
    DB, DS, _ = x.shape
    q, k, v = b_project(x, w_in)
    wb = win_k.shape[1]
    keys = jnp.concatenate([win_k, k.astype(win_k.dtype)], axis=1)
    vals = jnp.concatenate([win_v, v.astype(win_v.dtype)], axis=1)
    key_pos = PAST_LEN - wb + jnp.arange(wb + DS, dtype=jnp.int32)
    q_pos = PAST_LEN + jnp.arange(DS, dtype=jnp.int32)
    dist = q_pos[:, None] - key_pos[None, :]
    mask = (dist >= 0) & (dist < WINDOW)
    qg = q.reshape(DB, DS, KV_HEADS_B, G_B, HEAD_DIM)
    logits = jnp.einsum('bqgrd,bkgd->bgrqk', qg, keys).astype(jnp.float32) * ATTN_SCALE
    bias = t5_bias(table, dist).reshape(DS, wb + DS, KV_HEADS_B, G_B).transpose(2, 3, 0, 1)
    logits = jnp.where(mask[None, None, None], logits + bias[None], NEG)
    p = sink_softmax(logits, sink).astype(vals.dtype)
    out = jnp.einsum('bgrqk,bkgd->bqgrd', p, vals).reshape(DB, DS, N_HEADS * HEAD_DIM)
    return out @ w_out, keys[:, -wb:], vals[:, -wb:]


def setup_inputs(seed: int = 0) -> dict:
    key = jax.random.key(seed)
    ks = jax.random.split(key, 24)
    nrm = jax.random.normal
    n_pages = PAST_LEN // PAGE_SIZE
    n_used = DEC_BATCH * n_pages
    n_pool = n_used + n_used // 4
    wb = min(WINDOW, PAST_LEN)
    x_prompt = nrm(ks[0], (BATCH, SEQ, D_MODEL), jnp.float32)
    x_sample = nrm(ks[1], (DEC_BATCH, DEC_SEQ, D_MODEL), jnp.float32)
    cache_k = nrm(ks[2], (N_A_LAYERS, n_pool, PAGE_SIZE, KV_HEADS_A, HEAD_DIM), jnp.float32)
    cache_v = nrm(ks[3], (N_A_LAYERS, n_pool, PAGE_SIZE, KV_HEADS_A, HEAD_DIM), jnp.float32)
    cache_idx_k = nrm(ks[4], (N_A_LAYERS, n_pool, PAGE_SIZE, IDX_DIM), jnp.float32)
    state_win_k = nrm(ks[5], (N_B_LAYERS, DEC_BATCH, wb, KV_HEADS_B, HEAD_DIM), jnp.float32)
    state_win_v = nrm(ks[6], (N_B_LAYERS, DEC_BATCH, wb, KV_HEADS_B, HEAD_DIM), jnp.float32)
    page_table = jax.random.permutation(ks[7], n_pool)[:n_used].reshape(DEC_BATCH, n_pages).astype(jnp.int32)
    rel_bias_table = 0.1 * nrm(ks[8], (N_BUCKETS, N_HEADS), jnp.float32)
    a_col_scale = jnp.concatenate([jnp.ones((A_SIZES[0] + A_SIZES[1],), jnp.float32), jnp.full((A_SIZES[2],), BETA, jnp.float32), jnp.ones((sum(A_SIZES[3:]),), jnp.float32)])
    a_w_in = nrm(ks[9], (N_A_LAYERS, D_MODEL, A_IN), jnp.float32) * D_MODEL ** -0.5 * a_col_scale
    a_w_out = nrm(ks[10], (N_A_LAYERS, N_HEADS * HEAD_DIM, D_MODEL), jnp.float32) * (N_HEADS * HEAD_DIM) ** -0.5 * BETA
    a_idx_ln_g = 1.0 + 0.02 * nrm(ks[11], (N_A_LAYERS, IDX_DIM), jnp.float32)
    a_idx_ln_b = 0.02 * nrm(ks[12], (N_A_LAYERS, IDX_DIM), jnp.float32)
    b_col_scale = jnp.concatenate([jnp.ones((B_SIZES[0] + B_SIZES[1],), jnp.float32), jnp.full((B_SIZES[2],), BETA, jnp.float32)])
    b_w_in = nrm(ks[13], (N_B_LAYERS, D_MODEL, B_IN), jnp.float32) * D_MODEL ** -0.5 * b_col_scale
    b_w_out = nrm(ks[14], (N_B_LAYERS, N_HEADS * HEAD_DIM, D_MODEL), jnp.float32) * (N_HEADS * HEAD_DIM) ** -0.5 * BETA
    b_sink = nrm(ks[15], (N_B_LAYERS, N_HEADS), jnp.float32)
    ln1_g = 1.0 + 0.02 * nrm(ks[16], (DEPTH, D_MODEL), jnp.float32)
    ln1_b = 0.02 * nrm(ks[17], (DEPTH, D_MODEL), jnp.float32)
    ln2_g = 1.0 + 0.02 * nrm(ks[18], (DEPTH, D_MODEL), jnp.float32)
    ln2_b = 0.02 * nrm(ks[19], (DEPTH, D_MODEL), jnp.float32)
    mlp_w1 = nrm(ks[20], (DEPTH, D_MODEL, D_FF), jnp.float32) * D_MODEL ** -0.5 * BETA
    mlp_w2 = nrm(ks[21], (DEPTH, D_FF, D_MODEL), jnp.float32) * D_FF ** -0.5 * BETA
    return {'x_prompt': x_prompt, 'x_sample': x_sample, 'cache_k': cache_k, 'cache_v': cache_v, 'cache_idx_k': cache_idx_k, 'state_win_k': state_win_k, 'state_win_v': state_win_v, 'page_table': page_table, 'rel_bias_table': rel_bias_table, 'a_w_in': a_w_in, 'a_w_out': a_w_out, 'a_idx_ln_g': a_idx_ln_g, 'a_idx_ln_b': a_idx_ln_b, 'b_w_in': b_w_in, 'b_w_out': b_w_out, 'b_sink': b_sink, 'ln1_g': ln1_g, 'ln1_b': ln1_b, 'ln2_g': ln2_g, 'ln2_b': ln2_b, 'mlp_w1': mlp_w1, 'mlp_w2': mlp_w2}


def reference(x_prompt, x_sample, cache_k, cache_v, cache_idx_k, state_win_k, state_win_v, page_table, rel_bias_table, a_w_in, a_w_out, a_idx_ln_g, a_idx_ln_b, b_w_in, b_w_out, b_sink, ln1_g, ln1_b, ln2_g, ln2_b, mlp_w1, mlp_w2):
    B, S, _ = x_prompt.shape
    y_p, y_s = x_prompt, x_sample
    kp, vp, ikp, ks_, vs_, iks = [], [], [], [], [], []
    wkp, wvp, wks, wvs = [], [], [], []
    for i in range(DEPTH):
        j = i // 2
        if i % 2 == 0:
            m_p, k_p, v_p, ik_p = mixer_a_prompt(y_p, a_w_in[j], a_w_out[j], a_idx_ln_g[j], a_idx_ln_b[j], rel_bias_table)
            m_s, k_s, v_s, ik_s = mixer_a_sample(y_s, j, cache_k, cache_v, cache_idx_k, page_table, a_w_in[j], a_w_out[j], a_idx_ln_g[j], a_idx_ln_b[j], rel_bias_table)
            kp.append(k_p.reshape(B, S // PAGE_SIZE, PAGE_SIZE, KV_HEADS_A, HEAD_DIM))
            vp.append(v_p.reshape(B, S // PAGE_SIZE, PAGE_SIZE, KV_HEADS_A, HEAD_DIM))
            ikp.append(ik_p.reshape(B, S // PAGE_SIZE, PAGE_SIZE, IDX_DIM))
            ks_.append(k_s)
            vs_.append(v_s)
            iks.append(ik_s)
        else:
            m_p, wk_p, wv_p = mixer_b_prompt(y_p, b_w_in[j], b_w_out[j], b_sink[j], rel_bias_table)
            m_s, wk_s, wv_s = mixer_b_sample(y_s, state_win_k[j], state_win_v[j], b_w_in[j], b_w_out[j], b_sink[j], rel_bias_table)
            wkp.append(wk_p)
            wvp.append(wv_p)
            wks.append(wk_s)
            wvs.append(wv_s)
        y_p = post_norm(y_p, m_p, ln1_g[i], ln1_b[i])
        y_s = post_norm(y_s, m_s, ln1_g[i], ln1_b[i])
        y_p = post_norm(y_p, squared_relu_mlp(y_p, mlp_w1[i], mlp_w2[i]), ln2_g[i], ln2_b[i])
        y_s = post_norm(y_s, squared_relu_mlp(y_s, mlp_w1[i], mlp_w2[i]), ln2_g[i], ln2_b[i])
    return (y_p, y_s, jnp.stack(kp), jnp.stack(vp), jnp.stack(ikp), jnp.stack(ks_), jnp.stack(vs_), jnp.stack(iks), jnp.stack(wkp), jnp.stack(wvp), jnp.stack(wks), jnp.stack(wvs))
```

```python
import functools
import math

import numpy as np
import jax
import jax.numpy as jnp
from jax import lax
from jax.experimental import pallas as pl
from jax.experimental.pallas import tpu as pltpu

F32 = jnp.float32
BF16 = jnp.bfloat16

N_HEADS = 16
HEAD_DIM = 64
KV_HEADS_A = 4
KV_HEADS_B = 2
IDX_HEADS = 8
IDX_DIM = 64
TOPK_MAX = 256
WINDOW = 128
TILE = 128
N_BUCKETS = 32
MAX_DISTANCE = 128
LN_EPS = 1e-5
NEG = -1e30
INT_MIN = -(2 ** 31)
ATTN_SCALE = HEAD_DIM ** -0.5
IDX_SCALE = IDX_HEADS ** -0.5 * IDX_DIM ** -0.5
FAR_BUCKET = N_BUCKETS - 1
FAR_DISTANCE = 113

_NT = (((1,), (1,)), ((), ()))
_VMEM_LIMIT = 48 * 1024 * 1024


def _cparams(sem):
    return pltpu.CompilerParams(dimension_semantics=sem, vmem_limit_bytes=_VMEM_LIMIT)


def _bucket_np(dist):
    n = np.maximum(dist, 0)
    max_exact = N_BUCKETS // 2
    nf = np.maximum(n, max_exact).astype(np.float64)
    val = np.log(nf / max_exact) / math.log(MAX_DISTANCE / max_exact) * (N_BUCKETS - max_exact)
    frac = val - np.floor(val)
    interior = (n > max_exact) & (n < MAX_DISTANCE)
    assert not np.any(interior & ((frac < 1e-6) | (frac > 1 - 1e-6))), "bucket boundary too close to an integer"
    large = np.minimum(max_exact + val.astype(np.int32), N_BUCKETS - 1)
    out = np.where(n < max_exact, n, large).astype(np.int32)
    assert np.all(out[n >= FAR_DISTANCE] == FAR_BUCKET)
    return out


def _sortable(x):
    b = lax.bitcast_convert_type(x, jnp.int32)
    return b ^ (lax.shift_right_arithmetic(b, 31) & 0x7FFFFFFF)


def _ln(y, g, b):
    mu = jnp.mean(y, axis=-1, keepdims=True)
    yc = y - mu
    var = jnp.mean(yc * yc, axis=-1, keepdims=True)
    return yc * lax.rsqrt(var + LN_EPS) * g + b


def _bias_kernel(table_ref, bk_ref, bkd_ref, toep_ref, dec_ref):
    bk = bk_ref[...]
    bkd = bkd_ref[...]
    dec_rows = []
    for h in range(N_HEADS):
        acc = jnp.zeros(bk.shape, F32)
        accd = jnp.zeros(bkd.shape, F32)
        for b in range(N_BUCKETS):
            val = table_ref[b, h]
            acc = jnp.where(bk == b, val, acc)
            accd = jnp.where(bkd == b, val, accd)
        toep_ref[h] = acc
        dec_rows.append(accd)
    for kind in range(3):
        dec_ref[kind] = jnp.concatenate([d[kind:kind + 1, :] for d in dec_rows], axis=0)


def _bias_tables(table):
    i = np.arange(TILE)[None, :]
    j = np.arange(2 * TILE)[:, None]
    bk = np.concatenate([np.full((TILE, TILE), FAR_BUCKET, np.int32), _bucket_np(i + TILE - j)], axis=0)
    bkd = np.zeros((8, TILE), np.int32)
    bkd[0] = _bucket_np(TILE - np.arange(TILE))
    bkd[1] = FAR_BUCKET
    bkd[2] = 0
    toep, dec = pl.pallas_call(
        _bias_kernel,
        grid=(1,),
        in_specs=[pl.BlockSpec(memory_space=pltpu.SMEM),
                  pl.BlockSpec((3 * TILE, TILE), lambda h: (0, 0)),
                  pl.BlockSpec((8, TILE), lambda h: (0, 0))],
        out_specs=[pl.BlockSpec((N_HEADS, 3 * TILE, TILE), lambda h: (0, 0, 0)),
                   pl.BlockSpec((3, N_HEADS, TILE), lambda h: (0, 0, 0))],
        out_shape=[jax.ShapeDtypeStruct((N_HEADS, 3 * TILE, TILE), F32),
                   jax.ShapeDtypeStruct((3, N_HEADS, TILE), F32)],
        compiler_params=_cparams(("arbitrary",)),
        name="bias_tables",
    )(table, jnp.asarray(bk), jnp.asarray(bkd))
    return toep, dec


def _proj_kernel(nkv, has_idx, *refs):
    if has_idx:
        (x_ref, wqT_ref, wkv_ref, wkh_ref, wvT_ref, wqiT_ref, wki_ref, wwiT_ref, g_ref, b_ref,
         qT_ref, k_ref, v_ref, kh_ref, vT_ref, qiT_ref, ki_ref, kib_ref, wiT_ref) = refs
    else:
        (x_ref, wqT_ref, wkv_ref, wkh_ref, wvT_ref, qT_ref, k_ref, v_ref, kh_ref, vT_ref) = refs
    xb = x_ref[0].astype(BF16)
    kvd = nkv * HEAD_DIM
    qT_ref[0] = lax.dot_general(wqT_ref[...], xb, _NT, preferred_element_type=F32).astype(BF16)
    kv = jnp.dot(xb, wkv_ref[...], preferred_element_type=F32)
    k_ref[0] = kv[:, :kvd]
    v_ref[0] = kv[:, kvd:]
    for g in range(nkv):
        kh_ref[0, g] = jnp.dot(xb, wkh_ref[g], preferred_element_type=F32).astype(BF16)
    vT_ref[0] = lax.dot_general(wvT_ref[...], xb, _NT, preferred_element_type=F32).astype(BF16)
    if has_idx:
        qiT_ref[0] = lax.dot_general(wqiT_ref[...], xb, _NT, preferred_element_type=F32).astype(BF16)
        ki = _ln(jnp.dot(xb, wki_ref[...], preferred_element_type=F32), g_ref[...], b_ref[...])
        ki_ref[0] = ki
        kib_ref[0] = ki.astype(BF16)
        wiT_ref[0] = lax.dot_general(wwiT_ref[...], xb, _NT, preferred_element_type=F32)


def _prompt_proj(x, wq, wk, wv, nkv, idx=None, tm=512):
    B, S, D = x.shape
    tm = min(tm, S)
    kvd = nkv * HEAD_DIM
    hq = N_HEADS * HEAD_DIM
    wqT = (wq * ATTN_SCALE).T.astype(BF16)
    wkv = jnp.concatenate([wk, wv], axis=1).astype(BF16)
    wkh = wk.reshape(D, nkv, HEAD_DIM).transpose(1, 0, 2).astype(BF16)
    wvT = wv.T.astype(BF16)
    full2 = lambda a: pl.BlockSpec(a.shape, lambda b, m: (0,) * a.ndim)
    ins = [x, wqT, wkv, wkh, wvT]
    in_specs = [pl.BlockSpec((1, tm, D), lambda b, m: (b, m, 0))] + [full2(a) for a in ins[1:]]
    out_shape = [jax.ShapeDtypeStruct((B, hq, S), BF16),
                 jax.ShapeDtypeStruct((B, S, kvd), F32),
                 jax.ShapeDtypeStruct((B, S, kvd), F32),
                 jax.ShapeDtypeStruct((B, nkv, S, HEAD_DIM), BF16),
                 jax.ShapeDtypeStruct((B, kvd, S), BF16)]
    out_specs = [pl.BlockSpec((1, hq, tm), lambda b, m: (b, 0, m)),
                 pl.BlockSpec((1, tm, kvd), lambda b, m: (b, m, 0)),
                 pl.BlockSpec((1, tm, kvd), lambda b, m: (b, m, 0)),
                 pl.BlockSpec((1, nkv, tm, HEAD_DIM), lambda b, m: (b, 0, m, 0)),
                 pl.BlockSpec((1, kvd, tm), lambda b, m: (b, 0, m))]
    if idx is not None:
        wqi, wki, wwi, g, bb = idx
        extra = [wqi.T.astype(BF16), wki.astype(BF16), wwi.T.astype(BF16), g.reshape(1, -1), bb.reshape(1, -1)]
        ins += extra
        in_specs += [full2(a) for a in extra]
        hi = IDX_HEADS * IDX_DIM
        out_shape += [jax.ShapeDtypeStruct((B, hi, S), BF16),
                      jax.ShapeDtypeStruct((B, S, IDX_DIM), F32),
                      jax.ShapeDtypeStruct((B, S, IDX_DIM), BF16),
                      jax.ShapeDtypeStruct((B, IDX_HEADS, S), F32)]
        out_specs += [pl.BlockSpec((1, hi, tm), lambda b, m: (b, 0, m)),
                      pl.BlockSpec((1, tm, IDX_DIM), lambda b, m: (b, m, 0)),
                      pl.BlockSpec((1, tm, IDX_DIM), lambda b, m: (b, m, 0)),
                      pl.BlockSpec((1, IDX_HEADS, tm), lambda b, m: (b, 0, m))]
    return pl.pallas_call(
        functools.partial(_proj_kernel, nkv, idx is not None),
        grid=(B, S // tm),
        in_specs=in_specs,
        out_specs=out_specs,
        out_shape=out_shape,
        compiler_params=_cparams(("arbitrary", "arbitrary")),
        name="prompt_proj_idx" if idx is not None else "prompt_proj",
    )(*ins)


def _dsa_prompt_kernel(topk, qiT_ref, wiT_ref, ki_ref, qT_ref, kh_ref, vT_ref, toep_ref, o_ref,
                       skey_ref, sel_ref):
    n = pl.program_id(1)
    nk = n + 1
    G = N_HEADS // KV_HEADS_A
    rows = lax.broadcasted_iota(jnp.int32, (TILE, TILE), 0)
    lanes = lax.broadcasted_iota(jnp.int32, (TILE, TILE), 1)
    qpos = n * TILE + lanes

    qi = qiT_ref[0]
    qi_all = jnp.concatenate([qi[IDX_DIM * h:IDX_DIM * (h + 1), :] for h in range(IDX_HEADS)], axis=1)
    wi = wiT_ref[0]

    def score_body(j, carry):
        off = pl.multiple_of(j * TILE, TILE)
        kij = ki_ref[0, pl.ds(off, TILE), :]
        s = jnp.dot(kij, qi_all, preferred_element_type=F32)
        acc = jnp.zeros((TILE, TILE), F32)
        for h in range(IDX_HEADS):
            acc = acc + jnp.maximum(s[:, h * TILE:(h + 1) * TILE], 0.0) * wi[h:h + 1, :]
        sc = acc * IDX_SCALE
        sc = jnp.where(off + rows <= qpos, sc, NEG)
        sc = jnp.where(sc == 0.0, 0.0, sc)
        skey_ref[pl.ds(off, TILE), :] = _sortable(sc)
        return carry

    lax.fori_loop(0, nk, score_body, 0)

    def count(pred):
        def body(j, c):
            off = pl.multiple_of(j * TILE, TILE)
            ind = jnp.where(pred(skey_ref[pl.ds(off, TILE), :]), 1.0, 0.0)
            return c + jnp.sum(ind.reshape(TILE // 8, 8, TILE), axis=0)
        c8 = lax.fori_loop(0, nk, body, jnp.zeros((8, TILE), F32))
        return jnp.sum(c8, axis=0, keepdims=True)

    def bit_body(b, t):
        cand = t ^ lax.shift_left(jnp.int32(1), 31 - b)
        cnt = count(lambda kt: kt >= cand)
        return jnp.where(cnt >= topk, cand, t)

    t = lax.fori_loop(0, 32, bit_body, jnp.full((1, TILE), INT_MIN, jnp.int32))
    room = topk - count(lambda kt: kt > t)

    ii = lax.broadcasted_iota(jnp.int32, (TILE, TILE), 0)
    jj = lax.broadcasted_iota(jnp.int32, (TILE, TILE), 1)
    lower = jnp.where(jj < ii, 1.0, 0.0).astype(BF16)
    ones8 = jnp.ones((8, TILE), BF16)

    def sel_body(j, carry):
        off = pl.multiple_of(j * TILE, TILE)
        kt = skey_ref[pl.ds(off, TILE), :]
        eq = kt == t
        eqb = jnp.where(eq, 1.0, 0.0).astype(BF16)
        before = jnp.dot(lower, eqb, preferred_element_type=F32) + carry
        tot = jnp.dot(ones8, eqb, preferred_element_type=F32)[0:1]
        sel = jnp.where(kt > t, 1.0, jnp.where(eq, jnp.where(before < room, 1.0, 0.0), 0.0))
        sel_ref[pl.ds(off, TILE), :] = jnp.where(off + rows <= qpos, sel, 0.0)
        return carry + tot

    lax.fori_loop(0, nk, sel_body, jnp.zeros((1, TILE), F32))

    q = qT_ref[0]
    for g in range(KV_HEADS_A):
        heads = [g * G + r for r in range(G)]
        qg = jnp.concatenate([q[h * HEAD_DIM:(h + 1) * HEAD_DIM, :] for h in heads], axis=1)

        def att_body(j, carry, g=g, heads=heads, qg=qg):
            m, l, acc = carry
            off = pl.multiple_of(j * TILE, TILE)
            kt = kh_ref[0, g, pl.ds(off, TILE), :]
            s = jnp.dot(kt, qg, preferred_element_type=F32)
            roff = pl.multiple_of(jnp.where(j == n, 2 * TILE, jnp.where(j == n - 1, TILE, 0)), TILE)
            bias = jnp.concatenate([toep_ref[h, pl.ds(roff, TILE), :] for h in heads], axis=1)
            selv = sel_ref[pl.ds(off, TILE), :]
            selb = jnp.concatenate([selv] * G, axis=1) > 0.5
            s = jnp.where(selb, s + bias, NEG)
            m_new = jnp.maximum(m, jnp.max(s, axis=0, keepdims=True))
            alpha = jnp.exp(m - m_new)
            p = jnp.exp(s - m_new)
            l_new = alpha * l + jnp.sum(p, axis=0, keepdims=True)
            vt = vT_ref[0, g * HEAD_DIM:(g + 1) * HEAD_DIM, pl.ds(off, TILE)]
            acc_new = acc * alpha + jnp.dot(vt, p.astype(BF16), preferred_element_type=F32)
            return m_new, l_new, acc_new

        init = (jnp.full((1, G * TILE), NEG, F32), jnp.zeros((1, G * TILE), F32),
                jnp.zeros((HEAD_DIM, G * TILE), F32))
        m, l, acc = lax.fori_loop(0, nk, att_body, init)
        out = acc / l
        for r, h in enumerate(heads):
            o_ref[0, h * HEAD_DIM:(h + 1) * HEAD_DIM, :] = out[:, r * TILE:(r + 1) * TILE].astype(o_ref.dtype)


def _dsa_prompt(qiT, wiT, kib, qT, kh, vT, toep, topk):
    B, hq, S = qT.shape
    nq = S // TILE
    return pl.pallas_call(
        functools.partial(_dsa_prompt_kernel, topk),
        grid=(B, nq),
        in_specs=[pl.BlockSpec((1, qiT.shape[1], TILE), lambda b, n: (b, 0, n)),
                  pl.BlockSpec((1, IDX_HEADS, TILE), lambda b, n: (b, 0, n)),
                  pl.BlockSpec((1, S, IDX_DIM), lambda b, n: (b, 0, 0)),
                  pl.BlockSpec((1, hq, TILE), lambda b, n: (b, 0, n)),
                  pl.BlockSpec((1, KV_HEADS_A, S, HEAD_DIM), lambda b, n: (b, 0, 0, 0)),
                  pl.BlockSpec((1, KV_HEADS_A * HEAD_DIM, S), lambda b, n: (b, 0, 0)),
                  pl.BlockSpec(toep.shape, lambda b, n: (0, 0, 0))],
        out_specs=pl.BlockSpec((1, hq, TILE), lambda b, n: (b, 0, n)),
        out_shape=jax.ShapeDtypeStruct((B, hq, S), BF16),
        scratch_shapes=[pltpu.VMEM((S, TILE), jnp.int32), pltpu.VMEM((S, TILE), F32)],
        compiler_params=_cparams(("arbitrary", "arbitrary")),
        name="dsa_prompt",
    )(qiT, wiT, kib, qT, kh, vT, toep)


def _swa_prompt_kernel(sink_ref, qT_ref, khp_ref, khc_ref, vTp_ref, vTc_ref, toep_ref, o_ref):
    n = pl.program_id(1)
    G = N_HEADS // KV_HEADS_B
    rows = lax.broadcasted_iota(jnp.int32, (2 * TILE, TILE), 0)
    lanes = lax.broadcasted_iota(jnp.int32, (2 * TILE, TILE), 1)
    dj = rows - lanes
    inwin = jnp.where(dj >= TILE - WINDOW + 1, jnp.where(dj <= TILE, 1.0, 0.0), 0.0)
    inwin = jnp.where(rows >= TILE, inwin, jnp.where(n > 0, inwin, 0.0))
    maskb = jnp.concatenate([inwin] * G, axis=1) > 0.5
    q = qT_ref[0]
    for g in range(KV_HEADS_B):
        heads = [g * G + r for r in range(G)]
        qg = jnp.concatenate([q[h * HEAD_DIM:(h + 1) * HEAD_DIM, :] for h in heads], axis=1)
        kk = jnp.concatenate([khp_ref[0, g], khc_ref[0, g]], axis=0)
        s = jnp.dot(kk, qg, preferred_element_type=F32)
        bias = jnp.concatenate([toep_ref[h, TILE:3 * TILE, :] for h in heads], axis=1)
        s = jnp.where(maskb, s + bias, NEG)
        sink = jnp.concatenate([jnp.full((1, TILE), sink_ref[h], F32) for h in heads], axis=1)
        m = jnp.maximum(jnp.max(s, axis=0, keepdims=True), sink)
        p = jnp.exp(s - m)
        l = jnp.sum(p, axis=0, keepdims=True) + jnp.exp(sink - m)
        vv = jnp.concatenate([vTp_ref[0, g * HEAD_DIM:(g + 1) * HEAD_DIM, :],
                              vTc_ref[0, g * HEAD_DIM:(g + 1) * HEAD_DIM, :]], axis=1)
        out = jnp.dot(vv, p.astype(BF16), preferred_element_type=F32) / l
        for r, h in enumerate(heads):
            o_ref[0, h * HEAD_DIM:(h + 1) * HEAD_DIM, :] = out[:, r * TILE:(r + 1) * TILE].astype(o_ref.dtype)


def _swa_prompt(qT, kh, vT, toep, sink):
    B, hq, S = qT.shape
    nb = S // TILE
    prev = lambda n: jnp.maximum(n - 1, 0)
    return pl.pallas_call(
        _swa_prompt_kernel,
        grid=(B, nb),
        in_specs=[pl.BlockSpec(memory_space=pltpu.SMEM),
                  pl.BlockSpec((1, hq, TILE), lambda b, n: (b, 0, n)),
                  pl.BlockSpec((1, KV_HEADS_B, TILE, HEAD_DIM), lambda b, n: (b, 0, prev(n), 0)),
                  pl.BlockSpec((1, KV_HEADS_B, TILE, HEAD_DIM), lambda b, n: (b, 0, n, 0)),
                  pl.BlockSpec((1, KV_HEADS_B * HEAD_DIM, TILE), lambda b, n: (b, 0, prev(n))),
                  pl.BlockSpec((1, KV_HEADS_B * HEAD_DIM, TILE), lambda b, n: (b, 0, n)),
                  pl.BlockSpec(toep.shape, lambda b, n: (0, 0, 0))],
        out_specs=pl.BlockSpec((1, hq, TILE), lambda b, n: (b, 0, n)),
        out_shape=jax.ShapeDtypeStruct((B, hq, S), BF16),
        compiler_params=_cparams(("arbitrary", "arbitrary")),
        name="swa_prompt",
    )(sink, qT, kh, kh, vT, vT, toep)


def _outproj_ln_kernel(alpha, x_ref, a_ref, w_ref, g_ref, b_ref, o_ref):
    y = alpha * x_ref[...] + jnp.dot(a_ref[...], w_ref[...], preferred_element_type=F32)
    o_ref[...] = _ln(y, g_ref[...], b_ref[...])


def _outproj_ln(x, a, w, g, b, alpha, tm=512):
    M, D = x.shape
    tm = min(tm, M)
    return pl.pallas_call(
        functools.partial(_outproj_ln_kernel, alpha),
        grid=(M // tm,),
        in_specs=[pl.BlockSpec((tm, D), lambda m: (m, 0)),
                  pl.BlockSpec((tm, a.shape[1]), lambda m: (m, 0)),
                  pl.BlockSpec(w.shape, lambda m: (0, 0)),
                  pl.BlockSpec((1, D), lambda m: (0, 0)),
                  pl.BlockSpec((1, D), lambda m: (0, 0))],
        out_specs=pl.BlockSpec((tm, D), lambda m: (m, 0)),
        out_shape=jax.ShapeDtypeStruct((M, D), F32),
        compiler_params=_cparams(("arbitrary",)),
        name="outproj_ln",
    )(x, a, w, g.reshape(1, D), b.reshape(1, D))


def _mlp_ln_kernel(alpha, x_ref, w1_ref, w2_ref, g_ref, b_ref, o_ref, acc_ref):
    f = pl.program_id(1)

    @pl.when(f == 0)
    def _():
        acc_ref[...] = jnp.zeros_like(acc_ref)

    h = jnp.maximum(jnp.dot(x_ref[...].astype(BF16), w1_ref[...], preferred_element_type=F32), 0.0)
    acc_ref[...] += jnp.dot((h * h).astype(BF16), w2_ref[...], preferred_element_type=F32)

    @pl.when(f == pl.num_programs(1) - 1)
    def _():
        o_ref[...] = _ln(alpha * x_ref[...] + acc_ref[...], g_ref[...], b_ref[...])


def _mlp_ln(x, w1, w2, g, b, alpha, tm=1024, tf=512):
    M, D = x.shape
    FF = w1.shape[1]
    tm = min(tm, M)
    return pl.pallas_call(
        functools.partial(_mlp_ln_kernel, alpha),
        grid=(M // tm, FF // tf),
        in_specs=[pl.BlockSpec((tm, D), lambda m, f: (m, 0)),
                  pl.BlockSpec((D, tf), lambda m, f: (0, f)),
                  pl.BlockSpec((tf, D), lambda m, f: (f, 0)),
                  pl.BlockSpec((1, D), lambda m, f: (0, 0)),
                  pl.BlockSpec((1, D), lambda m, f: (0, 0))],
        out_specs=pl.BlockSpec((tm, D), lambda m, f: (m, 0)),
        out_shape=jax.ShapeDtypeStruct((M, D), F32),
        scratch_shapes=[pltpu.VMEM((tm, D), F32)],
        compiler_params=_cparams(("arbitrary", "arbitrary")),
        name="mlp_ln",
    )(x, w1, w2, g.reshape(1, D), b.reshape(1, D))


def _matmul_kernel(x_ref, w_ref, o_ref):
    o_ref[...] = jnp.dot(x_ref[...].astype(BF16), w_ref[...], preferred_element_type=F32)


def _matmul(x, w):
    M, K = x.shape
    N = w.shape[1]
    return pl.pallas_call(
        _matmul_kernel,
        grid=(1,),
        in_specs=[pl.BlockSpec((M, K), lambda i: (0, 0)), pl.BlockSpec((K, N), lambda i: (0, 0))],
        out_specs=pl.BlockSpec((M, N), lambda i: (0, 0)),
        out_shape=jax.ShapeDtypeStruct((M, N), F32),
        compiler_params=_cparams(("arbitrary",)),
        name="sample_proj",
    )(x, w)


_SEQ_BLOCK = 8


def _dec_index_kernel(topk, n_pages, pt_ref, qi_ref, wi_ref, kiraw_ref, g_ref, b_ref, cidx_ref,
                      selb_ref, newsel_ref, kiln_ref, buf, sem, sc_ref, snew_ref):
    step = pl.program_id(0)
    nseq = pl.num_programs(0) * _SEQ_BLOCK
    L = n_pages * TILE

    def page_copy(seq, slot, p):
        return pltpu.make_async_copy(cidx_ref.at[pt_ref[seq, p]], buf.at[slot, p], sem.at[slot])

    def start(seq, slot):
        for p in range(n_pages):
            page_copy(seq, slot, p).start()

    def wait(seq, slot):
        for p in range(n_pages):
            page_copy(seq, slot, p).wait()

    @pl.when(step == 0)
    def _():
        start(0, 0)

    kiln_ref[...] = _ln(kiraw_ref[...], g_ref[...], b_ref[...])

    def seq_body(i, carry):
        seq = step * _SEQ_BLOCK + i
        slot = i % 2

        @pl.when(seq + 1 < nseq)
        def _():
            start(seq + 1, 1 - slot)

        wait(seq, slot)
        xk = buf[slot].reshape(L, IDX_DIM).astype(BF16)
        qib = qi_ref[i].astype(BF16)
        s = lax.dot_general(qib, xk, _NT, preferred_element_type=F32)
        w = wi_ref[i]
        row = jnp.sum(jnp.maximum(s, 0.0) * w, axis=0, keepdims=True) * IDX_SCALE
        sc_ref[pl.ds(i, 1), :] = jnp.where(row == 0.0, 0.0, row)
        kn = kiln_ref[pl.ds(i, 1), :].astype(BF16).astype(F32)
        sn = jnp.sum(qib.astype(F32) * kn, axis=1, keepdims=True)
        snew = jnp.sum(jnp.maximum(sn, 0.0) * w, axis=0, keepdims=True) * IDX_SCALE
        snew_ref[pl.ds(i, 1), :] = jnp.broadcast_to(jnp.where(snew == 0.0, 0.0, snew), (1, TILE))
        return carry

    lax.fori_loop(0, _SEQ_BLOCK, seq_body, 0)
    keys = _sortable(sc_ref[...])
    knew = _sortable(snew_ref[:, 0:1])

    def count(pred):
        return (jnp.sum(jnp.where(pred(keys), 1.0, 0.0), axis=1, keepdims=True)
                + jnp.where(pred(knew), 1.0, 0.0))

    def bit_body(b, t):
        cand = t ^ lax.shift_left(jnp.int32(1), 31 - b)
        return jnp.where(count(lambda k: k >= cand) >= topk, cand, t)

    t = lax.fori_loop(0, 32, bit_body, jnp.full((_SEQ_BLOCK, 1), INT_MIN, jnp.int32))
    room = topk - count(lambda k: k > t)
    ii = lax.broadcasted_iota(jnp.int32, (TILE, 2 * TILE), 0)
    jj = lax.broadcasted_iota(jnp.int32, (TILE, 2 * TILE), 1)
    pref = jnp.where(jj >= TILE, 1.0, jnp.where(ii < jj, 1.0, 0.0)).astype(BF16)
    carry = jnp.zeros((_SEQ_BLOCK, TILE), F32)
    for jt in range(n_pages):
        kt = keys[:, jt * TILE:(jt + 1) * TILE]
        eq = kt == t
        res = jnp.dot(jnp.where(eq, 1.0, 0.0).astype(BF16), pref, preferred_element_type=F32)
        take = jnp.where(res[:, :TILE] + carry < room, 1.0, 0.0)
        selb_ref[:, jt * TILE:(jt + 1) * TILE] = jnp.where(kt > t, 1.0, jnp.where(eq, take, 0.0))
        carry = carry + res[:, TILE:]
    newsel = jnp.where(knew > t, 1.0, jnp.where(knew == t, jnp.where(carry[:, :1] < room, 1.0, 0.0), 0.0))
    newsel_ref[...] = jnp.broadcast_to(newsel, (_SEQ_BLOCK, TILE))


def _dec_index(page_table, qi, wi, kiraw, g, b, cidx, topk):
    DB, n_pages = page_table.shape
    L = n_pages * TILE
    sb = _SEQ_BLOCK
    grid_spec = pltpu.PrefetchScalarGridSpec(
        num_scalar_prefetch=1,
        grid=(DB // sb,),
        in_specs=[pl.BlockSpec((sb, IDX_HEADS, IDX_DIM), lambda s, pt: (s, 0, 0)),
                  pl.BlockSpec((sb, IDX_HEADS, 1), lambda s, pt: (s, 0, 0)),
                  pl.BlockSpec((sb, IDX_DIM), lambda s, pt: (s, 0)),
                  pl.BlockSpec((1, IDX_DIM), lambda s, pt: (0, 0)),
                  pl.BlockSpec((1, IDX_DIM), lambda s, pt: (0, 0)),
                  pl.BlockSpec(memory_space=pl.ANY)],
        out_specs=[pl.BlockSpec((sb, L), lambda s, pt: (s, 0)),
                   pl.BlockSpec((sb, TILE), lambda s, pt: (s, 0)),
                   pl.BlockSpec((sb, IDX_DIM), lambda s, pt: (s, 0))],
        scratch_shapes=[pltpu.VMEM((2, n_pages, TILE, IDX_DIM), F32),
                        pltpu.SemaphoreType.DMA((2,)),
                        pltpu.VMEM((sb, L), F32),
                        pltpu.VMEM((sb, TILE), F32)])
    return pl.pallas_call(
        functools.partial(_dec_index_kernel, topk, n_pages),
        grid_spec=grid_spec,
        out_shape=[jax.ShapeDtypeStruct((DB, L), F32),
                   jax.ShapeDtypeStruct((DB, TILE), F32),
                   jax.ShapeDtypeStruct((DB, IDX_DIM), F32)],
        compiler_params=_cparams(("arbitrary",)),
        name="dec_index",
    )(page_table, qi, wi, kiraw, g.reshape(1, -1), b.reshape(1, -1), cidx)


_PAGE_CHUNK = 8


def _dec_attn_kernel(n_pages, pt_ref, q_ref, selb_ref, newsel_ref, kn_ref, vn_ref, dec_ref, ck_ref, cv_ref,
                     o_ref, kbuf, vbuf, sem):
    b = pl.program_id(0)
    nb = pl.num_programs(0)
    nch = n_pages // _PAGE_CHUNK
    CL = _PAGE_CHUNK * TILE
    kvd = KV_HEADS_A * HEAD_DIM

    def copies(seq, c, slot):
        out = []
        for p in range(_PAGE_CHUNK):
            phys = pt_ref[seq, c * _PAGE_CHUNK + p]
            out.append(pltpu.make_async_copy(ck_ref.at[phys], kbuf.at[slot, p], sem.at[0, slot]))
            out.append(pltpu.make_async_copy(cv_ref.at[phys], vbuf.at[slot, p], sem.at[1, slot]))
        return out

    def start(seq, c, slot):
        for cp in copies(seq, c, slot):
            cp.start()

    def wait(seq, c, slot):
        for cp in copies(seq, c, slot):
            cp.wait()

    @pl.when(b == 0)
    def _():
        start(0, 0, 0)

    qb = q_ref[0]
    far = dec_ref[1]
    near = dec_ref[0]

    def chunk_body(c, carry):
        m, l, acc = carry
        slot = c % 2
        last = c + 1 == nch
        nxt_b = jnp.where(last, b + 1, b)
        nxt_c = jnp.where(last, 0, c + 1)

        @pl.when(nxt_b < nb)
        def _():
            start(nxt_b, nxt_c, 1 - slot)

        wait(b, c, slot)
        kc = kbuf[slot].reshape(CL, kvd).astype(BF16)
        vc = vbuf[slot].reshape(CL, kvd).astype(BF16)
        s = lax.dot_general(qb, kc, _NT, preferred_element_type=F32)
        bias = jnp.concatenate([far] * (_PAGE_CHUNK - 1) + [jnp.where(last, near, far)], axis=1)
        sel = selb_ref[0, :, pl.ds(pl.multiple_of(c * CL, CL), CL)] > 0.5
        s = jnp.where(sel, s + bias, NEG)
        m_new = jnp.maximum(m, jnp.max(s, axis=1, keepdims=True))
        alpha = jnp.exp(m - m_new)
        p = jnp.exp(s - m_new)
        l_new = alpha * l + jnp.sum(p, axis=1, keepdims=True)
        acc_new = alpha * acc + jnp.dot(p.astype(BF16), vc, preferred_element_type=F32)
        return m_new, l_new, acc_new

    init = (jnp.full((N_HEADS, 1), NEG, F32), jnp.zeros((N_HEADS, 1), F32), jnp.zeros((N_HEADS, kvd), F32))
    m, l, acc = lax.fori_loop(0, nch, chunk_body, init)
    kn = kn_ref[0].astype(BF16).astype(F32)
    sn = jnp.sum(qb.astype(F32) * kn, axis=1, keepdims=True) + dec_ref[2][:, 0:1]
    sn = jnp.where(newsel_ref[0, :, 0:1] > 0.5, sn, NEG)
    m_new = jnp.maximum(m, sn)
    alpha = jnp.exp(m - m_new)
    pn = jnp.exp(sn - m_new)
    l = alpha * l + pn
    acc = alpha * acc + pn * vn_ref[0]
    o_ref[0] = acc / l


def _dec_attn(page_table, qbd, selb, newsel, kn, vn, dec, ck, cv):
    DB, n_pages = page_table.shape
    L = n_pages * TILE
    kvd = KV_HEADS_A * HEAD_DIM
    assert n_pages % (2 * _PAGE_CHUNK) == 0
    grid_spec = pltpu.PrefetchScalarGridSpec(
        num_scalar_prefetch=1,
        grid=(DB,),
        in_specs=[pl.BlockSpec((1, N_HEADS, kvd), lambda s, pt: (s, 0, 0)),
                  pl.BlockSpec((1, 1, L), lambda s, pt: (s, 0, 0)),
                  pl.BlockSpec((1, 1, TILE), lambda s, pt: (s, 0, 0)),
                  pl.BlockSpec((1, 1, kvd), lambda s, pt: (s, 0, 0)),
                  pl.BlockSpec((1, 1, kvd), lambda s, pt: (s, 0, 0)),
                  pl.BlockSpec(dec.shape, lambda s, pt: (0, 0, 0)),
                  pl.BlockSpec(memory_space=pl.ANY),
                  pl.BlockSpec(memory_space=pl.ANY)],
        out_specs=pl.BlockSpec((1, N_HEADS, kvd), lambda s, pt: (s, 0, 0)),
        scratch_shapes=[pltpu.VMEM((2, _PAGE_CHUNK, TILE, kvd), F32),
                        pltpu.VMEM((2, _PAGE_CHUNK, TILE, kvd), F32),
                        pltpu.SemaphoreType.DMA((2, 2))])
    return pl.pallas_call(
        functools.partial(_dec_attn_kernel, n_pages),
        grid_spec=grid_spec,
        out_shape=jax.ShapeDtypeStruct((DB, N_HEADS, kvd), F32),
        compiler_params=_cparams(("arbitrary",)),
        name="dec_attn",
    )(page_table, qbd, selb.reshape(DB, 1, L), newsel.reshape(DB, 1, TILE),
      kn.reshape(DB, 1, kvd), vn.reshape(DB, 1, kvd), dec, ck, cv)


def _swa_dec_kernel(q_ref, wk_ref, wv_ref, kn_ref, vn_ref, dec_ref, sink_ref, o_ref):
    lane = lax.broadcasted_iota(jnp.int32, (N_HEADS, TILE), 1)
    bias = dec_ref[0]
    b0 = dec_ref[2][:, 0:1]
    sk = sink_ref[...]
    for i in range(_SEQ_BLOCK):
        qb = q_ref[i]
        s = lax.dot_general(qb, wk_ref[i].astype(BF16), _NT, preferred_element_type=F32)
        s = jnp.where(lane >= 1, s + bias, NEG)
        kn = kn_ref[i].astype(BF16).astype(F32)
        sn = jnp.sum(qb.astype(F32) * kn, axis=1, keepdims=True) + b0
        m = jnp.maximum(jnp.maximum(jnp.max(s, axis=1, keepdims=True), sn), sk)
        p = jnp.exp(s - m)
        pn = jnp.exp(sn - m)
        l = jnp.sum(p, axis=1, keepdims=True) + pn + jnp.exp(sk - m)
        out = jnp.dot(p.astype(BF16), wv_ref[i].astype(BF16), preferred_element_type=F32) + pn * vn_ref[i]
        o_ref[i] = out / l


def _swa_dec(qbd, wk, wv, kn, vn, dec, sink):
    DB = qbd.shape[0]
    kvd = KV_HEADS_B * HEAD_DIM
    sb = _SEQ_BLOCK
    return pl.pallas_call(
        _swa_dec_kernel,
        grid=(DB // sb,),
        in_specs=[pl.BlockSpec((sb, N_HEADS, kvd), lambda s: (s, 0, 0)),
                  pl.BlockSpec((sb, WINDOW, kvd), lambda s: (s, 0, 0)),
                  pl.BlockSpec((sb, WINDOW, kvd), lambda s: (s, 0, 0)),
                  pl.BlockSpec((sb, 1, kvd), lambda s: (s, 0, 0)),
                  pl.BlockSpec((sb, 1, kvd), lambda s: (s, 0, 0)),
                  pl.BlockSpec(dec.shape, lambda s: (0, 0, 0)),
                  pl.BlockSpec((N_HEADS, 1), lambda s: (0, 0))],
        out_specs=pl.BlockSpec((sb, N_HEADS, kvd), lambda s: (s, 0, 0)),
        out_shape=jax.ShapeDtypeStruct((DB, N_HEADS, kvd), F32),
        compiler_params=_cparams(("arbitrary",)),
        name="swa_dec",
    )(qbd, wk, wv, kn.reshape(DB, 1, kvd), vn.reshape(DB, 1, kvd), dec, sink.reshape(N_HEADS, 1))


def _block_diag_q(q, nkv):
    DB = q.shape[0]
    G = N_HEADS // nkv
    qh = (q * ATTN_SCALE).reshape(DB, N_HEADS, 1, HEAD_DIM)
    own = (np.arange(N_HEADS)[:, None] // G == np.arange(nkv)[None, :]).astype(np.float32)
    return (qh * own[None, :, :, None]).reshape(DB, N_HEADS, nkv * HEAD_DIM).astype(BF16)


def _own_group(o, nkv):
    DB = o.shape[0]
    G = N_HEADS // nkv
    o5 = o.reshape(DB, nkv, G, nkv, HEAD_DIM)
    return jnp.stack([o5[:, g, :, g, :] for g in range(nkv)], axis=1).reshape(DB, N_HEADS * HEAD_DIM)


def _pad_cols(w, mult=TILE):
    pad = (-w.shape[1]) % mult
    return jnp.pad(w, ((0, 0), (0, pad))) if pad else w


def kernel(x_prompt, x_sample, cache_k, cache_v, cache_idx_k, state_win_k, state_win_v, page_table, rel_bias_table, a_w_in, a_w_out, a_idx_ln_g, a_idx_ln_b, b_w_in, b_w_out, b_sink, ln1_g, ln1_b, ln2_g, ln2_b, mlp_w1, mlp_w2):
    B, S, D = x_prompt.shape
    DB, DS, _ = x_sample.shape
    depth = ln1_g.shape[0]
    assert depth == 2 and DS == 1 and S % TILE == 0 and DB % _SEQ_BLOCK == 0
    assert state_win_k.shape[2] == WINDOW and cache_k.shape[2] == TILE
    alpha = (2 * depth) ** 0.25
    n_pages = page_table.shape[1]
    past = n_pages * TILE
    hq = N_HEADS * HEAD_DIM
    kva = KV_HEADS_A * HEAD_DIM
    kvb = KV_HEADS_B * HEAD_DIM
    hi = IDX_HEADS * IDX_DIM

    toep, dec = _bias_tables(rel_bias_table)

    wa = a_w_in[0]
    c0, c1, c2, c3, c4 = hq, hq + kva, hq + 2 * kva, hq + 2 * kva + hi, hq + 2 * kva + hi + IDX_DIM
    wq, wk, wv, wqi, wki, wwi = wa[:, :c0], wa[:, c0:c1], wa[:, c1:c2], wa[:, c2:c3], wa[:, c3:c4], wa[:, c4:]
    a_wo = a_w_out[0].astype(BF16)
    w1 = [mlp_w1[i].astype(BF16) for i in range(depth)]
    w2 = [mlp_w2[i].astype(BF16) for i in range(depth)]

    qT, k_p, v_p, kh, vT, qiT, ki_p, kib, wiT = _prompt_proj(
        x_prompt, wq, wk, wv, KV_HEADS_A, idx=(wqi, wki, wwi, a_idx_ln_g[0], a_idx_ln_b[0]))
    oT = _dsa_prompt(qiT, wiT, kib, qT, kh, vT, toep, min(TOPK_MAX, S // 4))
    attn = oT.transpose(0, 2, 1).reshape(B * S, hq)
    y_p = _outproj_ln(x_prompt.reshape(B * S, D), attn, a_wo, ln1_g[0], ln1_b[0], alpha)
    y_p = _mlp_ln(y_p, w1[0], w2[0], ln2_g[0], ln2_b[0], alpha)

    xs = x_sample.reshape(DB, D)
    proj = _matmul(xs, _pad_cols(wa).astype(BF16))
    q_s, k_s, v_s = proj[:, :c0], proj[:, c0:c1], proj[:, c1:c2]
    qi_s, kiraw_s, wi_s = proj[:, c2:c3], proj[:, c3:c4], proj[:, c4:c4 + IDX_HEADS]
    selb, newsel, ki_s = _dec_index(
        page_table, qi_s.reshape(DB, IDX_HEADS, IDX_DIM), wi_s.reshape(DB, IDX_HEADS, 1), kiraw_s,
        a_idx_ln_g[0], a_idx_ln_b[0], cache_idx_k[0], min(TOPK_MAX, (past + DS) // 4))
    o_s = _dec_attn(page_table, _block_diag_q(q_s, KV_HEADS_A), selb, newsel, k_s, v_s, dec,
                    cache_k[0].reshape(-1, TILE, kva), cache_v[0].reshape(-1, TILE, kva))
    y_s = _outproj_ln(xs, _own_group(o_s, KV_HEADS_A).astype(BF16), a_wo, ln1_g[0], ln1_b[0], alpha)
    y_s = _mlp_ln(y_s, w1[0], w2[0], ln2_g[0], ln2_b[0], alpha)

    wb = b_w_in[0]
    bq, bk, bv = wb[:, :hq], wb[:, hq:hq + kvb], wb[:, hq + kvb:]
    b_wo = b_w_out[0].astype(BF16)

    qT1, k1_p, v1_p, kh1, vT1 = _prompt_proj(y_p.reshape(B, S, D), bq, bk, bv, KV_HEADS_B)
    oT1 = _swa_prompt(qT1, kh1, vT1, toep, b_sink[0])
    attn1 = oT1.transpose(0, 2, 1).reshape(B * S, hq)
    y_p = _outproj_ln(y_p, attn1, b_wo, ln1_g[1], ln1_b[1], alpha)
    y_p = _mlp_ln(y_p, w1[1], w2[1], ln2_g[1], ln2_b[1], alpha)

    proj1 = _matmul(y_s, wb.astype(BF16))
    q1_s, k1_s, v1_s = proj1[:, :hq], proj1[:, hq:hq + kvb], proj1[:, hq + kvb:]
    win_k = state_win_k[0].reshape(DB, WINDOW, kvb)
    win_v = state_win_v[0].reshape(DB, WINDOW, kvb)
    o1_s = _swa_dec(_block_diag_q(q1_s, KV_HEADS_B), win_k, win_v, k1_s, v1_s, dec, b_sink[0])
    y_s = _outproj_ln(y_s, _own_group(o1_s, KV_HEADS_B).astype(BF16), b_wo, ln1_g[1], ln1_b[1], alpha)
    y_s = _mlp_ln(y_s, w1[1], w2[1], ln2_g[1], ln2_b[1], alpha)

    npg = S // TILE
    wbp = min(WINDOW, S)
    new_wk_s = jnp.concatenate([win_k[:, 1:], k1_s[:, None, :]], axis=1).reshape(1, DB, WINDOW, KV_HEADS_B, HEAD_DIM)
    new_wv_s = jnp.concatenate([win_v[:, 1:], v1_s[:, None, :]], axis=1).reshape(1, DB, WINDOW, KV_HEADS_B, HEAD_DIM)
    return (y_p.reshape(B, S, D),
            y_s.reshape(DB, DS, D),
            k_p.reshape(1, B, npg, TILE, KV_HEADS_A, HEAD_DIM),
            v_p.reshape(1, B, npg, TILE, KV_HEADS_A, HEAD_DIM),
            ki_p.reshape(1, B, npg, TILE, IDX_DIM),
            k_s.reshape(1, DB, DS, KV_HEADS_A, HEAD_DIM),
            v_s.reshape(1, DB, DS, KV_HEADS_A, HEAD_DIM),
            ki_s.reshape(1, DB, DS, IDX_DIM),
            k1_p[:, S - wbp:].reshape(1, B, wbp, KV_HEADS_B, HEAD_DIM),
            v1_p[:, S - wbp:].reshape(1, B, wbp, KV_HEADS_B, HEAD_DIM),
            new_wk_s,
            new_wv_s)
```

```python
import functools
import math

import numpy as np
import jax
import jax.numpy as jnp
from jax import lax
from jax.experimental import pallas as pl
from jax.experimental.pallas import tpu as pltpu

F32 = jnp.float32
BF16 = jnp.bfloat16

N_HEADS = 16
HEAD_DIM = 64
KV_HEADS_A = 4
KV_HEADS_B = 2
IDX_HEADS = 8
IDX_DIM = 64
TOPK_MAX = 256
WINDOW = 128
TILE = 128
N_BUCKETS = 32
MAX_DISTANCE = 128
LN_EPS = 1e-5
NEG = -1e30
INT_MIN = -(2 ** 31)
ATTN_SCALE = HEAD_DIM ** -0.5
IDX_SCALE = IDX_HEADS ** -0.5 * IDX_DIM ** -0.5
FAR_BUCKET = N_BUCKETS - 1
FAR_DISTANCE = 113

_NT = (((1,), (1,)), ((), ()))
_VMEM_LIMIT = 48 * 1024 * 1024


def _cparams(sem):
    return pltpu.CompilerParams(dimension_semantics=sem, vmem_limit_bytes=_VMEM_LIMIT)


def _bucket_np(dist):
    n = np.maximum(dist, 0)
    max_exact = N_BUCKETS // 2
    nf = np.maximum(n, max_exact).astype(np.float64)
    val = np.log(nf / max_exact) / math.log(MAX_DISTANCE / max_exact) * (N_BUCKETS - max_exact)
    frac = val - np.floor(val)
    interior = (n > max_exact) & (n < MAX_DISTANCE)
    assert not np.any(interior & ((frac < 1e-6) | (frac > 1 - 1e-6))), "bucket boundary too close to an integer"
    large = np.minimum(max_exact + val.astype(np.int32), N_BUCKETS - 1)
    out = np.where(n < max_exact, n, large).astype(np.int32)
    assert np.all(out[n >= FAR_DISTANCE] == FAR_BUCKET)
    return out


def _sortable(x):
    b = lax.bitcast_convert_type(x, jnp.int32)
    return b ^ (lax.shift_right_arithmetic(b, 31) & 0x7FFFFFFF)


def _ln(y, g, b):
    mu = jnp.mean(y, axis=-1, keepdims=True)
    yc = y - mu
    var = jnp.mean(yc * yc, axis=-1, keepdims=True)
    return yc * lax.rsqrt(var + LN_EPS) * g + b


def _bias_kernel(table_ref, bk_ref, bkd_ref, toep_ref, dec_ref):
    bk = bk_ref[...]
    bkd = bkd_ref[...]
    dec_rows = []
    for h in range(N_HEADS):
        acc = jnp.zeros(bk.shape, F32)
        accd = jnp.zeros(bkd.shape, F32)
        for b in range(N_BUCKETS):
            val = table_ref[b, h]
            acc = jnp.where(bk == b, val, acc)
            accd = jnp.where(bkd == b, val, accd)
        toep_ref[h] = acc
        dec_rows.append(accd)
    for kind in range(3):
        dec_ref[kind] = jnp.concatenate([d[kind:kind + 1, :] for d in dec_rows], axis=0)


def _bias_tables(table):
    i = np.arange(TILE)[None, :]
    j = np.arange(2 * TILE)[:, None]
    bk = np.concatenate([np.full((TILE, TILE), FAR_BUCKET, np.int32), _bucket_np(i + TILE - j)], axis=0)
    bkd = np.zeros((8, TILE), np.int32)
    bkd[0] = _bucket_np(TILE - np.arange(TILE))
    bkd[1] = FAR_BUCKET
    bkd[2] = 0
    toep, dec = pl.pallas_call(
        _bias_kernel,
        grid=(1,),
        in_specs=[pl.BlockSpec(memory_space=pltpu.SMEM),
                  pl.BlockSpec((3 * TILE, TILE), lambda h: (0, 0)),
                  pl.BlockSpec((8, TILE), lambda h: (0, 0))],
        out_specs=[pl.BlockSpec((N_HEADS, 3 * TILE, TILE), lambda h: (0, 0, 0)),
                   pl.BlockSpec((3, N_HEADS, TILE), lambda h: (0, 0, 0))],
        out_shape=[jax.ShapeDtypeStruct((N_HEADS, 3 * TILE, TILE), F32),
                   jax.ShapeDtypeStruct((3, N_HEADS, TILE), F32)],
        compiler_params=_cparams(("arbitrary",)),
        name="bias_tables",
    )(table, jnp.asarray(bk), jnp.asarray(bkd))
    return toep, dec


def _normalize_cols(y):
    mu = jnp.mean(y, axis=0, keepdims=True)
    yc = y - mu
    var = jnp.mean(yc * yc, axis=0, keepdims=True)
    return yc * lax.rsqrt(var + LN_EPS)


def _proj_kernel(nkv, has_idx, *refs):
    if has_idx:
        (x_ref, wqT_ref, wkT_ref, wvT_ref, wkh_ref, wqiT_ref, wkiT_ref, wki_ref, wwiT_ref, g_ref, b_ref,
         gc_ref, bc_ref, qT_ref, kTp_ref, vTp_ref, kh_ref, vT_ref, qiT_ref, kiTp_ref, kib_ref, wiT_ref) = refs
    else:
        (x_ref, wqT_ref, wkT_ref, wvT_ref, wkh_ref, qT_ref, kTp_ref, vTp_ref, kh_ref, vT_ref) = refs
    xb = x_ref[0].astype(BF16)
    npg = xb.shape[0] // TILE
    qT_ref[0] = lax.dot_general(wqT_ref[...], xb, _NT, preferred_element_type=F32).astype(BF16)
    kT = lax.dot_general(wkT_ref[...], xb, _NT, preferred_element_type=F32)
    vT = lax.dot_general(wvT_ref[...], xb, _NT, preferred_element_type=F32)
    vT_ref[0] = vT.astype(BF16)
    for j in range(npg):
        kTp_ref[0, j] = kT[:, j * TILE:(j + 1) * TILE]
        vTp_ref[0, j] = vT[:, j * TILE:(j + 1) * TILE]
    for g in range(nkv):
        kh_ref[0, g] = jnp.dot(xb, wkh_ref[g], preferred_element_type=F32).astype(BF16)
    if has_idx:
        qiT_ref[0] = lax.dot_general(wqiT_ref[...], xb, _NT, preferred_element_type=F32).astype(BF16)
        kiT = _normalize_cols(lax.dot_general(wkiT_ref[...], xb, _NT, preferred_element_type=F32))
        for j in range(npg):
            kiTp_ref[0, j] = kiT[:, j * TILE:(j + 1) * TILE] * gc_ref[...] + bc_ref[...]
        ki = _ln(jnp.dot(xb, wki_ref[...], preferred_element_type=F32), g_ref[...], b_ref[...])
        kib_ref[0] = ki.astype(BF16)
        wiT_ref[0] = lax.dot_general(wwiT_ref[...], xb, _NT, preferred_element_type=F32)


def _prompt_proj(x, wq, wk, wv, nkv, idx=None, tm=512):
    B, S, D = x.shape
    tm = min(tm, S)
    kvd = nkv * HEAD_DIM
    hq = N_HEADS * HEAD_DIM
    wqT = (wq * ATTN_SCALE).T.astype(BF16)
    wkh = wk.reshape(D, nkv, HEAD_DIM).transpose(1, 0, 2).astype(BF16)
    full2 = lambda a: pl.BlockSpec(a.shape, lambda b, m: (0,) * a.ndim)
    ins = [x, wqT, wk.T.astype(BF16), wv.T.astype(BF16), wkh]
    in_specs = [pl.BlockSpec((1, tm, D), lambda b, m: (b, m, 0))] + [full2(a) for a in ins[1:]]
    out_shape = [jax.ShapeDtypeStruct((B, hq, S), BF16),
                 jax.ShapeDtypeStruct((B, S // TILE, kvd, TILE), F32),
                 jax.ShapeDtypeStruct((B, S // TILE, kvd, TILE), F32),
                 jax.ShapeDtypeStruct((B, nkv, S, HEAD_DIM), BF16),
                 jax.ShapeDtypeStruct((B, kvd, S), BF16)]
    out_specs = [pl.BlockSpec((1, hq, tm), lambda b, m: (b, 0, m)),
                 pl.BlockSpec((1, tm // TILE, kvd, TILE), lambda b, m: (b, m, 0, 0)),
                 pl.BlockSpec((1, tm // TILE, kvd, TILE), lambda b, m: (b, m, 0, 0)),
                 pl.BlockSpec((1, nkv, tm, HEAD_DIM), lambda b, m: (b, 0, m, 0)),
                 pl.BlockSpec((1, kvd, tm), lambda b, m: (b, 0, m))]
    if idx is not None:
        wqi, wki, wwi, g, bb = idx
        extra = [wqi.T.astype(BF16), wki.T.astype(BF16), wki.astype(BF16), wwi.T.astype(BF16),
                 g.reshape(1, -1), bb.reshape(1, -1),
                 jnp.broadcast_to(g.reshape(-1, 1), (IDX_DIM, TILE)), jnp.broadcast_to(bb.reshape(-1, 1), (IDX_DIM, TILE))]
        ins += extra
        in_specs += [full2(a) for a in extra]
        hi = IDX_HEADS * IDX_DIM
        out_shape += [jax.ShapeDtypeStruct((B, hi, S), BF16),
                      jax.ShapeDtypeStruct((B, S // TILE, IDX_DIM, TILE), F32),
                      jax.ShapeDtypeStruct((B, S, IDX_DIM), BF16),
                      jax.ShapeDtypeStruct((B, IDX_HEADS, S), F32)]
        out_specs += [pl.BlockSpec((1, hi, tm), lambda b, m: (b, 0, m)),
                      pl.BlockSpec((1, tm // TILE, IDX_DIM, TILE), lambda b, m: (b, m, 0, 0)),
                      pl.BlockSpec((1, tm, IDX_DIM), lambda b, m: (b, m, 0)),
                      pl.BlockSpec((1, IDX_HEADS, tm), lambda b, m: (b, 0, m))]
    return pl.pallas_call(
        functools.partial(_proj_kernel, nkv, idx is not None),
        grid=(B, S // tm),
        in_specs=in_specs,
        out_specs=out_specs,
        out_shape=out_shape,
        compiler_params=_cparams(("arbitrary", "arbitrary")),
        name="prompt_proj_idx" if idx is not None else "prompt_proj",
    )(*ins)


_KEY_UNROLL = 4
_ATT_TILES = 2


def _dsa_prompt_kernel(topk, qiT_ref, wiT_ref, ki_ref, qT_ref, kh_ref, vT_ref, toep_ref, o_ref,
                       skey_ref, sel_ref, m_ref, l_ref, acc_ref):
    n = pl.program_id(1)
    nk = n + 1
    ST = _KEY_UNROLL * TILE
    ns = n // _KEY_UNROLL + 1
    G = N_HEADS // KV_HEADS_A
    rows = lax.broadcasted_iota(jnp.int32, (ST, TILE), 0)
    lanes = lax.broadcasted_iota(jnp.int32, (ST, TILE), 1)
    qpos = n * TILE + lanes

    qi = qiT_ref[0]
    wi = wiT_ref[0]

    def score_body(u, carry):
        off = pl.multiple_of(u * ST, ST)
        kij = ki_ref[0, pl.ds(off, ST), :]
        acc = jnp.zeros((ST, TILE), F32)
        for h in range(0, IDX_HEADS, 2):
            qpair = jnp.concatenate([qi[IDX_DIM * h:IDX_DIM * (h + 1), :],
                                     qi[IDX_DIM * (h + 1):IDX_DIM * (h + 2), :]], axis=1)
            s = jnp.dot(kij, qpair, preferred_element_type=F32)
            acc = acc + jnp.maximum(s[:, :TILE], 0.0) * wi[h:h + 1, :]
            acc = acc + jnp.maximum(s[:, TILE:], 0.0) * wi[h + 1:h + 2, :]
        sc = acc * IDX_SCALE
        sc = jnp.where(off + rows <= qpos, sc, NEG)
        sc = jnp.where(sc == 0.0, 0.0, sc)
        skey_ref[pl.ds(off, ST), :] = _sortable(sc)
        return carry

    lax.fori_loop(0, ns, score_body, 0)

    def count(pred):
        def body(u, c):
            off = pl.multiple_of(u * ST, ST)
            x = jnp.where(pred(skey_ref[pl.ds(off, ST), :]), 1.0, 0.0).reshape(ST // 8, 8, TILE)
            while x.shape[0] > 1:
                half = x.shape[0] // 2
                x = x[:half] + x[half:]
            return c + x[0]
        c8 = lax.fori_loop(0, ns, body, jnp.zeros((8, TILE), F32))
        return jnp.sum(c8, axis=0, keepdims=True)

    def bit_body(b, t):
        cand = t ^ lax.shift_left(jnp.int32(1), 31 - b)
        cnt = count(lambda kt: kt >= cand)
        return jnp.where(cnt >= topk, cand, t)

    t = lax.fori_loop(0, 32, bit_body, jnp.full((1, TILE), INT_MIN, jnp.int32))
    room = topk - count(lambda kt: kt > t)

    ii = lax.broadcasted_iota(jnp.int32, (TILE, TILE), 0)
    jj = lax.broadcasted_iota(jnp.int32, (TILE, TILE), 1)
    lower = jnp.where(jj < ii, 1.0, 0.0).astype(BF16)
    ones8 = jnp.ones((8, TILE), BF16)

    def sel_body(u, carry):
        for k in range(_KEY_UNROLL):
            off = pl.multiple_of(u * ST + k * TILE, TILE)
            kt = skey_ref[pl.ds(off, TILE), :]
            eq = kt == t
            eqb = jnp.where(eq, 1.0, 0.0).astype(BF16)
            before = jnp.dot(lower, eqb, preferred_element_type=F32) + carry
            tot = jnp.dot(ones8, eqb, preferred_element_type=F32)[0:1]
            sel = jnp.where(kt > t, 1.0, jnp.where(eq, jnp.where(before < room, 1.0, 0.0), 0.0))
            sel_ref[pl.ds(off, TILE), :] = jnp.where(off + ii <= n * TILE + jj, sel, 0.0)
            carry = carry + tot
        return carry

    lax.fori_loop(0, ns, sel_body, jnp.zeros((1, TILE), F32))

    q = qT_ref[0]
    qgs = [jnp.concatenate([q[(g * G + r) * HEAD_DIM:(g * G + r + 1) * HEAD_DIM, :] for r in range(G)], axis=1)
           for g in range(KV_HEADS_A)]
    m_ref[...] = jnp.full(m_ref.shape, NEG, F32)
    l_ref[...] = jnp.zeros(l_ref.shape, F32)
    acc_ref[...] = jnp.zeros(acc_ref.shape, F32)

    AT = _ATT_TILES * TILE

    def att_body(u, carry):
        off = pl.multiple_of(u * AT, AT)
        tiles = [u * _ATT_TILES + k for k in range(_ATT_TILES)]
        roffs = [pl.multiple_of(jnp.where(j == n, 2 * TILE, jnp.where(j == n - 1, TILE, 0)), TILE) for j in tiles]
        selv = sel_ref[pl.ds(off, AT), :] > 0.5
        ss = [jnp.dot(kh_ref[0, g, pl.ds(off, AT), :], qgs[g], preferred_element_type=F32)
              for g in range(KV_HEADS_A)]
        for g in range(KV_HEADS_A):
            s = ss[g]
            s = jnp.concatenate(
                [jnp.where(selv,
                           s[:, r * TILE:(r + 1) * TILE]
                           + jnp.concatenate([toep_ref[g * G + r, pl.ds(ro, TILE), :] for ro in roffs], axis=0),
                           NEG)
                 for r in range(G)], axis=1)
            m = m_ref[g]
            m_new = jnp.maximum(m, jnp.max(s, axis=0, keepdims=True))
            alpha = jnp.exp(m - m_new)
            p = jnp.exp(s - m_new[0:1])
            l_ref[g] = alpha * l_ref[g] + jnp.sum(p, axis=0, keepdims=True)
            m_ref[g] = m_new
            vt = vT_ref[0, g * HEAD_DIM:(g + 1) * HEAD_DIM, pl.ds(off, AT)]
            acc_ref[g] = acc_ref[g] * alpha[0:1] + jnp.dot(vt, p.astype(BF16), preferred_element_type=F32)
        return carry

    lax.fori_loop(0, n // _ATT_TILES + 1, att_body, 0)
    for g in range(KV_HEADS_A):
        out = acc_ref[g] / l_ref[g][0:1]
        for r in range(G):
            h = g * G + r
            o_ref[0, h * HEAD_DIM:(h + 1) * HEAD_DIM, :] = out[:, r * TILE:(r + 1) * TILE].astype(o_ref.dtype)


def _dsa_prompt(qiT, wiT, kib, qT, kh, vT, toep, topk):
    B, hq, S = qT.shape
    nq = S // TILE
    assert nq % _KEY_UNROLL == 0
    gw = (N_HEADS // KV_HEADS_A) * TILE
    return pl.pallas_call(
        functools.partial(_dsa_prompt_kernel, topk),
        grid=(B, nq),
        in_specs=[pl.BlockSpec((1, qiT.shape[1], TILE), lambda b, n: (b, 0, n)),
                  pl.BlockSpec((1, IDX_HEADS, TILE), lambda b, n: (b, 0, n)),
                  pl.BlockSpec((1, S, IDX_DIM), lambda b, n: (b, 0, 0)),
                  pl.BlockSpec((1, hq, TILE), lambda b, n: (b, 0, n)),
                  pl.BlockSpec((1, KV_HEADS_A, S, HEAD_DIM), lambda b, n: (b, 0, 0, 0)),
                  pl.BlockSpec((1, KV_HEADS_A * HEAD_DIM, S), lambda b, n: (b, 0, 0)),
                  pl.BlockSpec(toep.shape, lambda b, n: (0, 0, 0))],
        out_specs=pl.BlockSpec((1, hq, TILE), lambda b, n: (b, 0, n)),
        out_shape=jax.ShapeDtypeStruct((B, hq, S), BF16),
        scratch_shapes=[pltpu.VMEM((S, TILE), jnp.int32), pltpu.VMEM((S, TILE), F32),
                        pltpu.VMEM((KV_HEADS_A, 8, gw), F32), pltpu.VMEM((KV_HEADS_A, 8, gw), F32),
                        pltpu.VMEM((KV_HEADS_A, HEAD_DIM, gw), F32)],
        compiler_params=_cparams(("arbitrary", "arbitrary")),
        name="dsa_prompt",
    )(qiT, wiT, kib, qT, kh, vT, toep)


def _swa_prompt_kernel(sink_ref, qT_ref, khp_ref, khc_ref, vTp_ref, vTc_ref, toep_ref, o_ref):
    n = pl.program_id(1)
    G = N_HEADS // KV_HEADS_B
    rows = lax.broadcasted_iota(jnp.int32, (2 * TILE, TILE), 0)
    lanes = lax.broadcasted_iota(jnp.int32, (2 * TILE, TILE), 1)
    dj = rows - lanes
    inwin = jnp.where(dj >= TILE - WINDOW + 1, jnp.where(dj <= TILE, 1.0, 0.0), 0.0)
    inwin = jnp.where(rows >= TILE, inwin, jnp.where(n > 0, inwin, 0.0))
    maskb = jnp.concatenate([inwin] * G, axis=1) > 0.5
    q = qT_ref[0]
    for g in range(KV_HEADS_B):
        heads = [g * G + r for r in range(G)]
        qg = jnp.concatenate([q[h * HEAD_DIM:(h + 1) * HEAD_DIM, :] for h in heads], axis=1)
        kk = jnp.concatenate([khp_ref[0, g], khc_ref[0, g]], axis=0)
        s = jnp.dot(kk, qg, preferred_element_type=F32)
        bias = jnp.concatenate([toep_ref[h, TILE:3 * TILE, :] for h in heads], axis=1)
        s = jnp.where(maskb, s + bias, NEG)
        sink = jnp.concatenate([jnp.full((1, TILE), sink_ref[h], F32) for h in heads], axis=1)
        m = jnp.maximum(jnp.max(s, axis=0, keepdims=True), sink)
        p = jnp.exp(s - m)
        l = jnp.sum(p, axis=0, keepdims=True) + jnp.exp(sink - m)
        vv = jnp.concatenate([vTp_ref[0, g * HEAD_DIM:(g + 1) * HEAD_DIM, :],
                              vTc_ref[0, g * HEAD_DIM:(g + 1) * HEAD_DIM, :]], axis=1)
        out = jnp.dot(vv, p.astype(BF16), preferred_element_type=F32) / l
        for r, h in enumerate(heads):
            o_ref[0, h * HEAD_DIM:(h + 1) * HEAD_DIM, :] = out[:, r * TILE:(r + 1) * TILE].astype(o_ref.dtype)


def _swa_prompt(qT, kh, vT, toep, sink):
    B, hq, S = qT.shape
    nb = S // TILE
    prev = lambda n: jnp.maximum(n - 1, 0)
    return pl.pallas_call(
        _swa_prompt_kernel,
        grid=(B, nb),
        in_specs=[pl.BlockSpec(memory_space=pltpu.SMEM),
                  pl.BlockSpec((1, hq, TILE), lambda b, n: (b, 0, n)),
                  pl.BlockSpec((1, KV_HEADS_B, TILE, HEAD_DIM), lambda b, n: (b, 0, prev(n), 0)),
                  pl.BlockSpec((1, KV_HEADS_B, TILE, HEAD_DIM), lambda b, n: (b, 0, n, 0)),
                  pl.BlockSpec((1, KV_HEADS_B * HEAD_DIM, TILE), lambda b, n: (b, 0, prev(n))),
                  pl.BlockSpec((1, KV_HEADS_B * HEAD_DIM, TILE), lambda b, n: (b, 0, n)),
                  pl.BlockSpec(toep.shape, lambda b, n: (0, 0, 0))],
        out_specs=pl.BlockSpec((1, hq, TILE), lambda b, n: (b, 0, n)),
        out_shape=jax.ShapeDtypeStruct((B, hq, S), BF16),
        compiler_params=_cparams(("arbitrary", "arbitrary")),
        name="swa_prompt",
    )(sink, qT, kh, kh, vT, vT, toep)


def _outproj_ln_kernel(alpha, x_ref, a_ref, w_ref, g_ref, b_ref, o_ref):
    y = alpha * x_ref[...] + jnp.dot(a_ref[...], w_ref[...], preferred_element_type=F32)
    o_ref[...] = _ln(y, g_ref[...], b_ref[...])


def _outproj_ln(x, a, w, g, b, alpha, tm=512):
    M, D = x.shape
    tm = min(tm, M)
    return pl.pallas_call(
        functools.partial(_outproj_ln_kernel, alpha),
        grid=(M // tm,),
        in_specs=[pl.BlockSpec((tm, D), lambda m: (m, 0)),
                  pl.BlockSpec((tm, a.shape[1]), lambda m: (m, 0)),
                  pl.BlockSpec(w.shape, lambda m: (0, 0)),
                  pl.BlockSpec((1, D), lambda m: (0, 0)),
                  pl.BlockSpec((1, D), lambda m: (0, 0))],
        out_specs=pl.BlockSpec((tm, D), lambda m: (m, 0)),
        out_shape=jax.ShapeDtypeStruct((M, D), F32),
        compiler_params=_cparams(("arbitrary",)),
        name="outproj_ln",
    )(x, a, w, g.reshape(1, D), b.reshape(1, D))


def _mlp_ln_kernel(alpha, x_ref, w1_ref, w2_ref, g_ref, b_ref, o_ref, acc_ref):
    f = pl.program_id(1)

    @pl.when(f == 0)
    def _():
        acc_ref[...] = jnp.zeros_like(acc_ref)

    h = jnp.maximum(jnp.dot(x_ref[...].astype(BF16), w1_ref[...], preferred_element_type=F32), 0.0)
    acc_ref[...] += jnp.dot((h * h).astype(BF16), w2_ref[...], preferred_element_type=F32)

    @pl.when(f == pl.num_programs(1) - 1)
    def _():
        o_ref[...] = _ln(alpha * x_ref[...] + acc_ref[...], g_ref[...], b_ref[...])


def _mlp_ln(x, w1, w2, g, b, alpha, tm=1024, tf=512):
    M, D = x.shape
    FF = w1.shape[1]
    tm = min(tm, M)
    return pl.pallas_call(
        functools.partial(_mlp_ln_kernel, alpha),
        grid=(M // tm, FF // tf),
        in_specs=[pl.BlockSpec((tm, D), lambda m, f: (m, 0)),
                  pl.BlockSpec((D, tf), lambda m, f: (0, f)),
                  pl.BlockSpec((tf, D), lambda m, f: (f, 0)),
                  pl.BlockSpec((1, D), lambda m, f: (0, 0)),
                  pl.BlockSpec((1, D), lambda m, f: (0, 0))],
        out_specs=pl.BlockSpec((tm, D), lambda m, f: (m, 0)),
        out_shape=jax.ShapeDtypeStruct((M, D), F32),
        scratch_shapes=[pltpu.VMEM((tm, D), F32)],
        compiler_params=_cparams(("arbitrary", "arbitrary")),
        name="mlp_ln",
    )(x, w1, w2, g.reshape(1, D), b.reshape(1, D))


def _matmul_kernel(x_ref, w_ref, o_ref):
    o_ref[...] = jnp.dot(x_ref[...].astype(BF16), w_ref[...], preferred_element_type=F32)


def _matmul(x, w):
    M, K = x.shape
    N = w.shape[1]
    return pl.pallas_call(
        _matmul_kernel,
        grid=(1,),
        in_specs=[pl.BlockSpec((M, K), lambda i: (0, 0)), pl.BlockSpec((K, N), lambda i: (0, 0))],
        out_specs=pl.BlockSpec((M, N), lambda i: (0, 0)),
        out_shape=jax.ShapeDtypeStruct((M, N), F32),
        compiler_params=_cparams(("arbitrary",)),
        name="sample_proj",
    )(x, w)


_SEQ_BLOCK = 8


def _dec_index_kernel(topk, n_pages, pt_ref, qi_ref, wi_ref, kiraw_ref, g_ref, b_ref, cidx_ref,
                      selb_ref, newsel_ref, kiln_ref, buf, sem, sc_ref, snew_ref):
    step = pl.program_id(0)
    nseq = pl.num_programs(0) * _SEQ_BLOCK
    L = n_pages * TILE

    def page_copy(seq, slot, p):
        return pltpu.make_async_copy(cidx_ref.at[pt_ref[seq, p]], buf.at[slot, p], sem.at[slot])

    def start(seq, slot):
        for p in range(n_pages):
            page_copy(seq, slot, p).start()

    def wait(seq, slot):
        for p in range(n_pages):
            page_copy(seq, slot, p).wait()

    @pl.when(step == 0)
    def _():
        start(0, 0)

    kiln_ref[...] = _ln(kiraw_ref[...], g_ref[...], b_ref[...])

    def seq_body(i, carry):
        seq = step * _SEQ_BLOCK + i
        slot = i % 2

        @pl.when(seq + 1 < nseq)
        def _():
            start(seq + 1, 1 - slot)

        wait(seq, slot)
        xk = jnp.concatenate([buf[slot, p] for p in range(n_pages)], axis=1).astype(BF16)
        qib = qi_ref[i].astype(BF16)
        s = jnp.dot(qib, xk, preferred_element_type=F32)
        w = wi_ref[i]
        row = jnp.sum(jnp.maximum(s, 0.0) * w, axis=0, keepdims=True) * IDX_SCALE
        sc_ref[pl.ds(i, 1), :] = jnp.where(row == 0.0, 0.0, row)
        kn = kiln_ref[pl.ds(i, 1), :].astype(BF16).astype(F32)
        sn = jnp.sum(qib.astype(F32) * kn, axis=1, keepdims=True)
        snew = jnp.sum(jnp.maximum(sn, 0.0) * w, axis=0, keepdims=True) * IDX_SCALE
        snew_ref[pl.ds(i, 1), :] = jnp.broadcast_to(jnp.where(snew == 0.0, 0.0, snew), (1, TILE))
        return carry

    lax.fori_loop(0, _SEQ_BLOCK, seq_body, 0)
    keys = _sortable(sc_ref[...])
    knew = _sortable(snew_ref[:, 0:1])

    def count(pred):
        return (jnp.sum(jnp.where(pred(keys), 1.0, 0.0), axis=1, keepdims=True)
                + jnp.where(pred(knew), 1.0, 0.0))

    def bit_body(b, t):
        cand = t ^ lax.shift_left(jnp.int32(1), 31 - b)
        return jnp.where(count(lambda k: k >= cand) >= topk, cand, t)

    t = lax.fori_loop(0, 32, bit_body, jnp.full((_SEQ_BLOCK, 1), INT_MIN, jnp.int32))
    room = topk - count(lambda k: k > t)
    ii = lax.broadcasted_iota(jnp.int32, (TILE, 2 * TILE), 0)
    jj = lax.broadcasted_iota(jnp.int32, (TILE, 2 * TILE), 1)
    pref = jnp.where(jj >= TILE, 1.0, jnp.where(ii < jj, 1.0, 0.0)).astype(BF16)
    carry = jnp.zeros((_SEQ_BLOCK, TILE), F32)
    for jt in range(n_pages):
        kt = keys[:, jt * TILE:(jt + 1) * TILE]
        eq = kt == t
        res = jnp.dot(jnp.where(eq, 1.0, 0.0).astype(BF16), pref, preferred_element_type=F32)
        take = jnp.where(res[:, :TILE] + carry < room, 1.0, 0.0)
        selb_ref[:, jt * TILE:(jt + 1) * TILE] = jnp.where(kt > t, 1.0, jnp.where(eq, take, 0.0))
        carry = carry + res[:, TILE:]
    newsel = jnp.where(knew > t, 1.0, jnp.where(knew == t, jnp.where(carry[:, :1] < room, 1.0, 0.0), 0.0))
    newsel_ref[...] = jnp.broadcast_to(newsel, (_SEQ_BLOCK, TILE))


def _dec_index(page_table, qi, wi, kiraw, g, b, cidx, topk):
    DB, n_pages = page_table.shape
    L = n_pages * TILE
    sb = _SEQ_BLOCK
    grid_spec = pltpu.PrefetchScalarGridSpec(
        num_scalar_prefetch=1,
        grid=(DB // sb,),
        in_specs=[pl.BlockSpec((sb, IDX_HEADS, IDX_DIM), lambda s, pt: (s, 0, 0)),
                  pl.BlockSpec((sb, IDX_HEADS, 1), lambda s, pt: (s, 0, 0)),
                  pl.BlockSpec((sb, IDX_DIM), lambda s, pt: (s, 0)),
                  pl.BlockSpec((1, IDX_DIM), lambda s, pt: (0, 0)),
                  pl.BlockSpec((1, IDX_DIM), lambda s, pt: (0, 0)),
                  pl.BlockSpec(memory_space=pl.ANY)],
        out_specs=[pl.BlockSpec((sb, L), lambda s, pt: (s, 0)),
                   pl.BlockSpec((sb, TILE), lambda s, pt: (s, 0)),
                   pl.BlockSpec((sb, IDX_DIM), lambda s, pt: (s, 0))],
        scratch_shapes=[pltpu.VMEM((2, n_pages, IDX_DIM, TILE), F32),
                        pltpu.SemaphoreType.DMA((2,)),
                        pltpu.VMEM((sb, L), F32),
                        pltpu.VMEM((sb, TILE), F32)])
    return pl.pallas_call(
        functools.partial(_dec_index_kernel, topk, n_pages),
        grid_spec=grid_spec,
        out_shape=[jax.ShapeDtypeStruct((DB, L), F32),
                   jax.ShapeDtypeStruct((DB, TILE), F32),
                   jax.ShapeDtypeStruct((DB, IDX_DIM), F32)],
        compiler_params=_cparams(("arbitrary",)),
        name="dec_index",
    )(page_table, qi, wi, kiraw, g.reshape(1, -1), b.reshape(1, -1), cidx)


_PAGE_CHUNK = 8


def _dec_attn_kernel(n_pages, pt_ref, q_ref, selb_ref, newsel_ref, kn_ref, vn_ref, dec_ref, ck_ref, cv_ref,
                     o_ref, kbuf, vbuf, sem):
    b = pl.program_id(0)
    nb = pl.num_programs(0)
    nch = n_pages // _PAGE_CHUNK
    CL = _PAGE_CHUNK * TILE
    kvd = KV_HEADS_A * HEAD_DIM

    def copies(seq, c, slot):
        out = []
        for p in range(_PAGE_CHUNK):
            phys = pt_ref[seq, c * _PAGE_CHUNK + p]
            out.append(pltpu.make_async_copy(ck_ref.at[phys], kbuf.at[slot, p], sem.at[0, slot]))
            out.append(pltpu.make_async_copy(cv_ref.at[phys], vbuf.at[slot, p], sem.at[1, slot]))
        return out

    def start(seq, c, slot):
        for cp in copies(seq, c, slot):
            cp.start()

    def wait(seq, c, slot):
        for cp in copies(seq, c, slot):
            cp.wait()

    @pl.when(b == 0)
    def _():
        start(0, 0, 0)

    qb = q_ref[0]
    far = dec_ref[1]
    near = dec_ref[0]

    def chunk_body(c, carry):
        m, l, acc = carry
        slot = c % 2
        last = c + 1 == nch
        nxt_b = jnp.where(last, b + 1, b)
        nxt_c = jnp.where(last, 0, c + 1)

        @pl.when(nxt_b < nb)
        def _():
            start(nxt_b, nxt_c, 1 - slot)

        wait(b, c, slot)
        kc = jnp.concatenate([kbuf[slot, p] for p in range(_PAGE_CHUNK)], axis=1).astype(BF16)
        vc = jnp.concatenate([vbuf[slot, p] for p in range(_PAGE_CHUNK)], axis=1).astype(BF16)
        s = jnp.dot(qb, kc, preferred_element_type=F32)
        bias = jnp.concatenate([far] * (_PAGE_CHUNK - 1) + [jnp.where(last, near, far)], axis=1)
        sel = selb_ref[0, :, pl.ds(pl.multiple_of(c * CL, CL), CL)] > 0.5
        s = jnp.where(sel, s + bias, NEG)
        m_new = jnp.maximum(m, jnp.max(s, axis=1, keepdims=True))
        alpha = jnp.exp(m - m_new)
        p = jnp.exp(s - m_new)
        l_new = alpha * l + jnp.sum(p, axis=1, keepdims=True)
        acc_new = alpha * acc + lax.dot_general(p.astype(BF16), vc, _NT, preferred_element_type=F32)
        return m_new, l_new, acc_new

    init = (jnp.full((N_HEADS, 1), NEG, F32), jnp.zeros((N_HEADS, 1), F32), jnp.zeros((N_HEADS, kvd), F32))
    m, l, acc = lax.fori_loop(0, nch, chunk_body, init)
    kn = kn_ref[0].astype(BF16).astype(F32)
    sn = jnp.sum(qb.astype(F32) * kn, axis=1, keepdims=True) + dec_ref[2][:, 0:1]
    sn = jnp.where(newsel_ref[0, :, 0:1] > 0.5, sn, NEG)
    m_new = jnp.maximum(m, sn)
    alpha = jnp.exp(m - m_new)
    pn = jnp.exp(sn - m_new)
    l = alpha * l + pn
    acc = alpha * acc + pn * vn_ref[0]
    o_ref[0] = acc / l


def _dec_attn(page_table, qbd, selb, newsel, kn, vn, dec, ck, cv):
    DB, n_pages = page_table.shape
    L = n_pages * TILE
    kvd = KV_HEADS_A * HEAD_DIM
    assert n_pages % (2 * _PAGE_CHUNK) == 0
    grid_spec = pltpu.PrefetchScalarGridSpec(
        num_scalar_prefetch=1,
        grid=(DB,),
        in_specs=[pl.BlockSpec((1, N_HEADS, kvd), lambda s, pt: (s, 0, 0)),
                  pl.BlockSpec((1, 1, L), lambda s, pt: (s, 0, 0)),
                  pl.BlockSpec((1, 1, TILE), lambda s, pt: (s, 0, 0)),
                  pl.BlockSpec((1, 1, kvd), lambda s, pt: (s, 0, 0)),
                  pl.BlockSpec((1, 1, kvd), lambda s, pt: (s, 0, 0)),
                  pl.BlockSpec(dec.shape, lambda s, pt: (0, 0, 0)),
                  pl.BlockSpec(memory_space=pl.ANY),
                  pl.BlockSpec(memory_space=pl.ANY)],
        out_specs=pl.BlockSpec((1, N_HEADS, kvd), lambda s, pt: (s, 0, 0)),
        scratch_shapes=[pltpu.VMEM((2, _PAGE_CHUNK, kvd, TILE), F32),
                        pltpu.VMEM((2, _PAGE_CHUNK, kvd, TILE), F32),
                        pltpu.SemaphoreType.DMA((2, 2))])
    return pl.pallas_call(
        functools.partial(_dec_attn_kernel, n_pages),
        grid_spec=grid_spec,
        out_shape=jax.ShapeDtypeStruct((DB, N_HEADS, kvd), F32),
        compiler_params=_cparams(("arbitrary",)),
        name="dec_attn",
    )(page_table, qbd, selb.reshape(DB, 1, L), newsel.reshape(DB, 1, TILE),
      kn.reshape(DB, 1, kvd), vn.reshape(DB, 1, kvd), dec, ck, cv)


def _swa_dec_kernel(q_ref, wk_ref, wv_ref, kn_ref, vn_ref, dec_ref, sink_ref, o_ref):
    lane = lax.broadcasted_iota(jnp.int32, (N_HEADS, TILE), 1)
    bias = dec_ref[0]
    b0 = dec_ref[2][:, 0:1]
    sk = sink_ref[...]
    for i in range(_SEQ_BLOCK):
        qb = q_ref[i]
        s = jnp.dot(qb, wk_ref[i].astype(BF16), preferred_element_type=F32)
        s = jnp.where(lane >= 1, s + bias, NEG)
        kn = kn_ref[i].astype(BF16).astype(F32)
        sn = jnp.sum(qb.astype(F32) * kn, axis=1, keepdims=True) + b0
        m = jnp.maximum(jnp.maximum(jnp.max(s, axis=1, keepdims=True), sn), sk)
        p = jnp.exp(s - m)
        pn = jnp.exp(sn - m)
        l = jnp.sum(p, axis=1, keepdims=True) + pn + jnp.exp(sk - m)
        out = lax.dot_general(p.astype(BF16), wv_ref[i].astype(BF16), _NT, preferred_element_type=F32) + pn * vn_ref[i]
        o_ref[i] = out / l


def _swa_dec(qbd, wk, wv, kn, vn, dec, sink):
    DB = qbd.shape[0]
    kvd = KV_HEADS_B * HEAD_DIM
    sb = _SEQ_BLOCK
    return pl.pallas_call(
        _swa_dec_kernel,
        grid=(DB // sb,),
        in_specs=[pl.BlockSpec((sb, N_HEADS, kvd), lambda s: (s, 0, 0)),
                  pl.BlockSpec((sb, kvd, WINDOW), lambda s: (s, 0, 0)),
                  pl.BlockSpec((sb, kvd, WINDOW), lambda s: (s, 0, 0)),
                  pl.BlockSpec((sb, 1, kvd), lambda s: (s, 0, 0)),
                  pl.BlockSpec((sb, 1, kvd), lambda s: (s, 0, 0)),
                  pl.BlockSpec(dec.shape, lambda s: (0, 0, 0)),
                  pl.BlockSpec((N_HEADS, 1), lambda s: (0, 0))],
        out_specs=pl.BlockSpec((sb, N_HEADS, kvd), lambda s: (s, 0, 0)),
        out_shape=jax.ShapeDtypeStruct((DB, N_HEADS, kvd), F32),
        compiler_params=_cparams(("arbitrary",)),
        name="swa_dec",
    )(qbd, wk, wv, kn.reshape(DB, 1, kvd), vn.reshape(DB, 1, kvd), dec, sink.reshape(N_HEADS, 1))


def _block_diag_q(q, nkv):
    DB = q.shape[0]
    G = N_HEADS // nkv
    qh = (q * ATTN_SCALE).reshape(DB, N_HEADS, 1, HEAD_DIM)
    own = (np.arange(N_HEADS)[:, None] // G == np.arange(nkv)[None, :]).astype(np.float32)
    return (qh * own[None, :, :, None]).reshape(DB, N_HEADS, nkv * HEAD_DIM).astype(BF16)


def _own_group(o, nkv):
    DB = o.shape[0]
    G = N_HEADS // nkv
    o5 = o.reshape(DB, nkv, G, nkv, HEAD_DIM)
    return jnp.stack([o5[:, g, :, g, :] for g in range(nkv)], axis=1).reshape(DB, N_HEADS * HEAD_DIM)


def _pad_cols(w, mult=TILE):
    pad = (-w.shape[1]) % mult
    return jnp.pad(w, ((0, 0), (0, pad))) if pad else w


def _slots_minor(a):
    n, t, h, d = a.shape
    return a.transpose(0, 2, 3, 1).reshape(n, h * d, t)


def _slots_major(a, heads):
    n, hd, t = a.shape
    return a.reshape(n, heads, hd // heads, t).transpose(0, 3, 1, 2)


def kernel(x_prompt, x_sample, cache_k, cache_v, cache_idx_k, state_win_k, state_win_v, page_table, rel_bias_table, a_w_in, a_w_out, a_idx_ln_g, a_idx_ln_b, b_w_in, b_w_out, b_sink, ln1_g, ln1_b, ln2_g, ln2_b, mlp_w1, mlp_w2):
    B, S, D = x_prompt.shape
    DB, DS, _ = x_sample.shape
    depth = ln1_g.shape[0]
    assert depth == 2 and DS == 1 and S % TILE == 0 and DB % _SEQ_BLOCK == 0
    assert state_win_k.shape[2] == WINDOW and cache_k.shape[2] == TILE
    alpha = (2 * depth) ** 0.25
    n_pages = page_table.shape[1]
    past = n_pages * TILE
    hq = N_HEADS * HEAD_DIM
    kva = KV_HEADS_A * HEAD_DIM
    kvb = KV_HEADS_B * HEAD_DIM
    hi = IDX_HEADS * IDX_DIM

    toep, dec = _bias_tables(rel_bias_table)

    wa = a_w_in[0]
    c0, c1, c2, c3, c4 = hq, hq + kva, hq + 2 * kva, hq + 2 * kva + hi, hq + 2 * kva + hi + IDX_DIM
    wq, wk, wv, wqi, wki, wwi = wa[:, :c0], wa[:, c0:c1], wa[:, c1:c2], wa[:, c2:c3], wa[:, c3:c4], wa[:, c4:]
    a_wo = a_w_out[0].astype(BF16)
    w1 = [mlp_w1[i].astype(BF16) for i in range(depth)]
    w2 = [mlp_w2[i].astype(BF16) for i in range(depth)]

    qT, kTp, vTp, kh, vT, qiT, kiTp, kib, wiT = _prompt_proj(
        x_prompt, wq, wk, wv, KV_HEADS_A, idx=(wqi, wki, wwi, a_idx_ln_g[0], a_idx_ln_b[0]))
    oT = _dsa_prompt(qiT, wiT, kib, qT, kh, vT, toep, min(TOPK_MAX, S // 4))
    attn = oT.transpose(0, 2, 1).reshape(B * S, hq)
    y_p = _outproj_ln(x_prompt.reshape(B * S, D), attn, a_wo, ln1_g[0], ln1_b[0], alpha)
    y_p = _mlp_ln(y_p, w1[0], w2[0], ln2_g[0], ln2_b[0], alpha)

    xs = x_sample.reshape(DB, D)
    proj = _matmul(xs, _pad_cols(wa).astype(BF16))
    q_s, k_s, v_s = proj[:, :c0], proj[:, c0:c1], proj[:, c1:c2]
    qi_s, kiraw_s, wi_s = proj[:, c2:c3], proj[:, c3:c4], proj[:, c4:c4 + IDX_HEADS]
    selb, newsel, ki_s = _dec_index(
        page_table, qi_s.reshape(DB, IDX_HEADS, IDX_DIM), wi_s.reshape(DB, IDX_HEADS, 1), kiraw_s,
        a_idx_ln_g[0], a_idx_ln_b[0], cache_idx_k[0].transpose(0, 2, 1), min(TOPK_MAX, (past + DS) // 4))
    o_s = _dec_attn(page_table, _block_diag_q(q_s, KV_HEADS_A), selb, newsel, k_s, v_s, dec,
                    _slots_minor(cache_k[0]), _slots_minor(cache_v[0]))
    y_s = _outproj_ln(xs, _own_group(o_s, KV_HEADS_A).astype(BF16), a_wo, ln1_g[0], ln1_b[0], alpha)
    y_s = _mlp_ln(y_s, w1[0], w2[0], ln2_g[0], ln2_b[0], alpha)

    wb = b_w_in[0]
    bq, bk, bv = wb[:, :hq], wb[:, hq:hq + kvb], wb[:, hq + kvb:]
    b_wo = b_w_out[0].astype(BF16)

    qT1, k1Tp, v1Tp, kh1, vT1 = _prompt_proj(y_p.reshape(B, S, D), bq, bk, bv, KV_HEADS_B)
    oT1 = _swa_prompt(qT1, kh1, vT1, toep, b_sink[0])
    attn1 = oT1.transpose(0, 2, 1).reshape(B * S, hq)
    y_p = _outproj_ln(y_p, attn1, b_wo, ln1_g[1], ln1_b[1], alpha)
    y_p = _mlp_ln(y_p, w1[1], w2[1], ln2_g[1], ln2_b[1], alpha)

    proj1 = _matmul(y_s, wb.astype(BF16))
    q1_s, k1_s, v1_s = proj1[:, :hq], proj1[:, hq:hq + kvb], proj1[:, hq + kvb:]
    win_k = _slots_minor(state_win_k[0])
    win_v = _slots_minor(state_win_v[0])
    o1_s = _swa_dec(_block_diag_q(q1_s, KV_HEADS_B), win_k, win_v, k1_s, v1_s, dec, b_sink[0])
    y_s = _outproj_ln(y_s, _own_group(o1_s, KV_HEADS_B).astype(BF16), b_wo, ln1_g[1], ln1_b[1], alpha)
    y_s = _mlp_ln(y_s, w1[1], w2[1], ln2_g[1], ln2_b[1], alpha)

    npg = S // TILE
    assert min(WINDOW, S) == TILE
    pages = lambda a, heads: _slots_major(a.reshape(B * npg, -1, TILE), heads).reshape(1, B, npg, TILE, heads, HEAD_DIM)
    new_wk_s = jnp.concatenate([win_k[:, :, 1:], k1_s[:, :, None]], axis=2)
    new_wv_s = jnp.concatenate([win_v[:, :, 1:], v1_s[:, :, None]], axis=2)
    return (y_p.reshape(B, S, D),
            y_s.reshape(DB, DS, D),
            pages(kTp, KV_HEADS_A),
            pages(vTp, KV_HEADS_A),
            kiTp.transpose(0, 1, 3, 2).reshape(1, B, npg, TILE, IDX_DIM),
            k_s.reshape(1, DB, DS, KV_HEADS_A, HEAD_DIM),
            v_s.reshape(1, DB, DS, KV_HEADS_A, HEAD_DIM),
            ki_s.reshape(1, DB, DS, IDX_DIM),
            _slots_major(k1Tp[:, npg - 1], KV_HEADS_B)[None],
            _slots_major(v1Tp[:, npg - 1], KV_HEADS_B)[None],
            _slots_major(new_wk_s, KV_HEADS_B)[None],
            _slots_major(new_wv_s, KV_HEADS_B)[None])
```

```python
import functools
import math

import numpy as np
import jax
import jax.numpy as jnp
from jax import lax
from jax.experimental import pallas as pl
from jax.experimental.pallas import tpu as pltpu

F32 = jnp.float32
BF16 = jnp.bfloat16

N_HEADS = 16
HEAD_DIM = 64
KV_HEADS_A = 4
KV_HEADS_B = 2
IDX_HEADS = 8
IDX_DIM = 64
TOPK_MAX = 256
WINDOW = 128
TILE = 128
N_BUCKETS = 32
MAX_DISTANCE = 128
LN_EPS = 1e-5
NEG = -1e30
INT_MIN = -(2 ** 31)
ATTN_SCALE = HEAD_DIM ** -0.5
LOG2E = math.log2(math.e)
IDX_SCALE = IDX_HEADS ** -0.5 * IDX_DIM ** -0.5
FAR_BUCKET = N_BUCKETS - 1
FAR_DISTANCE = 113

_NT = (((1,), (1,)), ((), ()))
_VMEM_LIMIT = 48 * 1024 * 1024


def _cparams(sem):
    return pltpu.CompilerParams(dimension_semantics=sem, vmem_limit_bytes=_VMEM_LIMIT)


def _bucket_np(dist):
    n = np.maximum(dist, 0)
    max_exact = N_BUCKETS // 2
    nf = np.maximum(n, max_exact).astype(np.float64)
    val = np.log(nf / max_exact) / math.log(MAX_DISTANCE / max_exact) * (N_BUCKETS - max_exact)
    frac = val - np.floor(val)
    interior = (n > max_exact) & (n < MAX_DISTANCE)
    assert not np.any(interior & ((frac < 1e-6) | (frac > 1 - 1e-6))), "bucket boundary too close to an integer"
    large = np.minimum(max_exact + val.astype(np.int32), N_BUCKETS - 1)
    out = np.where(n < max_exact, n, large).astype(np.int32)
    assert np.all(out[n >= FAR_DISTANCE] == FAR_BUCKET)
    return out


def _key_to_float(k):
    return lax.bitcast_convert_type(k ^ (lax.shift_right_arithmetic(k, 31) & 0x7FFFFFFF), F32)


def _kth_largest_search(count_ge, topk, shape):
    def bit_body(b, carry):
        t, cnt_t = carry
        cand = t ^ lax.shift_left(jnp.int32(1), 31 - b)
        cnt = count_ge(_key_to_float(cand))
        ok = cnt >= topk
        return jnp.where(ok, cand, t), jnp.where(ok, cnt, cnt_t)

    init = (jnp.full(shape, INT_MIN, jnp.int32), jnp.full(shape, 3e38, F32))
    t, cnt_t = lax.fori_loop(0, 32, bit_body, init)
    return _key_to_float(t), cnt_t


def _ln(y, g, b):
    mu = jnp.mean(y, axis=-1, keepdims=True)
    yc = y - mu
    var = jnp.mean(yc * yc, axis=-1, keepdims=True)
    return yc * lax.rsqrt(var + LN_EPS) * g + b


def _bias_kernel(table_ref, bk_ref, bkd_ref, toep_ref, toep2_ref, dec_ref):
    bk = bk_ref[...]
    bkd = bkd_ref[...]
    dec_rows = []
    for h in range(N_HEADS):
        acc = jnp.zeros(bk.shape, F32)
        accd = jnp.zeros(bkd.shape, F32)
        for b in range(N_BUCKETS):
            val = table_ref[b, h]
            acc = jnp.where(bk == b, val, acc)
            accd = jnp.where(bkd == b, val, accd)
        toep_ref[h] = acc
        toep2_ref[h] = acc * LOG2E
        dec_rows.append(accd)
    for kind in range(3):
        dec_ref[kind] = jnp.concatenate([d[kind:kind + 1, :] for d in dec_rows], axis=0)


def _bias_tables(table):
    i = np.arange(TILE)[None, :]
    j = np.arange(2 * TILE)[:, None]
    bk = np.concatenate([np.full((TILE, TILE), FAR_BUCKET, np.int32), _bucket_np(i + TILE - j)], axis=0)
    bkd = np.zeros((8, TILE), np.int32)
    bkd[0] = _bucket_np(TILE - np.arange(TILE))
    bkd[1] = FAR_BUCKET
    bkd[2] = 0
    return pl.pallas_call(
        _bias_kernel,
        grid=(1,),
        in_specs=[pl.BlockSpec(memory_space=pltpu.SMEM),
                  pl.BlockSpec((3 * TILE, TILE), lambda h: (0, 0)),
                  pl.BlockSpec((8, TILE), lambda h: (0, 0))],
        out_specs=[pl.BlockSpec((N_HEADS, 3 * TILE, TILE), lambda h: (0, 0, 0)),
                   pl.BlockSpec((N_HEADS, 3 * TILE, TILE), lambda h: (0, 0, 0)),
                   pl.BlockSpec((3, N_HEADS, TILE), lambda h: (0, 0, 0))],
        out_shape=[jax.ShapeDtypeStruct((N_HEADS, 3 * TILE, TILE), F32),
                   jax.ShapeDtypeStruct((N_HEADS, 3 * TILE, TILE), F32),
                   jax.ShapeDtypeStruct((3, N_HEADS, TILE), F32)],
        compiler_params=_cparams(("arbitrary",)),
        name="bias_tables",
    )(table, jnp.asarray(bk), jnp.asarray(bkd))


def _normalize_cols(y):
    mu = jnp.mean(y, axis=0, keepdims=True)
    yc = y - mu
    var = jnp.mean(yc * yc, axis=0, keepdims=True)
    return yc * lax.rsqrt(var + LN_EPS)


def _proj_kernel(nkv, has_idx, *refs):
    if has_idx:
        (x_ref, wqT_ref, wkT_ref, wvT_ref, wkh_ref, wqiT_ref, wkiT_ref, wki_ref, wwiT_ref, g_ref, b_ref,
         gc_ref, bc_ref, qT_ref, kTp_ref, vTp_ref, kh_ref, vT_ref, qiT_ref, kiTp_ref, kib_ref, wiT_ref) = refs
    else:
        (x_ref, wqT_ref, wkT_ref, wvT_ref, wkh_ref, qT_ref, kTp_ref, vTp_ref, kh_ref, vT_ref) = refs
    xb = x_ref[0].astype(BF16)
    npg = xb.shape[0] // TILE
    qT_ref[0] = lax.dot_general(wqT_ref[...], xb, _NT, preferred_element_type=F32).astype(BF16)
    kT = lax.dot_general(wkT_ref[...], xb, _NT, preferred_element_type=F32)
    vT = lax.dot_general(wvT_ref[...], xb, _NT, preferred_element_type=F32)
    vT_ref[0] = vT.astype(BF16)
    for j in range(npg):
        kTp_ref[0, j] = kT[:, j * TILE:(j + 1) * TILE]
        vTp_ref[0, j] = vT[:, j * TILE:(j + 1) * TILE]
    for g in range(nkv):
        kh_ref[0, g] = jnp.dot(xb, wkh_ref[g], preferred_element_type=F32).astype(BF16)
    if has_idx:
        qiT_ref[0] = lax.dot_general(wqiT_ref[...], xb, _NT, preferred_element_type=F32).astype(BF16)
        kiT = _normalize_cols(lax.dot_general(wkiT_ref[...], xb, _NT, preferred_element_type=F32))
        for j in range(npg):
            kiTp_ref[0, j] = kiT[:, j * TILE:(j + 1) * TILE] * gc_ref[...] + bc_ref[...]
        ki = _ln(jnp.dot(xb, wki_ref[...], preferred_element_type=F32), g_ref[...], b_ref[...])
        kib_ref[0] = ki.astype(BF16)
        wiT_ref[0] = lax.dot_general(wwiT_ref[...], xb, _NT, preferred_element_type=F32)


def _prompt_proj(x, wq, wk, wv, nkv, idx=None, q_scale=ATTN_SCALE, tm=512):
    B, S, D = x.shape
    tm = min(tm, S)
    kvd = nkv * HEAD_DIM
    hq = N_HEADS * HEAD_DIM
    wqT = (wq * q_scale).T.astype(BF16)
    wkh = wk.reshape(D, nkv, HEAD_DIM).transpose(1, 0, 2).astype(BF16)
    full2 = lambda a: pl.BlockSpec(a.shape, lambda b, m: (0,) * a.ndim)
    ins = [x, wqT, wk.T.astype(BF16), wv.T.astype(BF16), wkh]
    in_specs = [pl.BlockSpec((1, tm, D), lambda b, m: (b, m, 0))] + [full2(a) for a in ins[1:]]
    out_shape = [jax.ShapeDtypeStruct((B, hq, S), BF16),
                 jax.ShapeDtypeStruct((B, S // TILE, kvd, TILE), F32),
                 jax.ShapeDtypeStruct((B, S // TILE, kvd, TILE), F32),
                 jax.ShapeDtypeStruct((B, nkv, S, HEAD_DIM), BF16),
                 jax.ShapeDtypeStruct((B, kvd, S), BF16)]
    out_specs = [pl.BlockSpec((1, hq, tm), lambda b, m: (b, 0, m)),
                 pl.BlockSpec((1, tm // TILE, kvd, TILE), lambda b, m: (b, m, 0, 0)),
                 pl.BlockSpec((1, tm // TILE, kvd, TILE), lambda b, m: (b, m, 0, 0)),
                 pl.BlockSpec((1, nkv, tm, HEAD_DIM), lambda b, m: (b, 0, m, 0)),
                 pl.BlockSpec((1, kvd, tm), lambda b, m: (b, 0, m))]
    if idx is not None:
        wqi, wki, wwi, g, bb = idx
        extra = [wqi.T.astype(BF16), wki.T.astype(BF16), wki.astype(BF16), wwi.T.astype(BF16),
                 g.reshape(1, -1), bb.reshape(1, -1),
                 jnp.broadcast_to(g.reshape(-1, 1), (IDX_DIM, TILE)), jnp.broadcast_to(bb.reshape(-1, 1), (IDX_DIM, TILE))]
        ins += extra
        in_specs += [full2(a) for a in extra]
        hi = IDX_HEADS * IDX_DIM
        out_shape += [jax.ShapeDtypeStruct((B, hi, S), BF16),
                      jax.ShapeDtypeStruct((B, S // TILE, IDX_DIM, TILE), F32),
                      jax.ShapeDtypeStruct((B, S, IDX_DIM), BF16),
                      jax.ShapeDtypeStruct((B, IDX_HEADS, S), F32)]
        out_specs += [pl.BlockSpec((1, hi, tm), lambda b, m: (b, 0, m)),
                      pl.BlockSpec((1, tm // TILE, IDX_DIM, TILE), lambda b, m: (b, m, 0, 0)),
                      pl.BlockSpec((1, tm, IDX_DIM), lambda b, m: (b, m, 0)),
                      pl.BlockSpec((1, IDX_HEADS, tm), lambda b, m: (b, 0, m))]
    return pl.pallas_call(
        functools.partial(_proj_kernel, nkv, idx is not None),
        grid=(B, S // tm),
        in_specs=in_specs,
        out_specs=out_specs,
        out_shape=out_shape,
        compiler_params=_cparams(("arbitrary", "arbitrary")),
        name="prompt_proj_idx" if idx is not None else "prompt_proj",
    )(*ins)


_KEY_UNROLL = 4
_SUM_ROWS = 16
_ATT_TILES = 2


def _dsa_prompt_kernel(topk, qiT_ref, wiT_ref, ki_ref, qT_ref, kh_ref, vT_ref, toep_ref, o_ref,
                       skey_ref, sel_ref, m_ref, acc_ref):
    n = pl.program_id(1)
    nk = n + 1
    ST = _KEY_UNROLL * TILE
    ns = n // _KEY_UNROLL + 1
    G = N_HEADS // KV_HEADS_A
    rows = lax.broadcasted_iota(jnp.int32, (ST, TILE), 0)
    lanes = lax.broadcasted_iota(jnp.int32, (ST, TILE), 1)
    qpos = n * TILE + lanes

    qi = qiT_ref[0]
    wi = wiT_ref[0]

    def score_body(u, carry):
        off = pl.multiple_of(u * ST, ST)
        kij = ki_ref[0, pl.ds(off, ST), :]
        acc = jnp.zeros((ST, TILE), F32)
        for h in range(0, IDX_HEADS, 2):
            qpair = jnp.concatenate([qi[IDX_DIM * h:IDX_DIM * (h + 1), :],
                                     qi[IDX_DIM * (h + 1):IDX_DIM * (h + 2), :]], axis=1)
            s = jnp.dot(kij, qpair, preferred_element_type=F32)
            acc = acc + jnp.maximum(s[:, :TILE], 0.0) * wi[h:h + 1, :]
            acc = acc + jnp.maximum(s[:, TILE:], 0.0) * wi[h + 1:h + 2, :]
        sc = acc * IDX_SCALE
        sc = jnp.where(off + rows <= qpos, sc, NEG)
        sc = jnp.where(sc == 0.0, 0.0, sc)
        skey_ref[pl.ds(off, ST), :] = sc
        return carry

    lax.fori_loop(0, ns, score_body, 0)

    def count(pred):
        def body(u, c):
            off = pl.multiple_of(u * ST, ST)
            x = jnp.where(pred(skey_ref[pl.ds(off, ST), :]), 1.0, 0.0).reshape(ST // 8, 8, TILE)
            while x.shape[0] > 1:
                half = x.shape[0] // 2
                x = x[:half] + x[half:]
            return c + x[0]
        c8 = lax.fori_loop(0, ns, body, jnp.zeros((8, TILE), F32))
        return jnp.sum(c8, axis=0, keepdims=True)

    t, cnt_t = _kth_largest_search(lambda c: count(lambda kt: kt >= c), topk, (1, TILE))
    exact_fit = jnp.max(jnp.abs(cnt_t - topk)) == 0.0

    @pl.when(exact_fit)
    def _():
        def sel_body(u, carry):
            off = pl.multiple_of(u * ST, ST)
            sel = jnp.where(skey_ref[pl.ds(off, ST), :] >= t, 1.0, 0.0)
            sel_ref[pl.ds(off, ST), :] = jnp.where(off + rows <= qpos, sel, 0.0)
            return carry

        lax.fori_loop(0, ns, sel_body, 0)

    @pl.when(jnp.logical_not(exact_fit))
    def _():
        room = topk - count(lambda kt: kt > t)
        ii = lax.broadcasted_iota(jnp.int32, (TILE, TILE), 0)
        jj = lax.broadcasted_iota(jnp.int32, (TILE, TILE), 1)
        lower = jnp.where(jj < ii, 1.0, 0.0).astype(BF16)
        ones8 = jnp.ones((8, TILE), BF16)

        def sel_body(u, carry):
            for k in range(_KEY_UNROLL):
                off = pl.multiple_of(u * ST + k * TILE, TILE)
                kt = skey_ref[pl.ds(off, TILE), :]
                eq = kt == t
                eqb = jnp.where(eq, 1.0, 0.0).astype(BF16)
                before = jnp.dot(lower, eqb, preferred_element_type=F32) + carry
                tot = jnp.dot(ones8, eqb, preferred_element_type=F32)[0:1]
                sel = jnp.where(kt > t, 1.0, jnp.where(eq, jnp.where(before < room, 1.0, 0.0), 0.0))
                sel_ref[pl.ds(off, TILE), :] = jnp.where(off + ii <= n * TILE + jj, sel, 0.0)
                carry = carry + tot
            return carry

        lax.fori_loop(0, ns, sel_body, jnp.zeros((1, TILE), F32))

    q = qT_ref[0]
    qgs = [jnp.concatenate([q[(g * G + r) * HEAD_DIM:(g * G + r + 1) * HEAD_DIM, :] for r in range(G)], axis=1)
           for g in range(KV_HEADS_A)]
    m_ref[...] = jnp.full(m_ref.shape, NEG, F32)
    acc_ref[...] = jnp.zeros(acc_ref.shape, F32)

    AT = _ATT_TILES * TILE
    ones_rows = jnp.ones((_SUM_ROWS, AT), BF16)

    def att_body(u, carry):
        off = pl.multiple_of(u * AT, AT)
        tiles = [u * _ATT_TILES + k for k in range(_ATT_TILES)]
        roffs = [pl.multiple_of(jnp.where(j == n, 2 * TILE, jnp.where(j == n - 1, TILE, 0)), TILE) for j in tiles]
        selv = sel_ref[pl.ds(off, AT), :] > 0.5
        ss = [jnp.dot(kh_ref[0, g, pl.ds(off, AT), :], qgs[g], preferred_element_type=F32)
              for g in range(KV_HEADS_A)]
        for g in range(KV_HEADS_A):
            s = ss[g]
            s = jnp.concatenate(
                [jnp.where(selv,
                           s[:, r * TILE:(r + 1) * TILE]
                           + jnp.concatenate([toep_ref[g * G + r, pl.ds(ro, TILE), :] for ro in roffs], axis=0),
                           NEG)
                 for r in range(G)], axis=1)
            m = m_ref[g]
            m_new = jnp.maximum(m, jnp.max(s, axis=0, keepdims=True))
            alpha = jnp.exp2(m - m_new)
            p = jnp.exp2(s - m_new[0:1])
            m_ref[g] = m_new
            vt = jnp.concatenate([vT_ref[0, g * HEAD_DIM:(g + 1) * HEAD_DIM, pl.ds(off, AT)], ones_rows], axis=0)
            acc_ref[g] = acc_ref[g] * alpha[0:1] + jnp.dot(vt, p.astype(BF16), preferred_element_type=F32)
        return carry

    lax.fori_loop(0, n // _ATT_TILES + 1, att_body, 0)
    for g in range(KV_HEADS_A):
        out = acc_ref[g, 0:HEAD_DIM, :] / acc_ref[g, HEAD_DIM:HEAD_DIM + 1, :]
        for r in range(G):
            h = g * G + r
            o_ref[0, h * HEAD_DIM:(h + 1) * HEAD_DIM, :] = out[:, r * TILE:(r + 1) * TILE].astype(o_ref.dtype)


def _dsa_prompt(qiT, wiT, kib, qT, kh, vT, toep, topk):
    B, hq, S = qT.shape
    nq = S // TILE
    assert nq % _KEY_UNROLL == 0
    cw = (N_HEADS // KV_HEADS_A) * TILE
    return pl.pallas_call(
        functools.partial(_dsa_prompt_kernel, topk),
        grid=(B, nq),
        in_specs=[pl.BlockSpec((1, qiT.shape[1], TILE), lambda b, n: (b, 0, n)),
                  pl.BlockSpec((1, IDX_HEADS, TILE), lambda b, n: (b, 0, n)),
                  pl.BlockSpec((1, S, IDX_DIM), lambda b, n: (b, 0, 0)),
                  pl.BlockSpec((1, hq, TILE), lambda b, n: (b, 0, n)),
                  pl.BlockSpec((1, KV_HEADS_A, S, HEAD_DIM), lambda b, n: (b, 0, 0, 0)),
                  pl.BlockSpec((1, KV_HEADS_A * HEAD_DIM, S), lambda b, n: (b, 0, 0)),
                  pl.BlockSpec(toep.shape, lambda b, n: (0, 0, 0))],
        out_specs=pl.BlockSpec((1, hq, TILE), lambda b, n: (b, 0, n)),
        out_shape=jax.ShapeDtypeStruct((B, hq, S), BF16),
        scratch_shapes=[pltpu.VMEM((S, TILE), F32), pltpu.VMEM((S, TILE), F32),
                        pltpu.VMEM((KV_HEADS_A, 8, cw), F32),
                        pltpu.VMEM((KV_HEADS_A, HEAD_DIM + _SUM_ROWS, cw), F32)],
        compiler_params=_cparams(("arbitrary", "arbitrary")),
        name="dsa_prompt",
    )(qiT, wiT, kib, qT, kh, vT, toep)


def _swa_prompt_kernel(sink_ref, qT_ref, khp_ref, khc_ref, vTp_ref, vTc_ref, toep_ref, o_ref):
    n = pl.program_id(1)
    G = N_HEADS // KV_HEADS_B
    rows = lax.broadcasted_iota(jnp.int32, (2 * TILE, TILE), 0)
    lanes = lax.broadcasted_iota(jnp.int32, (2 * TILE, TILE), 1)
    dj = rows - lanes
    inwin = jnp.where(dj >= TILE - WINDOW + 1, jnp.where(dj <= TILE, 1.0, 0.0), 0.0)
    inwin = jnp.where(rows >= TILE, inwin, jnp.where(n > 0, inwin, 0.0))
    maskb = jnp.concatenate([inwin] * G, axis=1) > 0.5
    q = qT_ref[0]
    for g in range(KV_HEADS_B):
        heads = [g * G + r for r in range(G)]
        qg = jnp.concatenate([q[h * HEAD_DIM:(h + 1) * HEAD_DIM, :] for h in heads], axis=1)
        kk = jnp.concatenate([khp_ref[0, g], khc_ref[0, g]], axis=0)
        s = jnp.dot(kk, qg, preferred_element_type=F32)
        bias = jnp.concatenate([toep_ref[h, TILE:3 * TILE, :] for h in heads], axis=1)
        s = jnp.where(maskb, s + bias, NEG)
        sink = jnp.concatenate([jnp.full((1, TILE), sink_ref[h], F32) for h in heads], axis=1)
        m = jnp.maximum(jnp.max(s, axis=0, keepdims=True), sink)
        p = jnp.exp(s - m)
        l = jnp.sum(p, axis=0, keepdims=True) + jnp.exp(sink - m)
        vv = jnp.concatenate([vTp_ref[0, g * HEAD_DIM:(g + 1) * HEAD_DIM, :],
                              vTc_ref[0, g * HEAD_DIM:(g + 1) * HEAD_DIM, :]], axis=1)
        out = jnp.dot(vv, p.astype(BF16), preferred_element_type=F32) / l
        for r, h in enumerate(heads):
            o_ref[0, h * HEAD_DIM:(h + 1) * HEAD_DIM, :] = out[:, r * TILE:(r + 1) * TILE].astype(o_ref.dtype)


def _swa_prompt(qT, kh, vT, toep, sink):
    B, hq, S = qT.shape
    nb = S // TILE
    prev = lambda n: jnp.maximum(n - 1, 0)
    return pl.pallas_call(
        _swa_prompt_kernel,
        grid=(B, nb),
        in_specs=[pl.BlockSpec(memory_space=pltpu.SMEM),
                  pl.BlockSpec((1, hq, TILE), lambda b, n: (b, 0, n)),
                  pl.BlockSpec((1, KV_HEADS_B, TILE, HEAD_DIM), lambda b, n: (b, 0, prev(n), 0)),
                  pl.BlockSpec((1, KV_HEADS_B, TILE, HEAD_DIM), lambda b, n: (b, 0, n, 0)),
                  pl.BlockSpec((1, KV_HEADS_B * HEAD_DIM, TILE), lambda b, n: (b, 0, prev(n))),
                  pl.BlockSpec((1, KV_HEADS_B * HEAD_DIM, TILE), lambda b, n: (b, 0, n)),
                  pl.BlockSpec(toep.shape, lambda b, n: (0, 0, 0))],
        out_specs=pl.BlockSpec((1, hq, TILE), lambda b, n: (b, 0, n)),
        out_shape=jax.ShapeDtypeStruct((B, hq, S), BF16),
        compiler_params=_cparams(("arbitrary", "arbitrary")),
        name="swa_prompt",
    )(sink, qT, kh, kh, vT, vT, toep)


def _outproj_ln_kernel(alpha, x_ref, a_ref, w_ref, g_ref, b_ref, o_ref):
    y = alpha * x_ref[...] + jnp.dot(a_ref[...], w_ref[...], preferred_element_type=F32)
    o_ref[...] = _ln(y, g_ref[...], b_ref[...])


def _outproj_ln(x, a, w, g, b, alpha, tm=512):
    M, D = x.shape
    tm = min(tm, M)
    return pl.pallas_call(
        functools.partial(_outproj_ln_kernel, alpha),
        grid=(M // tm,),
        in_specs=[pl.BlockSpec((tm, D), lambda m: (m, 0)),
                  pl.BlockSpec((tm, a.shape[1]), lambda m: (m, 0)),
                  pl.BlockSpec(w.shape, lambda m: (0, 0)),
                  pl.BlockSpec((1, D), lambda m: (0, 0)),
                  pl.BlockSpec((1, D), lambda m: (0, 0))],
        out_specs=pl.BlockSpec((tm, D), lambda m: (m, 0)),
        out_shape=jax.ShapeDtypeStruct((M, D), F32),
        compiler_params=_cparams(("arbitrary",)),
        name="outproj_ln",
    )(x, a, w, g.reshape(1, D), b.reshape(1, D))


def _mlp_ln_kernel(alpha, x_ref, w1_ref, w2_ref, g_ref, b_ref, o_ref, acc_ref):
    f = pl.program_id(1)

    @pl.when(f == 0)
    def _():
        acc_ref[...] = jnp.zeros_like(acc_ref)

    h = jnp.maximum(jnp.dot(x_ref[...].astype(BF16), w1_ref[...], preferred_element_type=F32), 0.0)
    acc_ref[...] += jnp.dot((h * h).astype(BF16), w2_ref[...], preferred_element_type=F32)

    @pl.when(f == pl.num_programs(1) - 1)
    def _():
        o_ref[...] = _ln(alpha * x_ref[...] + acc_ref[...], g_ref[...], b_ref[...])


def _mlp_ln(x, w1, w2, g, b, alpha, tm=1024, tf=512):
    M, D = x.shape
    FF = w1.shape[1]
    tm = min(tm, M)
    return pl.pallas_call(
        functools.partial(_mlp_ln_kernel, alpha),
        grid=(M // tm, FF // tf),
        in_specs=[pl.BlockSpec((tm, D), lambda m, f: (m, 0)),
                  pl.BlockSpec((D, tf), lambda m, f: (0, f)),
                  pl.BlockSpec((tf, D), lambda m, f: (f, 0)),
                  pl.BlockSpec((1, D), lambda m, f: (0, 0)),
                  pl.BlockSpec((1, D), lambda m, f: (0, 0))],
        out_specs=pl.BlockSpec((tm, D), lambda m, f: (m, 0)),
        out_shape=jax.ShapeDtypeStruct((M, D), F32),
        scratch_shapes=[pltpu.VMEM((tm, D), F32)],
        compiler_params=_cparams(("arbitrary", "arbitrary")),
        name="mlp_ln",
    )(x, w1, w2, g.reshape(1, D), b.reshape(1, D))


def _matmul_kernel(x_ref, w_ref, o_ref):
    o_ref[...] = jnp.dot(x_ref[...].astype(BF16), w_ref[...], preferred_element_type=F32)


def _matmul(x, w):
    M, K = x.shape
    N = w.shape[1]
    return pl.pallas_call(
        _matmul_kernel,
        grid=(1,),
        in_specs=[pl.BlockSpec((M, K), lambda i: (0, 0)), pl.BlockSpec((K, N), lambda i: (0, 0))],
        out_specs=pl.BlockSpec((M, N), lambda i: (0, 0)),
        out_shape=jax.ShapeDtypeStruct((M, N), F32),
        compiler_params=_cparams(("arbitrary",)),
        name="sample_proj",
    )(x, w)


_SEQ_BLOCK = 8


def _dec_index_kernel(topk, n_pages, pt_ref, qi_ref, wi_ref, kiraw_ref, g_ref, b_ref, cidx_ref,
                      selb_ref, newsel_ref, kiln_ref, buf, sem, sc_ref, snew_ref):
    step = pl.program_id(0)
    nseq = pl.num_programs(0) * _SEQ_BLOCK
    L = n_pages * TILE

    def page_copy(seq, slot, p):
        return pltpu.make_async_copy(cidx_ref.at[pt_ref[seq, p]], buf.at[slot, p], sem.at[slot])

    def start(seq, slot):
        for p in range(n_pages):
            page_copy(seq, slot, p).start()

    def wait(seq, slot):
        for p in range(n_pages):
            page_copy(seq, slot, p).wait()

    @pl.when(step == 0)
    def _():
        start(0, 0)

    kiln_ref[...] = _ln(kiraw_ref[...], g_ref[...], b_ref[...])

    def seq_body(i, carry):
        seq = step * _SEQ_BLOCK + i
        slot = i % 2

        @pl.when(seq + 1 < nseq)
        def _():
            start(seq + 1, 1 - slot)

        wait(seq, slot)
        xk = jnp.concatenate([buf[slot, p] for p in range(n_pages)], axis=1).astype(BF16)
        qib = qi_ref[i].astype(BF16)
        s = jnp.dot(qib, xk, preferred_element_type=F32)
        w = wi_ref[i]
        row = jnp.sum(jnp.maximum(s, 0.0) * w, axis=0, keepdims=True) * IDX_SCALE
        sc_ref[pl.ds(i, 1), :] = jnp.where(row == 0.0, 0.0, row)
        kn = kiln_ref[pl.ds(i, 1), :].astype(BF16).astype(F32)
        sn = jnp.sum(qib.astype(F32) * kn, axis=1, keepdims=True)
        snew = jnp.sum(jnp.maximum(sn, 0.0) * w, axis=0, keepdims=True) * IDX_SCALE
        snew_ref[pl.ds(i, 1), :] = jnp.broadcast_to(jnp.where(snew == 0.0, 0.0, snew), (1, TILE))
        return carry

    lax.fori_loop(0, _SEQ_BLOCK, seq_body, 0)
    keys = sc_ref[...]
    knew = snew_ref[:, 0:1]

    def count(pred):
        return (jnp.sum(jnp.where(pred(keys), 1.0, 0.0), axis=1, keepdims=True)
                + jnp.where(pred(knew), 1.0, 0.0))

    t, _ = _kth_largest_search(lambda c: count(lambda k: k >= c), topk, (_SEQ_BLOCK, 1))
    room = topk - count(lambda k: k > t)
    ii = lax.broadcasted_iota(jnp.int32, (TILE, 2 * TILE), 0)
    jj = lax.broadcasted_iota(jnp.int32, (TILE, 2 * TILE), 1)
    pref = jnp.where(jj >= TILE, 1.0, jnp.where(ii < jj, 1.0, 0.0)).astype(BF16)
    carry = jnp.zeros((_SEQ_BLOCK, TILE), F32)
    for jt in range(n_pages):
        kt = keys[:, jt * TILE:(jt + 1) * TILE]
        eq = kt == t
        res = jnp.dot(jnp.where(eq, 1.0, 0.0).astype(BF16), pref, preferred_element_type=F32)
        take = jnp.where(res[:, :TILE] + carry < room, 1.0, 0.0)
        selb_ref[:, jt * TILE:(jt + 1) * TILE] = jnp.where(kt > t, 1.0, jnp.where(eq, take, 0.0))
        carry = carry + res[:, TILE:]
    newsel = jnp.where(knew > t, 1.0, jnp.where(knew == t, jnp.where(carry[:, :1] < room, 1.0, 0.0), 0.0))
    newsel_ref[...] = jnp.broadcast_to(newsel, (_SEQ_BLOCK, TILE))


def _dec_index(page_table, qi, wi, kiraw, g, b, cidx, topk):
    DB, n_pages = page_table.shape
    L = n_pages * TILE
    sb = _SEQ_BLOCK
    grid_spec = pltpu.PrefetchScalarGridSpec(
        num_scalar_prefetch=1,
        grid=(DB // sb,),
        in_specs=[pl.BlockSpec((sb, IDX_HEADS, IDX_DIM), lambda s, pt: (s, 0, 0)),
                  pl.BlockSpec((sb, IDX_HEADS, 1), lambda s, pt: (s, 0, 0)),
                  pl.BlockSpec((sb, IDX_DIM), lambda s, pt: (s, 0)),
                  pl.BlockSpec((1, IDX_DIM), lambda s, pt: (0, 0)),
                  pl.BlockSpec((1, IDX_DIM), lambda s, pt: (0, 0)),
                  pl.BlockSpec(memory_space=pl.ANY)],
        out_specs=[pl.BlockSpec((sb, L), lambda s, pt: (s, 0)),
                   pl.BlockSpec((sb, TILE), lambda s, pt: (s, 0)),
                   pl.BlockSpec((sb, IDX_DIM), lambda s, pt: (s, 0))],
        scratch_shapes=[pltpu.VMEM((2, n_pages, IDX_DIM, TILE), F32),
                        pltpu.SemaphoreType.DMA((2,)),
                        pltpu.VMEM((sb, L), F32),
                        pltpu.VMEM((sb, TILE), F32)])
    return pl.pallas_call(
        functools.partial(_dec_index_kernel, topk, n_pages),
        grid_spec=grid_spec,
        out_shape=[jax.ShapeDtypeStruct((DB, L), F32),
                   jax.ShapeDtypeStruct((DB, TILE), F32),
                   jax.ShapeDtypeStruct((DB, IDX_DIM), F32)],
        compiler_params=_cparams(("arbitrary",)),
        name="dec_index",
    )(page_table, qi, wi, kiraw, g.reshape(1, -1), b.reshape(1, -1), cidx)


_PAGE_CHUNK = 8
_DEC_SLOTS = 4


def _dec_attn_kernel(n_pages, pt_ref, q_ref, selb_ref, newsel_ref, kn_ref, vn_ref, dec_ref, ck_ref, cv_ref,
                     o_ref, kbuf, vbuf, sem):
    b = pl.program_id(0)
    nb = pl.num_programs(0)
    nch = n_pages // _PAGE_CHUNK
    CL = _PAGE_CHUNK * TILE
    kvd = KV_HEADS_A * HEAD_DIM

    def copies(seq, c, slot):
        out = []
        for p in range(_PAGE_CHUNK):
            phys = pt_ref[seq, c * _PAGE_CHUNK + p]
            out.append(pltpu.make_async_copy(ck_ref.at[phys], kbuf.at[slot, p], sem.at[0, slot]))
            out.append(pltpu.make_async_copy(cv_ref.at[phys], vbuf.at[slot, p], sem.at[1, slot]))
        return out

    def start(seq, c, slot):
        for cp in copies(seq, c, slot):
            cp.start()

    def wait(seq, c, slot):
        for cp in copies(seq, c, slot):
            cp.wait()

    ahead = _DEC_SLOTS - 1

    @pl.when(b == 0)
    def _():
        for c0 in range(ahead):
            start(0, c0, c0)

    qb = q_ref[0]
    far = dec_ref[1]
    near = dec_ref[0]

    def chunk_body(c, carry):
        m, l, acc = carry
        slot = c % _DEC_SLOTS
        last = c + 1 == nch
        wrap = c + ahead >= nch
        nxt_b = jnp.where(wrap, b + 1, b)
        nxt_c = jnp.where(wrap, c + ahead - nch, c + ahead)

        @pl.when(nxt_b < nb)
        def _():
            start(nxt_b, nxt_c, (c + ahead) % _DEC_SLOTS)

        wait(b, c, slot)
        kc = jnp.concatenate([kbuf[slot, p] for p in range(_PAGE_CHUNK)], axis=1).astype(BF16)
        vc = jnp.concatenate([vbuf[slot, p] for p in range(_PAGE_CHUNK)], axis=1).astype(BF16)
        s = jnp.dot(qb, kc, preferred_element_type=F32)
        bias = jnp.concatenate([far] * (_PAGE_CHUNK - 1) + [jnp.where(last, near, far)], axis=1)
        sel = selb_ref[0, :, pl.ds(pl.multiple_of(c * CL, CL), CL)] > 0.5
        s = jnp.where(sel, s + bias, NEG)
        m_new = jnp.maximum(m, jnp.max(s, axis=1, keepdims=True))
        alpha = jnp.exp(m - m_new)
        p = jnp.exp(s - m_new)
        l_new = alpha * l + jnp.sum(p, axis=1, keepdims=True)
        acc_new = alpha * acc + lax.dot_general(p.astype(BF16), vc, _NT, preferred_element_type=F32)
        return m_new, l_new, acc_new

    init = (jnp.full((N_HEADS, 1), NEG, F32), jnp.zeros((N_HEADS, 1), F32), jnp.zeros((N_HEADS, kvd), F32))
    m, l, acc = lax.fori_loop(0, nch, chunk_body, init)
    kn = kn_ref[0].astype(BF16).astype(F32)
    sn = jnp.sum(qb.astype(F32) * kn, axis=1, keepdims=True) + dec_ref[2][:, 0:1]
    sn = jnp.where(newsel_ref[0, :, 0:1] > 0.5, sn, NEG)
    m_new = jnp.maximum(m, sn)
    alpha = jnp.exp(m - m_new)
    pn = jnp.exp(sn - m_new)
    l = alpha * l + pn
    acc = alpha * acc + pn * vn_ref[0]
    o_ref[0] = acc / l


def _dec_attn(page_table, qbd, selb, newsel, kn, vn, dec, ck, cv):
    DB, n_pages = page_table.shape
    L = n_pages * TILE
    kvd = KV_HEADS_A * HEAD_DIM
    assert n_pages % (_DEC_SLOTS * _PAGE_CHUNK) == 0
    grid_spec = pltpu.PrefetchScalarGridSpec(
        num_scalar_prefetch=1,
        grid=(DB,),
        in_specs=[pl.BlockSpec((1, N_HEADS, kvd), lambda s, pt: (s, 0, 0)),
                  pl.BlockSpec((1, 1, L), lambda s, pt: (s, 0, 0)),
                  pl.BlockSpec((1, 1, TILE), lambda s, pt: (s, 0, 0)),
                  pl.BlockSpec((1, 1, kvd), lambda s, pt: (s, 0, 0)),
                  pl.BlockSpec((1, 1, kvd), lambda s, pt: (s, 0, 0)),
                  pl.BlockSpec(dec.shape, lambda s, pt: (0, 0, 0)),
                  pl.BlockSpec(memory_space=pl.ANY),
                  pl.BlockSpec(memory_space=pl.ANY)],
        out_specs=pl.BlockSpec((1, N_HEADS, kvd), lambda s, pt: (s, 0, 0)),
        scratch_shapes=[pltpu.VMEM((_DEC_SLOTS, _PAGE_CHUNK, kvd, TILE), F32),
                        pltpu.VMEM((_DEC_SLOTS, _PAGE_CHUNK, kvd, TILE), F32),
                        pltpu.SemaphoreType.DMA((2, _DEC_SLOTS))])
    return pl.pallas_call(
        functools.partial(_dec_attn_kernel, n_pages),
        grid_spec=grid_spec,
        out_shape=jax.ShapeDtypeStruct((DB, N_HEADS, kvd), F32),
        compiler_params=_cparams(("arbitrary",)),
        name="dec_attn",
    )(page_table, qbd, selb.reshape(DB, 1, L), newsel.reshape(DB, 1, TILE),
      kn.reshape(DB, 1, kvd), vn.reshape(DB, 1, kvd), dec, ck, cv)


def _swa_dec_kernel(q_ref, wk_ref, wv_ref, kn_ref, vn_ref, dec_ref, sink_ref, o_ref):
    lane = lax.broadcasted_iota(jnp.int32, (N_HEADS, TILE), 1)
    bias = dec_ref[0]
    b0 = dec_ref[2][:, 0:1]
    sk = sink_ref[...]
    for i in range(_SEQ_BLOCK):
        qb = q_ref[i]
        s = jnp.dot(qb, wk_ref[i].astype(BF16), preferred_element_type=F32)
        s = jnp.where(lane >= 1, s + bias, NEG)
        kn = kn_ref[i].astype(BF16).astype(F32)
        sn = jnp.sum(qb.astype(F32) * kn, axis=1, keepdims=True) + b0
        m = jnp.maximum(jnp.maximum(jnp.max(s, axis=1, keepdims=True), sn), sk)
        p = jnp.exp(s - m)
        pn = jnp.exp(sn - m)
        l = jnp.sum(p, axis=1, keepdims=True) + pn + jnp.exp(sk - m)
        out = lax.dot_general(p.astype(BF16), wv_ref[i].astype(BF16), _NT, preferred_element_type=F32) + pn * vn_ref[i]
        o_ref[i] = out / l


def _swa_dec(qbd, wk, wv, kn, vn, dec, sink):
    DB = qbd.shape[0]
    kvd = KV_HEADS_B * HEAD_DIM
    sb = _SEQ_BLOCK
    return pl.pallas_call(
        _swa_dec_kernel,
        grid=(DB // sb,),
        in_specs=[pl.BlockSpec((sb, N_HEADS, kvd), lambda s: (s, 0, 0)),
                  pl.BlockSpec((sb, kvd, WINDOW), lambda s: (s, 0, 0)),
                  pl.BlockSpec((sb, kvd, WINDOW), lambda s: (s, 0, 0)),
                  pl.BlockSpec((sb, 1, kvd), lambda s: (s, 0, 0)),
                  pl.BlockSpec((sb, 1, kvd), lambda s: (s, 0, 0)),
                  pl.BlockSpec(dec.shape, lambda s: (0, 0, 0)),
                  pl.BlockSpec((N_HEADS, 1), lambda s: (0, 0))],
        out_specs=pl.BlockSpec((sb, N_HEADS, kvd), lambda s: (s, 0, 0)),
        out_shape=jax.ShapeDtypeStruct((DB, N_HEADS, kvd), F32),
        compiler_params=_cparams(("arbitrary",)),
        name="swa_dec",
    )(qbd, wk, wv, kn.reshape(DB, 1, kvd), vn.reshape(DB, 1, kvd), dec, sink.reshape(N_HEADS, 1))


def _block_diag_q(q, nkv):
    DB = q.shape[0]
    G = N_HEADS // nkv
    qh = (q * ATTN_SCALE).reshape(DB, N_HEADS, 1, HEAD_DIM)
    own = (np.arange(N_HEADS)[:, None] // G == np.arange(nkv)[None, :]).astype(np.float32)
    return (qh * own[None, :, :, None]).reshape(DB, N_HEADS, nkv * HEAD_DIM).astype(BF16)


def _own_group(o, nkv):
    DB = o.shape[0]
    G = N_HEADS // nkv
    o5 = o.reshape(DB, nkv, G, nkv, HEAD_DIM)
    return jnp.stack([o5[:, g, :, g, :] for g in range(nkv)], axis=1).reshape(DB, N_HEADS * HEAD_DIM)


def _pad_cols(w, mult=TILE):
    pad = (-w.shape[1]) % mult
    return jnp.pad(w, ((0, 0), (0, pad))) if pad else w


def _slots_minor(a):
    n, t, h, d = a.shape
    return a.transpose(0, 2, 3, 1).reshape(n, h * d, t)


def _slots_major(a, heads):
    n, hd, t = a.shape
    return a.reshape(n, heads, hd // heads, t).transpose(0, 3, 1, 2)


def kernel(x_prompt, x_sample, cache_k, cache_v, cache_idx_k, state_win_k, state_win_v, page_table, rel_bias_table, a_w_in, a_w_out, a_idx_ln_g, a_idx_ln_b, b_w_in, b_w_out, b_sink, ln1_g, ln1_b, ln2_g, ln2_b, mlp_w1, mlp_w2):
    B, S, D = x_prompt.shape
    DB, DS, _ = x_sample.shape
    depth = ln1_g.shape[0]
    assert depth == 2 and DS == 1 and S % TILE == 0 and DB % _SEQ_BLOCK == 0
    assert state_win_k.shape[2] == WINDOW and cache_k.shape[2] == TILE
    alpha = (2 * depth) ** 0.25
    n_pages = page_table.shape[1]
    past = n_pages * TILE
    hq = N_HEADS * HEAD_DIM
    kva = KV_HEADS_A * HEAD_DIM
    kvb = KV_HEADS_B * HEAD_DIM
    hi = IDX_HEADS * IDX_DIM

    toep, toep2, dec = _bias_tables(rel_bias_table)

    wa = a_w_in[0]
    c0, c1, c2, c3, c4 = hq, hq + kva, hq + 2 * kva, hq + 2 * kva + hi, hq + 2 * kva + hi + IDX_DIM
    wq, wk, wv, wqi, wki, wwi = wa[:, :c0], wa[:, c0:c1], wa[:, c1:c2], wa[:, c2:c3], wa[:, c3:c4], wa[:, c4:]
    a_wo = a_w_out[0].astype(BF16)
    w1 = [mlp_w1[i].astype(BF16) for i in range(depth)]
    w2 = [mlp_w2[i].astype(BF16) for i in range(depth)]

    qT, kTp, vTp, kh, vT, qiT, kiTp, kib, wiT = _prompt_proj(
        x_prompt, wq, wk, wv, KV_HEADS_A, idx=(wqi, wki, wwi, a_idx_ln_g[0], a_idx_ln_b[0]),
        q_scale=ATTN_SCALE * LOG2E)
    oT = _dsa_prompt(qiT, wiT, kib, qT, kh, vT, toep2, min(TOPK_MAX, S // 4))
    attn = oT.transpose(0, 2, 1).reshape(B * S, hq)
    y_p = _outproj_ln(x_prompt.reshape(B * S, D), attn, a_wo, ln1_g[0], ln1_b[0], alpha)
    y_p = _mlp_ln(y_p, w1[0], w2[0], ln2_g[0], ln2_b[0], alpha)

    xs = x_sample.reshape(DB, D)
    proj = _matmul(xs, _pad_cols(wa).astype(BF16))
    q_s, k_s, v_s = proj[:, :c0], proj[:, c0:c1], proj[:, c1:c2]
    qi_s, kiraw_s, wi_s = proj[:, c2:c3], proj[:, c3:c4], proj[:, c4:c4 + IDX_HEADS]
    selb, newsel, ki_s = _dec_index(
        page_table, qi_s.reshape(DB, IDX_HEADS, IDX_DIM), wi_s.reshape(DB, IDX_HEADS, 1), kiraw_s,
        a_idx_ln_g[0], a_idx_ln_b[0], cache_idx_k[0].transpose(0, 2, 1), min(TOPK_MAX, (past + DS) // 4))
    o_s = _dec_attn(page_table, _block_diag_q(q_s, KV_HEADS_A), selb, newsel, k_s, v_s, dec,
                    _slots_minor(cache_k[0]), _slots_minor(cache_v[0]))
    y_s = _outproj_ln(xs, _own_group(o_s, KV_HEADS_A).astype(BF16), a_wo, ln1_g[0], ln1_b[0], alpha)
    y_s = _mlp_ln(y_s, w1[0], w2[0], ln2_g[0], ln2_b[0], alpha)

    wb = b_w_in[0]
    bq, bk, bv = wb[:, :hq], wb[:, hq:hq + kvb], wb[:, hq + kvb:]
    b_wo = b_w_out[0].astype(BF16)

    qT1, k1Tp, v1Tp, kh1, vT1 = _prompt_proj(y_p.reshape(B, S, D), bq, bk, bv, KV_HEADS_B)
    oT1 = _swa_prompt(qT1, kh1, vT1, toep, b_sink[0])
    attn1 = oT1.transpose(0, 2, 1).reshape(B * S, hq)
    y_p = _outproj_ln(y_p, attn1, b_wo, ln1_g[1], ln1_b[1], alpha)
    y_p = _mlp_ln(y_p, w1[1], w2[1], ln2_g[1], ln2_b[1], alpha)

    proj1 = _matmul(y_s, wb.astype(BF16))
    q1_s, k1_s, v1_s = proj1[:, :hq], proj1[:, hq:hq + kvb], proj1[:, hq + kvb:]
    win_k = _slots_minor(state_win_k[0])
    win_v = _slots_minor(state_win_v[0])
    o1_s = _swa_dec(_block_diag_q(q1_s, KV_HEADS_B), win_k, win_v, k1_s, v1_s, dec, b_sink[0])
    y_s = _outproj_ln(y_s, _own_group(o1_s, KV_HEADS_B).astype(BF16), b_wo, ln1_g[1], ln1_b[1], alpha)
    y_s = _mlp_ln(y_s, w1[1], w2[1], ln2_g[1], ln2_b[1], alpha)

    npg = S // TILE
    assert min(WINDOW, S) == TILE
    pages = lambda a, heads: _slots_major(a.reshape(B * npg, -1, TILE), heads).reshape(1, B, npg, TILE, heads, HEAD_DIM)
    new_wk_s = jnp.concatenate([win_k[:, :, 1:], k1_s[:, :, None]], axis=2)
    new_wv_s = jnp.concatenate([win_v[:, :, 1:], v1_s[:, :, None]], axis=2)
    return (y_p.reshape(B, S, D),
            y_s.reshape(DB, DS, D),
            pages(kTp, KV_HEADS_A),
            pages(vTp, KV_HEADS_A),
            kiTp.transpose(0, 1, 3, 2).reshape(1, B, npg, TILE, IDX_DIM),
            k_s.reshape(1, DB, DS, KV_HEADS_A, HEAD_DIM),
            v_s.reshape(1, DB, DS, KV_HEADS_A, HEAD_DIM),
            ki_s.reshape(1, DB, DS, IDX_DIM),
            _slots_major(k1Tp[:, npg - 1], KV_HEADS_B)[None],
            _slots_major(v1Tp[:, npg - 1], KV_HEADS_B)[None],
            _slots_major(new_wk_s, KV_HEADS_B)[None],
            _slots_major(new_wv_s, KV_HEADS_B)[None])
```

```python
import functools
import math

import numpy as np
import jax
import jax.numpy as jnp
from jax import lax
from jax.experimental import pallas as pl
from jax.experimental.pallas import tpu as pltpu

F32 = jnp.float32
BF16 = jnp.bfloat16

N_HEADS = 16
HEAD_DIM = 64
KV_HEADS_A = 4
KV_HEADS_B = 2
IDX_HEADS = 8
IDX_DIM = 64
TOPK_MAX = 256
WINDOW = 128
TILE = 128
N_BUCKETS = 32
MAX_DISTANCE = 128
LN_EPS = 1e-5
NEG = -1e30
INT_MIN = -(2 ** 31)
ATTN_SCALE = HEAD_DIM ** -0.5
LOG2E = math.log2(math.e)
IDX_SCALE = IDX_HEADS ** -0.5 * IDX_DIM ** -0.5
FAR_BUCKET = N_BUCKETS - 1
FAR_DISTANCE = 113

_NT = (((1,), (1,)), ((), ()))
_VMEM_LIMIT = 48 * 1024 * 1024
_VMEM_LIMIT_MIXER = 58 * 1024 * 1024


def _cparams(sem, vmem_limit=_VMEM_LIMIT):
    return pltpu.CompilerParams(dimension_semantics=sem, vmem_limit_bytes=vmem_limit)


def _bucket_np(dist):
    n = np.maximum(dist, 0)
    max_exact = N_BUCKETS // 2
    nf = np.maximum(n, max_exact).astype(np.float64)
    val = np.log(nf / max_exact) / math.log(MAX_DISTANCE / max_exact) * (N_BUCKETS - max_exact)
    frac = val - np.floor(val)
    interior = (n > max_exact) & (n < MAX_DISTANCE)
    assert not np.any(interior & ((frac < 1e-6) | (frac > 1 - 1e-6))), "bucket boundary too close to an integer"
    large = np.minimum(max_exact + val.astype(np.int32), N_BUCKETS - 1)
    out = np.where(n < max_exact, n, large).astype(np.int32)
    assert np.all(out[n >= FAR_DISTANCE] == FAR_BUCKET)
    return out


def _key_to_float(k):
    return lax.bitcast_convert_type(k ^ (lax.shift_right_arithmetic(k, 31) & 0x7FFFFFFF), F32)


def _kth_largest_search(count_ge, topk, shape):
    def bit_body(b, carry):
        t, cnt_t = carry
        cand = t ^ lax.shift_left(jnp.int32(1), 31 - b)
        cnt = count_ge(_key_to_float(cand))
        ok = cnt >= topk
        return jnp.where(ok, cand, t), jnp.where(ok, cnt, cnt_t)

    init = (jnp.full(shape, INT_MIN, jnp.int32), jnp.full(shape, 3e38, F32))
    t, cnt_t = lax.fori_loop(0, 32, bit_body, init)
    return _key_to_float(t), cnt_t


def _ln(y, g, b):
    mu = jnp.mean(y, axis=-1, keepdims=True)
    yc = y - mu
    var = jnp.mean(yc * yc, axis=-1, keepdims=True)
    return yc * lax.rsqrt(var + LN_EPS) * g + b


def _bias_kernel(table_ref, bk_ref, bkd_ref, toep_ref, toep2_ref, dec_ref):
    bk = bk_ref[...]
    bkd = bkd_ref[...]
    dec_rows = []
    for h in range(N_HEADS):
        acc = jnp.zeros(bk.shape, F32)
        accd = jnp.zeros(bkd.shape, F32)
        for b in range(N_BUCKETS):
            val = table_ref[b, h]
            acc = jnp.where(bk == b, val, acc)
            accd = jnp.where(bkd == b, val, accd)
        toep_ref[h] = acc
        toep2_ref[h] = acc * LOG2E
        dec_rows.append(accd)
    for kind in range(3):
        dec_ref[kind] = jnp.concatenate([d[kind:kind + 1, :] for d in dec_rows], axis=0)


def _bias_tables(table):
    i = np.arange(TILE)[None, :]
    j = np.arange(2 * TILE)[:, None]
    bk = np.concatenate([np.full((TILE, TILE), FAR_BUCKET, np.int32), _bucket_np(i + TILE - j)], axis=0)
    bkd = np.zeros((8, TILE), np.int32)
    bkd[0] = _bucket_np(TILE - np.arange(TILE))
    bkd[1] = FAR_BUCKET
    bkd[2] = 0
    return pl.pallas_call(
        _bias_kernel,
        grid=(1,),
        in_specs=[pl.BlockSpec(memory_space=pltpu.SMEM),
                  pl.BlockSpec((3 * TILE, TILE), lambda h: (0, 0)),
                  pl.BlockSpec((8, TILE), lambda h: (0, 0))],
        out_specs=[pl.BlockSpec((N_HEADS, 3 * TILE, TILE), lambda h: (0, 0, 0)),
                   pl.BlockSpec((N_HEADS, 3 * TILE, TILE), lambda h: (0, 0, 0)),
                   pl.BlockSpec((3, N_HEADS, TILE), lambda h: (0, 0, 0))],
        out_shape=[jax.ShapeDtypeStruct((N_HEADS, 3 * TILE, TILE), F32),
                   jax.ShapeDtypeStruct((N_HEADS, 3 * TILE, TILE), F32),
                   jax.ShapeDtypeStruct((3, N_HEADS, TILE), F32)],
        compiler_params=_cparams(("arbitrary",)),
        name="bias_tables",
    )(table, jnp.asarray(bk), jnp.asarray(bkd))


def _normalize_cols(y):
    mu = jnp.mean(y, axis=0, keepdims=True)
    yc = y - mu
    var = jnp.mean(yc * yc, axis=0, keepdims=True)
    return yc * lax.rsqrt(var + LN_EPS)


def _proj_kernel(nkv, has_idx, *refs):
    if has_idx:
        (x_ref, wqT_ref, wkT_ref, wvT_ref, wkh_ref, wqiT_ref, wkiT_ref, wki_ref, wwiT_ref, g_ref, b_ref,
         gc_ref, bc_ref, qT_ref, kTp_ref, vTp_ref, kh_ref, vT_ref, qiT_ref, kiTp_ref, kib_ref, wiT_ref) = refs
    else:
        (x_ref, wqT_ref, wkT_ref, wvT_ref, wkh_ref, qT_ref, kTp_ref, vTp_ref, kh_ref, vT_ref) = refs
    xb = x_ref[0].astype(BF16)
    npg = xb.shape[0] // TILE
    qT_ref[0] = lax.dot_general(wqT_ref[...], xb, _NT, preferred_element_type=F32).astype(BF16)
    kT = lax.dot_general(wkT_ref[...], xb, _NT, preferred_element_type=F32)
    vT = lax.dot_general(wvT_ref[...], xb, _NT, preferred_element_type=F32)
    vT_ref[0] = vT.astype(BF16)
    for j in range(npg):
        kTp_ref[0, j] = kT[:, j * TILE:(j + 1) * TILE]
        vTp_ref[0, j] = vT[:, j * TILE:(j + 1) * TILE]
    for g in range(nkv):
        kh_ref[0, g] = jnp.dot(xb, wkh_ref[g], preferred_element_type=F32).astype(BF16)
    if has_idx:
        qiT_ref[0] = lax.dot_general(wqiT_ref[...], xb, _NT, preferred_element_type=F32).astype(BF16)
        kiT = _normalize_cols(lax.dot_general(wkiT_ref[...], xb, _NT, preferred_element_type=F32))
        for j in range(npg):
            kiTp_ref[0, j] = kiT[:, j * TILE:(j + 1) * TILE] * gc_ref[...] + bc_ref[...]
        ki = _ln(jnp.dot(xb, wki_ref[...], preferred_element_type=F32), g_ref[...], b_ref[...])
        kib_ref[0] = ki.astype(BF16)
        wiT_ref[0] = lax.dot_general(wwiT_ref[...], xb, _NT, preferred_element_type=F32)


def _prompt_proj(x, wq, wk, wv, nkv, idx=None, q_scale=ATTN_SCALE, tm=512):
    B, S, D = x.shape
    tm = min(tm, S)
    kvd = nkv * HEAD_DIM
    hq = N_HEADS * HEAD_DIM
    wqT = (wq * q_scale).T.astype(BF16)
    wkh = wk.reshape(D, nkv, HEAD_DIM).transpose(1, 0, 2).astype(BF16)
    full2 = lambda a: pl.BlockSpec(a.shape, lambda b, m: (0,) * a.ndim)
    ins = [x, wqT, wk.T.astype(BF16), wv.T.astype(BF16), wkh]
    in_specs = [pl.BlockSpec((1, tm, D), lambda b, m: (b, m, 0))] + [full2(a) for a in ins[1:]]
    out_shape = [jax.ShapeDtypeStruct((B, hq, S), BF16),
                 jax.ShapeDtypeStruct((B, S // TILE, kvd, TILE), F32),
                 jax.ShapeDtypeStruct((B, S // TILE, kvd, TILE), F32),
                 jax.ShapeDtypeStruct((B, nkv, S, HEAD_DIM), BF16),
                 jax.ShapeDtypeStruct((B, kvd, S), BF16)]
    out_specs = [pl.BlockSpec((1, hq, tm), lambda b, m: (b, 0, m)),
                 pl.BlockSpec((1, tm // TILE, kvd, TILE), lambda b, m: (b, m, 0, 0)),
                 pl.BlockSpec((1, tm // TILE, kvd, TILE), lambda b, m: (b, m, 0, 0)),
                 pl.BlockSpec((1, nkv, tm, HEAD_DIM), lambda b, m: (b, 0, m, 0)),
                 pl.BlockSpec((1, kvd, tm), lambda b, m: (b, 0, m))]
    if idx is not None:
        wqi, wki, wwi, g, bb = idx
        extra = [wqi.T.astype(BF16), wki.T.astype(BF16), wki.astype(BF16), wwi.T.astype(BF16),
                 g.reshape(1, -1), bb.reshape(1, -1),
                 jnp.broadcast_to(g.reshape(-1, 1), (IDX_DIM, TILE)), jnp.broadcast_to(bb.reshape(-1, 1), (IDX_DIM, TILE))]
        ins += extra
        in_specs += [full2(a) for a in extra]
        hi = IDX_HEADS * IDX_DIM
        out_shape += [jax.ShapeDtypeStruct((B, hi, S), BF16),
                      jax.ShapeDtypeStruct((B, S // TILE, IDX_DIM, TILE), F32),
                      jax.ShapeDtypeStruct((B, S, IDX_DIM), BF16),
                      jax.ShapeDtypeStruct((B, IDX_HEADS, S), F32)]
        out_specs += [pl.BlockSpec((1, hi, tm), lambda b, m: (b, 0, m)),
                      pl.BlockSpec((1, tm // TILE, IDX_DIM, TILE), lambda b, m: (b, m, 0, 0)),
                      pl.BlockSpec((1, tm, IDX_DIM), lambda b, m: (b, m, 0)),
                      pl.BlockSpec((1, IDX_HEADS, tm), lambda b, m: (b, 0, m))]
    return pl.pallas_call(
        functools.partial(_proj_kernel, nkv, idx is not None),
        grid=(B, S // tm),
        in_specs=in_specs,
        out_specs=out_specs,
        out_shape=out_shape,
        compiler_params=_cparams(("arbitrary", "arbitrary")),
        name="prompt_proj_idx" if idx is not None else "prompt_proj",
    )(*ins)


_KEY_UNROLL = 4
_SUM_ROWS = 16
_ATT_TILES = 2


def _dsa_prompt_block(topk, n, qiT_ref, wiT_ref, ki_ref, qT_ref, kh_ref, vT_ref, toep_ref, o_ref,
                      skey_ref, sel_ref, m_ref, acc_ref):
    ST = _KEY_UNROLL * TILE
    ns = n // _KEY_UNROLL + 1
    G = N_HEADS // KV_HEADS_A
    rows = lax.broadcasted_iota(jnp.int32, (ST, TILE), 0)
    lanes = lax.broadcasted_iota(jnp.int32, (ST, TILE), 1)
    qpos = n * TILE + lanes

    qi = qiT_ref[0]
    wi = wiT_ref[0]

    def score_body(u, carry):
        off = pl.multiple_of(u * ST, ST)
        kij = ki_ref[0, pl.ds(off, ST), :]
        acc = jnp.zeros((ST, TILE), F32)
        for h in range(0, IDX_HEADS, 2):
            qpair = jnp.concatenate([qi[IDX_DIM * h:IDX_DIM * (h + 1), :],
                                     qi[IDX_DIM * (h + 1):IDX_DIM * (h + 2), :]], axis=1)
            s = jnp.dot(kij, qpair, preferred_element_type=F32)
            acc = acc + jnp.maximum(s[:, :TILE], 0.0) * wi[h:h + 1, :]
            acc = acc + jnp.maximum(s[:, TILE:], 0.0) * wi[h + 1:h + 2, :]
        sc = acc * IDX_SCALE
        sc = jnp.where(off + rows <= qpos, sc, NEG)
        sc = jnp.where(sc == 0.0, 0.0, sc)
        skey_ref[pl.ds(off, ST), :] = sc
        return carry

    lax.fori_loop(0, ns, score_body, 0)

    def count(pred):
        def body(u, c):
            off = pl.multiple_of(u * ST, ST)
            x = jnp.where(pred(skey_ref[pl.ds(off, ST), :]), 1.0, 0.0).reshape(ST // 8, 8, TILE)
            while x.shape[0] > 1:
                half = x.shape[0] // 2
                x = x[:half] + x[half:]
            return c + x[0]
        c8 = lax.fori_loop(0, ns, body, jnp.zeros((8, TILE), F32))
        return jnp.sum(c8, axis=0, keepdims=True)

    t, cnt_t = _kth_largest_search(lambda c: count(lambda kt: kt >= c), topk, (1, TILE))
    exact_fit = jnp.max(jnp.abs(cnt_t - topk)) == 0.0

    @pl.when(exact_fit)
    def _():
        def sel_body(u, carry):
            off = pl.multiple_of(u * ST, ST)
            sel = jnp.where(skey_ref[pl.ds(off, ST), :] >= t, 1.0, 0.0)
            sel_ref[pl.ds(off, ST), :] = jnp.where(off + rows <= qpos, sel, 0.0)
            return carry

        lax.fori_loop(0, ns, sel_body, 0)

    @pl.when(jnp.logical_not(exact_fit))
    def _():
        room = topk - count(lambda kt: kt > t)
        ii = lax.broadcasted_iota(jnp.int32, (TILE, TILE), 0)
        jj = lax.broadcasted_iota(jnp.int32, (TILE, TILE), 1)
        lower = jnp.where(jj < ii, 1.0, 0.0).astype(BF16)
        ones8 = jnp.ones((8, TILE), BF16)

        def sel_body(u, carry):
            for k in range(_KEY_UNROLL):
                off = pl.multiple_of(u * ST + k * TILE, TILE)
                kt = skey_ref[pl.ds(off, TILE), :]
                eq = kt == t
                eqb = jnp.where(eq, 1.0, 0.0).astype(BF16)
                before = jnp.dot(lower, eqb, preferred_element_type=F32) + carry
                tot = jnp.dot(ones8, eqb, preferred_element_type=F32)[0:1]
                sel = jnp.where(kt > t, 1.0, jnp.where(eq, jnp.where(before < room, 1.0, 0.0), 0.0))
                sel_ref[pl.ds(off, TILE), :] = jnp.where(off + ii <= n * TILE + jj, sel, 0.0)
                carry = carry + tot
            return carry

        lax.fori_loop(0, ns, sel_body, jnp.zeros((1, TILE), F32))

    q = qT_ref[0]
    qgs = [jnp.concatenate([q[(g * G + r) * HEAD_DIM:(g * G + r + 1) * HEAD_DIM, :] for r in range(G)], axis=1)
           for g in range(KV_HEADS_A)]
    m_ref[...] = jnp.full(m_ref.shape, NEG, F32)
    acc_ref[...] = jnp.zeros(acc_ref.shape, F32)

    AT = _ATT_TILES * TILE
    ones_rows = jnp.ones((_SUM_ROWS, AT), BF16)

    def att_body(u, carry):
        off = pl.multiple_of(u * AT, AT)
        tiles = [u * _ATT_TILES + k for k in range(_ATT_TILES)]
        roffs = [pl.multiple_of(jnp.where(j == n, 2 * TILE, jnp.where(j == n - 1, TILE, 0)), TILE) for j in tiles]
        selv = sel_ref[pl.ds(off, AT), :] > 0.5
        ss = [jnp.dot(kh_ref[0, g, pl.ds(off, AT), :], qgs[g], preferred_element_type=F32)
              for g in range(KV_HEADS_A)]
        for g in range(KV_HEADS_A):
            s = ss[g]
            s = jnp.concatenate(
                [jnp.where(selv,
                           s[:, r * TILE:(r + 1) * TILE]
                           + jnp.concatenate([toep_ref[g * G + r, pl.ds(ro, TILE), :] for ro in roffs], axis=0),
                           NEG)
                 for r in range(G)], axis=1)
            m = m_ref[g]
            m_new = jnp.maximum(m, jnp.max(s, axis=0, keepdims=True))
            alpha = jnp.exp2(m - m_new)
            p = jnp.exp2(s - m_new[0:1])
            m_ref[g] = m_new
            vt = jnp.concatenate([vT_ref[0, g * HEAD_DIM:(g + 1) * HEAD_DIM, pl.ds(off, AT)], ones_rows], axis=0)
            acc_ref[g] = acc_ref[g] * alpha[0:1] + jnp.dot(vt, p.astype(BF16), preferred_element_type=F32)
        return carry

    lax.fori_loop(0, n // _ATT_TILES + 1, att_body, 0)
    for g in range(KV_HEADS_A):
        out = acc_ref[g, 0:HEAD_DIM, :] / acc_ref[g, HEAD_DIM:HEAD_DIM + 1, :]
        for r in range(G):
            h = g * G + r
            o_ref[0, h * HEAD_DIM:(h + 1) * HEAD_DIM, :] = out[:, r * TILE:(r + 1) * TILE].astype(o_ref.dtype)


def _swa_prompt_kernel(sink_ref, qT_ref, khp_ref, khc_ref, vTp_ref, vTc_ref, toep_ref, o_ref):
    n = pl.program_id(1)
    G = N_HEADS // KV_HEADS_B
    rows = lax.broadcasted_iota(jnp.int32, (2 * TILE, TILE), 0)
    lanes = lax.broadcasted_iota(jnp.int32, (2 * TILE, TILE), 1)
    dj = rows - lanes
    inwin = jnp.where(dj >= TILE - WINDOW + 1, jnp.where(dj <= TILE, 1.0, 0.0), 0.0)
    inwin = jnp.where(rows >= TILE, inwin, jnp.where(n > 0, inwin, 0.0))
    maskb = jnp.concatenate([inwin] * G, axis=1) > 0.5
    q = qT_ref[0]
    for g in range(KV_HEADS_B):
        heads = [g * G + r for r in range(G)]
        qg = jnp.concatenate([q[h * HEAD_DIM:(h + 1) * HEAD_DIM, :] for h in heads], axis=1)
        kk = jnp.concatenate([khp_ref[0, g], khc_ref[0, g]], axis=0)
        s = jnp.dot(kk, qg, preferred_element_type=F32)
        bias = jnp.concatenate([toep_ref[h, TILE:3 * TILE, :] for h in heads], axis=1)
        s = jnp.where(maskb, s + bias, NEG)
        sink = jnp.concatenate([jnp.full((1, TILE), sink_ref[h], F32) for h in heads], axis=1)
        m = jnp.maximum(jnp.max(s, axis=0, keepdims=True), sink)
        p = jnp.exp(s - m)
        l = jnp.sum(p, axis=0, keepdims=True) + jnp.exp(sink - m)
        vv = jnp.concatenate([vTp_ref[0, g * HEAD_DIM:(g + 1) * HEAD_DIM, :],
                              vTc_ref[0, g * HEAD_DIM:(g + 1) * HEAD_DIM, :]], axis=1)
        out = jnp.dot(vv, p.astype(BF16), preferred_element_type=F32) / l
        for r, h in enumerate(heads):
            o_ref[0, h * HEAD_DIM:(h + 1) * HEAD_DIM, :] = out[:, r * TILE:(r + 1) * TILE].astype(o_ref.dtype)


def _swa_prompt(qT, kh, vT, toep, sink):
    B, hq, S = qT.shape
    nb = S // TILE
    prev = lambda n: jnp.maximum(n - 1, 0)
    return pl.pallas_call(
        _swa_prompt_kernel,
        grid=(B, nb),
        in_specs=[pl.BlockSpec(memory_space=pltpu.SMEM),
                  pl.BlockSpec((1, hq, TILE), lambda b, n: (b, 0, n)),
                  pl.BlockSpec((1, KV_HEADS_B, TILE, HEAD_DIM), lambda b, n: (b, 0, prev(n), 0)),
                  pl.BlockSpec((1, KV_HEADS_B, TILE, HEAD_DIM), lambda b, n: (b, 0, n, 0)),
                  pl.BlockSpec((1, KV_HEADS_B * HEAD_DIM, TILE), lambda b, n: (b, 0, prev(n))),
                  pl.BlockSpec((1, KV_HEADS_B * HEAD_DIM, TILE), lambda b, n: (b, 0, n)),
                  pl.BlockSpec(toep.shape, lambda b, n: (0, 0, 0))],
        out_specs=pl.BlockSpec((1, hq, TILE), lambda b, n: (b, 0, n)),
        out_shape=jax.ShapeDtypeStruct((B, hq, S), BF16),
        compiler_params=_cparams(("arbitrary", "arbitrary")),
        name="swa_prompt",
    )(sink, qT, kh, kh, vT, vT, toep)


def _outproj_ln_kernel(alpha, x_ref, a_ref, w_ref, g_ref, b_ref, o_ref):
    y = alpha * x_ref[...] + jnp.dot(a_ref[...], w_ref[...], preferred_element_type=F32)
    o_ref[...] = _ln(y, g_ref[...], b_ref[...])


def _outproj_ln(x, a, w, g, b, alpha, tm=512):
    M, D = x.shape
    tm = min(tm, M)
    return pl.pallas_call(
        functools.partial(_outproj_ln_kernel, alpha),
        grid=(M // tm,),
        in_specs=[pl.BlockSpec((tm, D), lambda m: (m, 0)),
                  pl.BlockSpec((tm, a.shape[1]), lambda m: (m, 0)),
                  pl.BlockSpec(w.shape, lambda m: (0, 0)),
                  pl.BlockSpec((1, D), lambda m: (0, 0)),
                  pl.BlockSpec((1, D), lambda m: (0, 0))],
        out_specs=pl.BlockSpec((tm, D), lambda m: (m, 0)),
        out_shape=jax.ShapeDtypeStruct((M, D), F32),
        compiler_params=_cparams(("arbitrary",)),
        name="outproj_ln",
    )(x, a, w, g.reshape(1, D), b.reshape(1, D))


def _mlp_ln_kernel(alpha, x_ref, w1_ref, w2_ref, g_ref, b_ref, o_ref, acc_ref):
    f = pl.program_id(1)

    @pl.when(f == 0)
    def _():
        acc_ref[...] = jnp.zeros_like(acc_ref)

    h = jnp.maximum(jnp.dot(x_ref[...].astype(BF16), w1_ref[...], preferred_element_type=F32), 0.0)
    acc_ref[...] += jnp.dot((h * h).astype(BF16), w2_ref[...], preferred_element_type=F32)

    @pl.when(f == pl.num_programs(1) - 1)
    def _():
        o_ref[...] = _ln(alpha * x_ref[...] + acc_ref[...], g_ref[...], b_ref[...])


def _mlp_ln(x, w1, w2, g, b, alpha, tm=1024, tf=512):
    M, D = x.shape
    FF = w1.shape[1]
    tm = min(tm, M)
    return pl.pallas_call(
        functools.partial(_mlp_ln_kernel, alpha),
        grid=(M // tm, FF // tf),
        in_specs=[pl.BlockSpec((tm, D), lambda m, f: (m, 0)),
                  pl.BlockSpec((D, tf), lambda m, f: (0, f)),
                  pl.BlockSpec((tf, D), lambda m, f: (f, 0)),
                  pl.BlockSpec((1, D), lambda m, f: (0, 0)),
                  pl.BlockSpec((1, D), lambda m, f: (0, 0))],
        out_specs=pl.BlockSpec((tm, D), lambda m, f: (m, 0)),
        out_shape=jax.ShapeDtypeStruct((M, D), F32),
        scratch_shapes=[pltpu.VMEM((tm, D), F32)],
        compiler_params=_cparams(("arbitrary", "arbitrary")),
        name="mlp_ln",
    )(x, w1, w2, g.reshape(1, D), b.reshape(1, D))


def _matmul_kernel(x_ref, w_ref, o_ref):
    o_ref[...] = jnp.dot(x_ref[...].astype(BF16), w_ref[...], preferred_element_type=F32)


def _matmul(x, w):
    M, K = x.shape
    N = w.shape[1]
    return pl.pallas_call(
        _matmul_kernel,
        grid=(1,),
        in_specs=[pl.BlockSpec((M, K), lambda i: (0, 0)), pl.BlockSpec((K, N), lambda i: (0, 0))],
        out_specs=pl.BlockSpec((M, N), lambda i: (0, 0)),
        out_shape=jax.ShapeDtypeStruct((M, N), F32),
        compiler_params=_cparams(("arbitrary",)),
        name="sample_proj",
    )(x, w)


_SEQ_BLOCK = 8


def _dec_index_kernel(topk, n_pages, pt_ref, qi_ref, wi_ref, kiraw_ref, g_ref, b_ref, cidx_ref,
                      selb_ref, newsel_ref, kiln_ref, buf, sem, sc_ref, snew_ref):
    step = pl.program_id(0)
    nseq = pl.num_programs(0) * _SEQ_BLOCK
    L = n_pages * TILE

    def page_copy(seq, slot, p):
        return pltpu.make_async_copy(cidx_ref.at[pt_ref[seq, p]], buf.at[slot, p], sem.at[slot])

    def start(seq, slot):
        for p in range(n_pages):
            page_copy(seq, slot, p).start()

    def wait(seq, slot):
        for p in range(n_pages):
            page_copy(seq, slot, p).wait()

    @pl.when(step == 0)
    def _():
        start(0, 0)

    kiln_ref[...] = _ln(kiraw_ref[...], g_ref[...], b_ref[...])

    def seq_body(i, carry):
        seq = step * _SEQ_BLOCK + i
        slot = i % 2

        @pl.when(seq + 1 < nseq)
        def _():
            start(seq + 1, 1 - slot)

        wait(seq, slot)
        xk = jnp.concatenate([buf[slot, p] for p in range(n_pages)], axis=1).astype(BF16)
        qib = qi_ref[i].astype(BF16)
        s = jnp.dot(qib, xk, preferred_element_type=F32)
        w = wi_ref[i]
        row = jnp.sum(jnp.maximum(s, 0.0) * w, axis=0, keepdims=True) * IDX_SCALE
        sc_ref[pl.ds(i, 1), :] = jnp.where(row == 0.0, 0.0, row)
        kn = kiln_ref[pl.ds(i, 1), :].astype(BF16).astype(F32)
        sn = jnp.sum(qib.astype(F32) * kn, axis=1, keepdims=True)
        snew = jnp.sum(jnp.maximum(sn, 0.0) * w, axis=0, keepdims=True) * IDX_SCALE
        snew_ref[pl.ds(i, 1), :] = jnp.broadcast_to(jnp.where(snew == 0.0, 0.0, snew), (1, TILE))
        return carry

    lax.fori_loop(0, _SEQ_BLOCK, seq_body, 0)
    keys = sc_ref[...]
    knew = snew_ref[:, 0:1]

    def count(pred):
        return (jnp.sum(jnp.where(pred(keys), 1.0, 0.0), axis=1, keepdims=True)
                + jnp.where(pred(knew), 1.0, 0.0))

    t, _ = _kth_largest_search(lambda c: count(lambda k: k >= c), topk, (_SEQ_BLOCK, 1))
    room = topk - count(lambda k: k > t)
    ii = lax.broadcasted_iota(jnp.int32, (TILE, 2 * TILE), 0)
    jj = lax.broadcasted_iota(jnp.int32, (TILE, 2 * TILE), 1)
    pref = jnp.where(jj >= TILE, 1.0, jnp.where(ii < jj, 1.0, 0.0)).astype(BF16)
    carry = jnp.zeros((_SEQ_BLOCK, TILE), F32)
    for jt in range(n_pages):
        kt = keys[:, jt * TILE:(jt + 1) * TILE]
        eq = kt == t
        res = jnp.dot(jnp.where(eq, 1.0, 0.0).astype(BF16), pref, preferred_element_type=F32)
        take = jnp.where(res[:, :TILE] + carry < room, 1.0, 0.0)
        selb_ref[:, jt * TILE:(jt + 1) * TILE] = jnp.where(kt > t, 1.0, jnp.where(eq, take, 0.0))
        carry = carry + res[:, TILE:]
    newsel = jnp.where(knew > t, 1.0, jnp.where(knew == t, jnp.where(carry[:, :1] < room, 1.0, 0.0), 0.0))
    newsel_ref[...] = jnp.broadcast_to(newsel, (_SEQ_BLOCK, TILE))


def _dec_index(page_table, qi, wi, kiraw, g, b, cidx, topk):
    DB, n_pages = page_table.shape
    L = n_pages * TILE
    sb = _SEQ_BLOCK
    grid_spec = pltpu.PrefetchScalarGridSpec(
        num_scalar_prefetch=1,
        grid=(DB // sb,),
        in_specs=[pl.BlockSpec((sb, IDX_HEADS, IDX_DIM), lambda s, pt: (s, 0, 0)),
                  pl.BlockSpec((sb, IDX_HEADS, 1), lambda s, pt: (s, 0, 0)),
                  pl.BlockSpec((sb, IDX_DIM), lambda s, pt: (s, 0)),
                  pl.BlockSpec((1, IDX_DIM), lambda s, pt: (0, 0)),
                  pl.BlockSpec((1, IDX_DIM), lambda s, pt: (0, 0)),
                  pl.BlockSpec(memory_space=pl.ANY)],
        out_specs=[pl.BlockSpec((sb, L), lambda s, pt: (s, 0)),
                   pl.BlockSpec((sb, TILE), lambda s, pt: (s, 0)),
                   pl.BlockSpec((sb, IDX_DIM), lambda s, pt: (s, 0))],
        scratch_shapes=[pltpu.VMEM((2, n_pages, IDX_DIM, TILE), F32),
                        pltpu.SemaphoreType.DMA((2,)),
                        pltpu.VMEM((sb, L), F32),
                        pltpu.VMEM((sb, TILE), F32)])
    return pl.pallas_call(
        functools.partial(_dec_index_kernel, topk, n_pages),
        grid_spec=grid_spec,
        out_shape=[jax.ShapeDtypeStruct((DB, L), F32),
                   jax.ShapeDtypeStruct((DB, TILE), F32),
                   jax.ShapeDtypeStruct((DB, IDX_DIM), F32)],
        compiler_params=_cparams(("arbitrary",)),
        name="dec_index",
    )(page_table, qi, wi, kiraw, g.reshape(1, -1), b.reshape(1, -1), cidx)


_PAGE_CHUNK = 8


def _decode_attention(n_pages, q_ref, selb_ref, newsel_ref, kn_ref, vn_ref, dec_ref, kbuf, vbuf, o_ref):
    nch = n_pages // _PAGE_CHUNK
    CL = _PAGE_CHUNK * TILE
    qb = q_ref[0]
    far = dec_ref[1]
    near = dec_ref[0]
    pages = lambda buf, c: jnp.concatenate(
        [buf[c * _PAGE_CHUNK + p] for p in range(_PAGE_CHUNK)], axis=1).astype(BF16)
    ss = []
    for c in range(nch):
        s = jnp.dot(qb, pages(kbuf, c), preferred_element_type=F32)
        bias = jnp.concatenate([far] * (_PAGE_CHUNK - 1) + [near if c + 1 == nch else far], axis=1)
        sel = selb_ref[0, :, c * CL:(c + 1) * CL] > 0.5
        ss.append(jnp.where(sel, s + bias, NEG))
    kn = kn_ref[0].astype(BF16).astype(F32)
    sn = jnp.sum(qb.astype(F32) * kn, axis=1, keepdims=True) + dec_ref[2][:, 0:1]
    sn = jnp.where(newsel_ref[0, :, 0:1] > 0.5, sn, NEG)
    m = jnp.maximum(jnp.max(functools.reduce(jnp.maximum, ss), axis=1, keepdims=True), sn)
    pn = jnp.exp(sn - m)
    ps = [jnp.exp(s - m) for s in ss]
    l = jnp.sum(functools.reduce(jnp.add, ps), axis=1, keepdims=True) + pn
    acc = pn * vn_ref[0]
    for c in range(nch):
        acc = acc + lax.dot_general(ps[c].astype(BF16), pages(vbuf, c), _NT, preferred_element_type=F32)
    o_ref[0] = acc / l


def _dsa_mixer_kernel(topk, n_pages, pt_ref,
                      qiT_ref, wiT_ref, ki_ref, qT_ref, kh_ref, vT_ref, toep_ref,
                      qd_ref, selb_ref, newsel_ref, kn_ref, vn_ref, dec_ref, ck_ref, cv_ref,
                      o_ref, od_ref, skey_ref, sel_ref, m_ref, acc_ref, kbuf, vbuf, sem):
    n = pl.program_id(1)
    nq = pl.num_programs(1)
    seq = pl.program_id(0) * nq + n
    nseq = pl.num_programs(0) * nq

    def copies(s_):
        out = []
        for p in range(n_pages):
            phys = pt_ref[s_, p]
            out.append(pltpu.make_async_copy(ck_ref.at[phys], kbuf.at[p], sem.at[0]))
            out.append(pltpu.make_async_copy(cv_ref.at[phys], vbuf.at[p], sem.at[1]))
        return out

    @pl.when(seq == 0)
    def _():
        for cp in copies(0):
            cp.start()

    for cp in copies(seq):
        cp.wait()
    _decode_attention(n_pages, qd_ref, selb_ref, newsel_ref, kn_ref, vn_ref, dec_ref, kbuf, vbuf, od_ref)

    @pl.when(seq + 1 < nseq)
    def _():
        for cp in copies(seq + 1):
            cp.start()

    _dsa_prompt_block(topk, n, qiT_ref, wiT_ref, ki_ref, qT_ref, kh_ref, vT_ref, toep_ref, o_ref,
                      skey_ref, sel_ref, m_ref, acc_ref)


def _dsa_mixer(page_table, qiT, wiT, kib, qT, kh, vT, toep, topk, qbd, selb, newsel, kn, vn, dec, ck, cv):
    B, hq, S = qT.shape
    nq = S // TILE
    DB, n_pages = page_table.shape
    L = n_pages * TILE
    kvd = KV_HEADS_A * HEAD_DIM
    assert nq % _KEY_UNROLL == 0 and n_pages % _PAGE_CHUNK == 0
    assert DB == B * nq
    cw = (N_HEADS // KV_HEADS_A) * TILE
    dseq = lambda b, n, pt: (b * nq + n, 0, 0)
    grid_spec = pltpu.PrefetchScalarGridSpec(
        num_scalar_prefetch=1,
        grid=(B, nq),
        in_specs=[pl.BlockSpec((1, qiT.shape[1], TILE), lambda b, n, pt: (b, 0, n)),
                  pl.BlockSpec((1, IDX_HEADS, TILE), lambda b, n, pt: (b, 0, n)),
                  pl.BlockSpec((1, S, IDX_DIM), lambda b, n, pt: (b, 0, 0)),
                  pl.BlockSpec((1, hq, TILE), lambda b, n, pt: (b, 0, n)),
                  pl.BlockSpec((1, KV_HEADS_A, S, HEAD_DIM), lambda b, n, pt: (b, 0, 0, 0)),
                  pl.BlockSpec((1, kvd, S), lambda b, n, pt: (b, 0, 0)),
                  pl.BlockSpec(toep.shape, lambda b, n, pt: (0, 0, 0)),
                  pl.BlockSpec((1, N_HEADS, kvd), dseq),
                  pl.BlockSpec((1, 1, L), dseq),
                  pl.BlockSpec((1, 1, TILE), dseq),
                  pl.BlockSpec((1, 1, kvd), dseq),
                  pl.BlockSpec((1, 1, kvd), dseq),
                  pl.BlockSpec(dec.shape, lambda b, n, pt: (0, 0, 0)),
                  pl.BlockSpec(memory_space=pl.ANY),
                  pl.BlockSpec(memory_space=pl.ANY)],
        out_specs=[pl.BlockSpec((1, hq, TILE), lambda b, n, pt: (b, 0, n)),
                   pl.BlockSpec((1, N_HEADS, kvd), dseq)],
        scratch_shapes=[pltpu.VMEM((S, TILE), F32), pltpu.VMEM((S, TILE), F32),
                        pltpu.VMEM((KV_HEADS_A, 8, cw), F32),
                        pltpu.VMEM((KV_HEADS_A, HEAD_DIM + _SUM_ROWS, cw), F32),
                        pltpu.VMEM((n_pages, kvd, TILE), F32),
                        pltpu.VMEM((n_pages, kvd, TILE), F32),
                        pltpu.SemaphoreType.DMA((2,))])
    return pl.pallas_call(
        functools.partial(_dsa_mixer_kernel, topk, n_pages),
        grid_spec=grid_spec,
        out_shape=[jax.ShapeDtypeStruct((B, hq, S), BF16),
                   jax.ShapeDtypeStruct((DB, N_HEADS, kvd), F32)],
        compiler_params=_cparams(("arbitrary", "arbitrary"), _VMEM_LIMIT_MIXER),
        name="dsa_mixer",
    )(page_table, qiT, wiT, kib, qT, kh, vT, toep,
      qbd, selb.reshape(DB, 1, L), newsel.reshape(DB, 1, TILE), kn.reshape(DB, 1, kvd), vn.reshape(DB, 1, kvd),
      dec, ck, cv)


def _swa_dec_kernel(q_ref, wk_ref, wv_ref, kn_ref, vn_ref, dec_ref, sink_ref, o_ref):
    lane = lax.broadcasted_iota(jnp.int32, (N_HEADS, TILE), 1)
    bias = dec_ref[0]
    b0 = dec_ref[2][:, 0:1]
    sk = sink_ref[...]
    for i in range(_SEQ_BLOCK):
        qb = q_ref[i]
        s = jnp.dot(qb, wk_ref[i].astype(BF16), preferred_element_type=F32)
        s = jnp.where(lane >= 1, s + bias, NEG)
        kn = kn_ref[i].astype(BF16).astype(F32)
        sn = jnp.sum(qb.astype(F32) * kn, axis=1, keepdims=True) + b0
        m = jnp.maximum(jnp.maximum(jnp.max(s, axis=1, keepdims=True), sn), sk)
        p = jnp.exp(s - m)
        pn = jnp.exp(sn - m)
        l = jnp.sum(p, axis=1, keepdims=True) + pn + jnp.exp(sk - m)
        out = lax.dot_general(p.astype(BF16), wv_ref[i].astype(BF16), _NT, preferred_element_type=F32) + pn * vn_ref[i]
        o_ref[i] = out / l


def _swa_dec(qbd, wk, wv, kn, vn, dec, sink):
    DB = qbd.shape[0]
    kvd = KV_HEADS_B * HEAD_DIM
    sb = _SEQ_BLOCK
    return pl.pallas_call(
        _swa_dec_kernel,
        grid=(DB // sb,),
        in_specs=[pl.BlockSpec((sb, N_HEADS, kvd), lambda s: (s, 0, 0)),
                  pl.BlockSpec((sb, kvd, WINDOW), lambda s: (s, 0, 0)),
                  pl.BlockSpec((sb, kvd, WINDOW), lambda s: (s, 0, 0)),
                  pl.BlockSpec((sb, 1, kvd), lambda s: (s, 0, 0)),
                  pl.BlockSpec((sb, 1, kvd), lambda s: (s, 0, 0)),
                  pl.BlockSpec(dec.shape, lambda s: (0, 0, 0)),
                  pl.BlockSpec((N_HEADS, 1), lambda s: (0, 0))],
        out_specs=pl.BlockSpec((sb, N_HEADS, kvd), lambda s: (s, 0, 0)),
        out_shape=jax.ShapeDtypeStruct((DB, N_HEADS, kvd), F32),
        compiler_params=_cparams(("arbitrary",)),
        name="swa_dec",
    )(qbd, wk, wv, kn.reshape(DB, 1, kvd), vn.reshape(DB, 1, kvd), dec, sink.reshape(N_HEADS, 1))


def _block_diag_q(q, nkv):
    DB = q.shape[0]
    G = N_HEADS // nkv
    qh = (q * ATTN_SCALE).reshape(DB, N_HEADS, 1, HEAD_DIM)
    own = (np.arange(N_HEADS)[:, None] // G == np.arange(nkv)[None, :]).astype(np.float32)
    return (qh * own[None, :, :, None]).reshape(DB, N_HEADS, nkv * HEAD_DIM).astype(BF16)


def _own_group(o, nkv):
    DB = o.shape[0]
    G = N_HEADS // nkv
    o5 = o.reshape(DB, nkv, G, nkv, HEAD_DIM)
    return jnp.stack([o5[:, g, :, g, :] for g in range(nkv)], axis=1).reshape(DB, N_HEADS * HEAD_DIM)


def _pad_cols(w, mult=TILE):
    pad = (-w.shape[1]) % mult
    return jnp.pad(w, ((0, 0), (0, pad))) if pad else w


def _slots_minor(a):
    n, t, h, d = a.shape
    return a.transpose(0, 2, 3, 1).reshape(n, h * d, t)


def _slots_major(a, heads):
    n, hd, t = a.shape
    return a.reshape(n, heads, hd // heads, t).transpose(0, 3, 1, 2)


def kernel(x_prompt, x_sample, cache_k, cache_v, cache_idx_k, state_win_k, state_win_v, page_table, rel_bias_table, a_w_in, a_w_out, a_idx_ln_g, a_idx_ln_b, b_w_in, b_w_out, b_sink, ln1_g, ln1_b, ln2_g, ln2_b, mlp_w1, mlp_w2):
    B, S, D = x_prompt.shape
    DB, DS, _ = x_sample.shape
    depth = ln1_g.shape[0]
    assert depth == 2 and DS == 1 and S % TILE == 0 and DB % _SEQ_BLOCK == 0
    assert state_win_k.shape[2] == WINDOW and cache_k.shape[2] == TILE
    alpha = (2 * depth) ** 0.25
    n_pages = page_table.shape[1]
    past = n_pages * TILE
    hq = N_HEADS * HEAD_DIM
    kva = KV_HEADS_A * HEAD_DIM
    kvb = KV_HEADS_B * HEAD_DIM
    hi = IDX_HEADS * IDX_DIM

    toep, toep2, dec = _bias_tables(rel_bias_table)

    wa = a_w_in[0]
    c0, c1, c2, c3, c4 = hq, hq + kva, hq + 2 * kva, hq + 2 * kva + hi, hq + 2 * kva + hi + IDX_DIM
    wq, wk, wv, wqi, wki, wwi = wa[:, :c0], wa[:, c0:c1], wa[:, c1:c2], wa[:, c2:c3], wa[:, c3:c4], wa[:, c4:]
    a_wo = a_w_out[0].astype(BF16)
    w1 = [mlp_w1[i].astype(BF16) for i in range(depth)]
    w2 = [mlp_w2[i].astype(BF16) for i in range(depth)]

    qT, kTp, vTp, kh, vT, qiT, kiTp, kib, wiT = _prompt_proj(
        x_prompt, wq, wk, wv, KV_HEADS_A, idx=(wqi, wki, wwi, a_idx_ln_g[0], a_idx_ln_b[0]),
        q_scale=ATTN_SCALE * LOG2E)
    xs = x_sample.reshape(DB, D)
    proj = _matmul(xs, _pad_cols(wa).astype(BF16))
    q_s, k_s, v_s = proj[:, :c0], proj[:, c0:c1], proj[:, c1:c2]
    qi_s, kiraw_s, wi_s = proj[:, c2:c3], proj[:, c3:c4], proj[:, c4:c4 + IDX_HEADS]
    selb, newsel, ki_s = _dec_index(
        page_table, qi_s.reshape(DB, IDX_HEADS, IDX_DIM), wi_s.reshape(DB, IDX_HEADS, 1), kiraw_s,
        a_idx_ln_g[0], a_idx_ln_b[0], cache_idx_k[0].transpose(0, 2, 1), min(TOPK_MAX, (past + DS) // 4))
    oT, o_s = _dsa_mixer(page_table, qiT, wiT, kib, qT, kh, vT, toep2, min(TOPK_MAX, S // 4),
                         _block_diag_q(q_s, KV_HEADS_A), selb, newsel, k_s, v_s, dec,
                         _slots_minor(cache_k[0]), _slots_minor(cache_v[0]))

    attn = oT.transpose(0, 2, 1).reshape(B * S, hq)
    y_p = _outproj_ln(x_prompt.reshape(B * S, D), attn, a_wo, ln1_g[0], ln1_b[0], alpha)
    y_p = _mlp_ln(y_p, w1[0], w2[0], ln2_g[0], ln2_b[0], alpha)

    y_s = _outproj_ln(xs, _own_group(o_s, KV_HEADS_A).astype(BF16), a_wo, ln1_g[0], ln1_b[0], alpha)
    y_s = _mlp_ln(y_s, w1[0], w2[0], ln2_g[0], ln2_b[0], alpha)

    wb = b_w_in[0]
    bq, bk, bv = wb[:, :hq], wb[:, hq:hq + kvb], wb[:, hq + kvb:]
    b_wo = b_w_out[0].astype(BF16)

    qT1, k1Tp, v1Tp, kh1, vT1 = _prompt_proj(y_p.reshape(B, S, D), bq, bk, bv, KV_HEADS_B)
    oT1 = _swa_prompt(qT1, kh1, vT1, toep, b_sink[0])
    attn1 = oT1.transpose(0, 2, 1).reshape(B * S, hq)
    y_p = _outproj_ln(y_p, attn1, b_wo, ln1_g[1], ln1_b[1], alpha)
    y_p = _mlp_ln(y_p, w1[1], w2[1], ln2_g[1], ln2_b[1], alpha)

    proj1 = _matmul(y_s, wb.astype(BF16))
    q1_s, k1_s, v1_s = proj1[:, :hq], proj1[:, hq:hq + kvb], proj1[:, hq + kvb:]
    win_k = _slots_minor(state_win_k[0])
    win_v = _slots_minor(state_win_v[0])
    o1_s = _swa_dec(_block_diag_q(q1_s, KV_HEADS_B), win_k, win_v, k1_s, v1_s, dec, b_sink[0])
    y_s = _outproj_ln(y_s, _own_group(o1_s, KV_HEADS_B).astype(BF16), b_wo, ln1_g[1], ln1_b[1], alpha)
    y_s = _mlp_ln(y_s, w1[1], w2[1], ln2_g[1], ln2_b[1], alpha)

    npg = S // TILE
    assert min(WINDOW, S) == TILE
    pages = lambda a, heads: _slots_major(a.reshape(B * npg, -1, TILE), heads).reshape(1, B, npg, TILE, heads, HEAD_DIM)
    new_wk_s = jnp.concatenate([win_k[:, :, 1:], k1_s[:, :, None]], axis=2)
    new_wv_s = jnp.concatenate([win_v[:, :, 1:], v1_s[:, :, None]], axis=2)
    return (y_p.reshape(B, S, D),
            y_s.reshape(DB, DS, D),
            pages(kTp, KV_HEADS_A),
            pages(vTp, KV_HEADS_A),
            kiTp.transpose(0, 1, 3, 2).reshape(1, B, npg, TILE, IDX_DIM),
            k_s.reshape(1, DB, DS, KV_HEADS_A, HEAD_DIM),
            v_s.reshape(1, DB, DS, KV_HEADS_A, HEAD_DIM),
            ki_s.reshape(1, DB, DS, IDX_DIM),
            _slots_major(k1Tp[:, npg - 1], KV_HEADS_B)[None],
            _slots_major(v1Tp[:, npg - 1], KV_HEADS_B)[None],
            _slots_major(new_wk_s, KV_HEADS_B)[None],
            _slots_major(new_wv_s, KV_HEADS_B)[None])
```

```python
import functools
import math

import numpy as np
import jax
import jax.numpy as jnp
from jax import lax
from jax.experimental import pallas as pl
from jax.experimental.pallas import tpu as pltpu

F32 = jnp.float32
BF16 = jnp.bfloat16

N_HEADS = 16
HEAD_DIM = 64
KV_HEADS_A = 4
KV_HEADS_B = 2
IDX_HEADS = 8
IDX_DIM = 64
TOPK_MAX = 256
WINDOW = 128
TILE = 128
N_BUCKETS = 32
MAX_DISTANCE = 128
LN_EPS = 1e-5
NEG = -1e30
INT_MIN = -(2 ** 31)
ATTN_SCALE = HEAD_DIM ** -0.5
LOG2E = math.log2(math.e)
IDX_SCALE = IDX_HEADS ** -0.5 * IDX_DIM ** -0.5
FAR_BUCKET = N_BUCKETS - 1
FAR_DISTANCE = 113

_NT = (((1,), (1,)), ((), ()))
_VMEM_LIMIT = 48 * 1024 * 1024
_VMEM_LIMIT_MIXER = 58 * 1024 * 1024


def _cparams(sem, vmem_limit=_VMEM_LIMIT):
    return pltpu.CompilerParams(dimension_semantics=sem, vmem_limit_bytes=vmem_limit)


def _bucket_np(dist):
    n = np.maximum(dist, 0)
    max_exact = N_BUCKETS // 2
    nf = np.maximum(n, max_exact).astype(np.float64)
    val = np.log(nf / max_exact) / math.log(MAX_DISTANCE / max_exact) * (N_BUCKETS - max_exact)
    frac = val - np.floor(val)
    interior = (n > max_exact) & (n < MAX_DISTANCE)
    assert not np.any(interior & ((frac < 1e-6) | (frac > 1 - 1e-6))), "bucket boundary too close to an integer"
    large = np.minimum(max_exact + val.astype(np.int32), N_BUCKETS - 1)
    out = np.where(n < max_exact, n, large).astype(np.int32)
    assert np.all(out[n >= FAR_DISTANCE] == FAR_BUCKET)
    return out


def _key_to_float(k):
    return lax.bitcast_convert_type(k ^ (lax.shift_right_arithmetic(k, 31) & 0x7FFFFFFF), F32)


def _kth_largest_search(count_ge, topk, shape):
    def bit_body(b, carry):
        t, cnt_t = carry
        cand = t ^ lax.shift_left(jnp.int32(1), 31 - b)
        cnt = count_ge(_key_to_float(cand))
        ok = cnt >= topk
        return jnp.where(ok, cand, t), jnp.where(ok, cnt, cnt_t)

    init = (jnp.full(shape, INT_MIN, jnp.int32), jnp.full(shape, 3e38, F32))
    t, cnt_t = lax.fori_loop(0, 32, bit_body, init)
    return _key_to_float(t), cnt_t


def _ln(y, g, b):
    mu = jnp.mean(y, axis=-1, keepdims=True)
    yc = y - mu
    var = jnp.mean(yc * yc, axis=-1, keepdims=True)
    return yc * lax.rsqrt(var + LN_EPS) * g + b


def _bias_kernel(table_ref, bk_ref, bkd_ref, toep_ref, toep2_ref, dec_ref):
    bk = bk_ref[...]
    bkd = bkd_ref[...]
    dec_rows = []
    for h in range(N_HEADS):
        acc = jnp.zeros(bk.shape, F32)
        accd = jnp.zeros(bkd.shape, F32)
        for b in range(N_BUCKETS):
            val = table_ref[b, h]
            acc = jnp.where(bk == b, val, acc)
            accd = jnp.where(bkd == b, val, accd)
        toep_ref[h] = acc
        toep2_ref[h] = acc * LOG2E
        dec_rows.append(accd)
    for kind in range(3):
        dec_ref[kind] = jnp.concatenate([d[kind:kind + 1, :] for d in dec_rows], axis=0)


def _bias_tables(table):
    i = np.arange(TILE)[None, :]
    j = np.arange(2 * TILE)[:, None]
    bk = np.concatenate([np.full((TILE, TILE), FAR_BUCKET, np.int32), _bucket_np(i + TILE - j)], axis=0)
    bkd = np.zeros((8, TILE), np.int32)
    bkd[0] = _bucket_np(TILE - np.arange(TILE))
    bkd[1] = FAR_BUCKET
    bkd[2] = 0
    return pl.pallas_call(
        _bias_kernel,
        grid=(1,),
        in_specs=[pl.BlockSpec(memory_space=pltpu.SMEM),
                  pl.BlockSpec((3 * TILE, TILE), lambda h: (0, 0)),
                  pl.BlockSpec((8, TILE), lambda h: (0, 0))],
        out_specs=[pl.BlockSpec((N_HEADS, 3 * TILE, TILE), lambda h: (0, 0, 0)),
                   pl.BlockSpec((N_HEADS, 3 * TILE, TILE), lambda h: (0, 0, 0)),
                   pl.BlockSpec((3, N_HEADS, TILE), lambda h: (0, 0, 0))],
        out_shape=[jax.ShapeDtypeStruct((N_HEADS, 3 * TILE, TILE), F32),
                   jax.ShapeDtypeStruct((N_HEADS, 3 * TILE, TILE), F32),
                   jax.ShapeDtypeStruct((3, N_HEADS, TILE), F32)],
        compiler_params=_cparams(("arbitrary",)),
        name="bias_tables",
    )(table, jnp.asarray(bk), jnp.asarray(bkd))


def _normalize_cols(y):
    mu = jnp.mean(y, axis=0, keepdims=True)
    yc = y - mu
    var = jnp.mean(yc * yc, axis=0, keepdims=True)
    return yc * lax.rsqrt(var + LN_EPS)


def _proj_kernel(nkv, has_idx, *refs):
    if has_idx:
        (x_ref, wqT_ref, wkT_ref, wvT_ref, wkh_ref, wqiT_ref, wkiT_ref, wki_ref, wwiT_ref, g_ref, b_ref,
         gc_ref, bc_ref, qT_ref, kTp_ref, vTp_ref, kh_ref, vT_ref, qiT_ref, kiTp_ref, kib_ref, wiT_ref) = refs
    else:
        (x_ref, wqT_ref, wkT_ref, wvT_ref, wkh_ref, qT_ref, kTp_ref, vTp_ref, kh_ref, vT_ref) = refs
    xb = x_ref[0].astype(BF16)
    npg = xb.shape[0] // TILE
    qT_ref[0] = lax.dot_general(wqT_ref[...], xb, _NT, preferred_element_type=F32).astype(BF16)
    kT = lax.dot_general(wkT_ref[...], xb, _NT, preferred_element_type=F32)
    vT = lax.dot_general(wvT_ref[...], xb, _NT, preferred_element_type=F32)
    vT_ref[0] = vT.astype(BF16)
    for j in range(npg):
        kTp_ref[0, j] = kT[:, j * TILE:(j + 1) * TILE]
        vTp_ref[0, j] = vT[:, j * TILE:(j + 1) * TILE]
    for g in range(nkv):
        kh_ref[0, g] = jnp.dot(xb, wkh_ref[g], preferred_element_type=F32).astype(BF16)
    if has_idx:
        qiT_ref[0] = lax.dot_general(wqiT_ref[...], xb, _NT, preferred_element_type=F32).astype(BF16)
        kiT = _normalize_cols(lax.dot_general(wkiT_ref[...], xb, _NT, preferred_element_type=F32))
        for j in range(npg):
            kiTp_ref[0, j] = kiT[:, j * TILE:(j + 1) * TILE] * gc_ref[...] + bc_ref[...]
        ki = _ln(jnp.dot(xb, wki_ref[...], preferred_element_type=F32), g_ref[...], b_ref[...])
        kib_ref[0] = ki.astype(BF16)
        wiT_ref[0] = lax.dot_general(wwiT_ref[...], xb, _NT, preferred_element_type=F32)


def _prompt_proj(x, wq, wk, wv, nkv, idx=None, q_scale=ATTN_SCALE, tm=512):
    B, S, D = x.shape
    tm = min(tm, S)
    kvd = nkv * HEAD_DIM
    hq = N_HEADS * HEAD_DIM
    wqT = (wq * q_scale).T.astype(BF16)
    wkh = wk.reshape(D, nkv, HEAD_DIM).transpose(1, 0, 2).astype(BF16)
    full2 = lambda a: pl.BlockSpec(a.shape, lambda b, m: (0,) * a.ndim)
    ins = [x, wqT, wk.T.astype(BF16), wv.T.astype(BF16), wkh]
    in_specs = [pl.BlockSpec((1, tm, D), lambda b, m: (b, m, 0))] + [full2(a) for a in ins[1:]]
    out_shape = [jax.ShapeDtypeStruct((B, hq, S), BF16),
                 jax.ShapeDtypeStruct((B, S // TILE, kvd, TILE), F32),
                 jax.ShapeDtypeStruct((B, S // TILE, kvd, TILE), F32),
                 jax.ShapeDtypeStruct((B, nkv, S, HEAD_DIM), BF16),
                 jax.ShapeDtypeStruct((B, kvd, S), BF16)]
    out_specs = [pl.BlockSpec((1, hq, tm), lambda b, m: (b, 0, m)),
                 pl.BlockSpec((1, tm // TILE, kvd, TILE), lambda b, m: (b, m, 0, 0)),
                 pl.BlockSpec((1, tm // TILE, kvd, TILE), lambda b, m: (b, m, 0, 0)),
                 pl.BlockSpec((1, nkv, tm, HEAD_DIM), lambda b, m: (b, 0, m, 0)),
                 pl.BlockSpec((1, kvd, tm), lambda b, m: (b, 0, m))]
    if idx is not None:
        wqi, wki, wwi, g, bb = idx
        extra = [wqi.T.astype(BF16), wki.T.astype(BF16), wki.astype(BF16), wwi.T.astype(BF16),
                 g.reshape(1, -1), bb.reshape(1, -1),
                 jnp.broadcast_to(g.reshape(-1, 1), (IDX_DIM, TILE)), jnp.broadcast_to(bb.reshape(-1, 1), (IDX_DIM, TILE))]
        ins += extra
        in_specs += [full2(a) for a in extra]
        hi = IDX_HEADS * IDX_DIM
        out_shape += [jax.ShapeDtypeStruct((B, hi, S), BF16),
                      jax.ShapeDtypeStruct((B, S // TILE, IDX_DIM, TILE), F32),
                      jax.ShapeDtypeStruct((B, S, IDX_DIM), BF16),
                      jax.ShapeDtypeStruct((B, IDX_HEADS, S), F32)]
        out_specs += [pl.BlockSpec((1, hi, tm), lambda b, m: (b, 0, m)),
                      pl.BlockSpec((1, tm // TILE, IDX_DIM, TILE), lambda b, m: (b, m, 0, 0)),
                      pl.BlockSpec((1, tm, IDX_DIM), lambda b, m: (b, m, 0)),
                      pl.BlockSpec((1, IDX_HEADS, tm), lambda b, m: (b, 0, m))]
    return pl.pallas_call(
        functools.partial(_proj_kernel, nkv, idx is not None),
        grid=(B, S // tm),
        in_specs=in_specs,
        out_specs=out_specs,
        out_shape=out_shape,
        compiler_params=_cparams(("arbitrary", "arbitrary")),
        name="prompt_proj_idx" if idx is not None else "prompt_proj",
    )(*ins)


_KEY_UNROLL = 4
_SUM_ROWS = 16
_ATT_TILES = 2


def _dsa_prompt_block(topk, n, qiT_ref, wiT_ref, ki_ref, qT_ref, kh_ref, vT_ref, toep_ref, o_ref,
                      skey_ref, sel_ref, m_ref, acc_ref):
    ST = _KEY_UNROLL * TILE
    ns = n // _KEY_UNROLL + 1
    G = N_HEADS // KV_HEADS_A
    rows = lax.broadcasted_iota(jnp.int32, (ST, TILE), 0)
    lanes = lax.broadcasted_iota(jnp.int32, (ST, TILE), 1)
    qpos = n * TILE + lanes

    qi = qiT_ref[0]
    wi = wiT_ref[0]

    def score_body(u, carry):
        off = pl.multiple_of(u * ST, ST)
        kij = ki_ref[0, pl.ds(off, ST), :]
        acc = jnp.zeros((ST, TILE), F32)
        for h in range(0, IDX_HEADS, 2):
            qpair = jnp.concatenate([qi[IDX_DIM * h:IDX_DIM * (h + 1), :],
                                     qi[IDX_DIM * (h + 1):IDX_DIM * (h + 2), :]], axis=1)
            s = jnp.dot(kij, qpair, preferred_element_type=F32)
            acc = acc + jnp.maximum(s[:, :TILE], 0.0) * wi[h:h + 1, :]
            acc = acc + jnp.maximum(s[:, TILE:], 0.0) * wi[h + 1:h + 2, :]
        sc = acc * IDX_SCALE
        sc = jnp.where(off + rows <= qpos, sc, NEG)
        sc = jnp.where(sc == 0.0, 0.0, sc)
        skey_ref[pl.ds(off, ST), :] = sc
        return carry

    lax.fori_loop(0, ns, score_body, 0)

    def count(pred):
        def body(u, c):
            off = pl.multiple_of(u * ST, ST)
            x = jnp.where(pred(skey_ref[pl.ds(off, ST), :]), 1.0, 0.0).reshape(ST // 8, 8, TILE)
            while x.shape[0] > 1:
                half = x.shape[0] // 2
                x = x[:half] + x[half:]
            return c + x[0]
        c8 = lax.fori_loop(0, ns, body, jnp.zeros((8, TILE), F32))
        return jnp.sum(c8, axis=0, keepdims=True)

    t, cnt_t = _kth_largest_search(lambda c: count(lambda kt: kt >= c), topk, (1, TILE))
    exact_fit = jnp.max(jnp.abs(cnt_t - topk)) == 0.0

    @pl.when(exact_fit)
    def _():
        def sel_body(u, carry):
            off = pl.multiple_of(u * ST, ST)
            sel = jnp.where(skey_ref[pl.ds(off, ST), :] >= t, 1.0, 0.0)
            sel_ref[pl.ds(off, ST), :] = jnp.where(off + rows <= qpos, sel, 0.0)
            return carry

        lax.fori_loop(0, ns, sel_body, 0)

    @pl.when(jnp.logical_not(exact_fit))
    def _():
        room = topk - count(lambda kt: kt > t)
        ii = lax.broadcasted_iota(jnp.int32, (TILE, TILE), 0)
        jj = lax.broadcasted_iota(jnp.int32, (TILE, TILE), 1)
        lower = jnp.where(jj < ii, 1.0, 0.0).astype(BF16)
        ones8 = jnp.ones((8, TILE), BF16)

        def sel_body(u, carry):
            for k in range(_KEY_UNROLL):
                off = pl.multiple_of(u * ST + k * TILE, TILE)
                kt = skey_ref[pl.ds(off, TILE), :]
                eq = kt == t
                eqb = jnp.where(eq, 1.0, 0.0).astype(BF16)
                before = jnp.dot(lower, eqb, preferred_element_type=F32) + carry
                tot = jnp.dot(ones8, eqb, preferred_element_type=F32)[0:1]
                sel = jnp.where(kt > t, 1.0, jnp.where(eq, jnp.where(before < room, 1.0, 0.0), 0.0))
                sel_ref[pl.ds(off, TILE), :] = jnp.where(off + ii <= n * TILE + jj, sel, 0.0)
                carry = carry + tot
            return carry

        lax.fori_loop(0, ns, sel_body, jnp.zeros((1, TILE), F32))

    q = qT_ref[0]
    qgs = [jnp.concatenate([q[(g * G + r) * HEAD_DIM:(g * G + r + 1) * HEAD_DIM, :] for r in range(G)], axis=1)
           for g in range(KV_HEADS_A)]
    m_ref[...] = jnp.full(m_ref.shape, NEG, F32)
    acc_ref[...] = jnp.zeros(acc_ref.shape, F32)

    AT = _ATT_TILES * TILE
    ones_rows = jnp.ones((_SUM_ROWS, AT), BF16)

    def att_body(u, carry):
        off = pl.multiple_of(u * AT, AT)
        tiles = [u * _ATT_TILES + k for k in range(_ATT_TILES)]
        roffs = [pl.multiple_of(jnp.where(j == n, 2 * TILE, jnp.where(j == n - 1, TILE, 0)), TILE) for j in tiles]
        selv = sel_ref[pl.ds(off, AT), :] > 0.5
        ss = [jnp.dot(kh_ref[0, g, pl.ds(off, AT), :], qgs[g], preferred_element_type=F32)
              for g in range(KV_HEADS_A)]
        for g in range(KV_HEADS_A):
            s = ss[g]
            s = jnp.concatenate(
                [jnp.where(selv,
                           s[:, r * TILE:(r + 1) * TILE]
                           + jnp.concatenate([toep_ref[g * G + r, pl.ds(ro, TILE), :] for ro in roffs], axis=0),
                           NEG)
                 for r in range(G)], axis=1)
            m = m_ref[g]
            m_new = jnp.maximum(m, jnp.max(s, axis=0, keepdims=True))
            alpha = jnp.exp2(m - m_new)
            p = jnp.exp2(s - m_new[0:1])
            m_ref[g] = m_new
            vt = jnp.concatenate([vT_ref[0, g * HEAD_DIM:(g + 1) * HEAD_DIM, pl.ds(off, AT)], ones_rows], axis=0)
            acc_ref[g] = acc_ref[g] * alpha[0:1] + jnp.dot(vt, p.astype(BF16), preferred_element_type=F32)
        return carry

    lax.fori_loop(0, n // _ATT_TILES + 1, att_body, 0)
    for g in range(KV_HEADS_A):
        out = acc_ref[g, 0:HEAD_DIM, :] / acc_ref[g, HEAD_DIM:HEAD_DIM + 1, :]
        for r in range(G):
            h = g * G + r
            o_ref[0, h * HEAD_DIM:(h + 1) * HEAD_DIM, :] = out[:, r * TILE:(r + 1) * TILE].astype(o_ref.dtype)


def _swa_prompt_kernel(sink_ref, qT_ref, khp_ref, khc_ref, vTp_ref, vTc_ref, toep_ref, o_ref):
    n = pl.program_id(1)
    G = N_HEADS // KV_HEADS_B
    rows = lax.broadcasted_iota(jnp.int32, (2 * TILE, TILE), 0)
    lanes = lax.broadcasted_iota(jnp.int32, (2 * TILE, TILE), 1)
    dj = rows - lanes
    inwin = jnp.where(dj >= TILE - WINDOW + 1, jnp.where(dj <= TILE, 1.0, 0.0), 0.0)
    inwin = jnp.where(rows >= TILE, inwin, jnp.where(n > 0, inwin, 0.0))
    maskb = jnp.concatenate([inwin] * G, axis=1) > 0.5
    q = qT_ref[0]
    ones_rows = jnp.ones((_SUM_ROWS, 2 * TILE), BF16)
    for g in range(KV_HEADS_B):
        heads = [g * G + r for r in range(G)]
        qg = jnp.concatenate([q[h * HEAD_DIM:(h + 1) * HEAD_DIM, :] for h in heads], axis=1)
        kk = jnp.concatenate([khp_ref[0, g], khc_ref[0, g]], axis=0)
        s = jnp.dot(kk, qg, preferred_element_type=F32)
        bias = jnp.concatenate([toep_ref[h, TILE:3 * TILE, :] for h in heads], axis=1)
        s = jnp.where(maskb, s + bias, NEG)
        sink = jnp.concatenate([jnp.full((1, TILE), sink_ref[h] * LOG2E, F32) for h in heads], axis=1)
        m = jnp.maximum(jnp.max(s, axis=0, keepdims=True), sink)
        p = jnp.exp2(s - m)
        vv = jnp.concatenate([vTp_ref[0, g * HEAD_DIM:(g + 1) * HEAD_DIM, :],
                              vTc_ref[0, g * HEAD_DIM:(g + 1) * HEAD_DIM, :]], axis=1)
        pv = jnp.dot(jnp.concatenate([vv, ones_rows], axis=0), p.astype(BF16), preferred_element_type=F32)
        out = pv[0:HEAD_DIM, :] / (pv[HEAD_DIM:HEAD_DIM + 1, :] + jnp.exp2(sink - m))
        for r, h in enumerate(heads):
            o_ref[0, h * HEAD_DIM:(h + 1) * HEAD_DIM, :] = out[:, r * TILE:(r + 1) * TILE].astype(o_ref.dtype)


def _swa_prompt(qT, kh, vT, toep, sink):
    B, hq, S = qT.shape
    nb = S // TILE
    prev = lambda n: jnp.maximum(n - 1, 0)
    return pl.pallas_call(
        _swa_prompt_kernel,
        grid=(B, nb),
        in_specs=[pl.BlockSpec(memory_space=pltpu.SMEM),
                  pl.BlockSpec((1, hq, TILE), lambda b, n: (b, 0, n)),
                  pl.BlockSpec((1, KV_HEADS_B, TILE, HEAD_DIM), lambda b, n: (b, 0, prev(n), 0)),
                  pl.BlockSpec((1, KV_HEADS_B, TILE, HEAD_DIM), lambda b, n: (b, 0, n, 0)),
                  pl.BlockSpec((1, KV_HEADS_B * HEAD_DIM, TILE), lambda b, n: (b, 0, prev(n))),
                  pl.BlockSpec((1, KV_HEADS_B * HEAD_DIM, TILE), lambda b, n: (b, 0, n)),
                  pl.BlockSpec(toep.shape, lambda b, n: (0, 0, 0))],
        out_specs=pl.BlockSpec((1, hq, TILE), lambda b, n: (b, 0, n)),
        out_shape=jax.ShapeDtypeStruct((B, hq, S), BF16),
        compiler_params=_cparams(("arbitrary", "arbitrary")),
        name="swa_prompt",
    )(sink, qT, kh, kh, vT, vT, toep)


def _outproj_ln_kernel(alpha, x_ref, a_ref, w_ref, g_ref, b_ref, o_ref):
    y = alpha * x_ref[...] + jnp.dot(a_ref[...], w_ref[...], preferred_element_type=F32)
    o_ref[...] = _ln(y, g_ref[...], b_ref[...])


def _outproj_ln(x, a, w, g, b, alpha, tm=512):
    M, D = x.shape
    tm = min(tm, M)
    return pl.pallas_call(
        functools.partial(_outproj_ln_kernel, alpha),
        grid=(M // tm,),
        in_specs=[pl.BlockSpec((tm, D), lambda m: (m, 0)),
                  pl.BlockSpec((tm, a.shape[1]), lambda m: (m, 0)),
                  pl.BlockSpec(w.shape, lambda m: (0, 0)),
                  pl.BlockSpec((1, D), lambda m: (0, 0)),
                  pl.BlockSpec((1, D), lambda m: (0, 0))],
        out_specs=pl.BlockSpec((tm, D), lambda m: (m, 0)),
        out_shape=jax.ShapeDtypeStruct((M, D), F32),
        compiler_params=_cparams(("arbitrary",)),
        name="outproj_ln",
    )(x, a, w, g.reshape(1, D), b.reshape(1, D))


def _mlp_ln_kernel(alpha, x_ref, w1_ref, w2_ref, g_ref, b_ref, o_ref, acc_ref):
    f = pl.program_id(1)

    @pl.when(f == 0)
    def _():
        acc_ref[...] = jnp.zeros_like(acc_ref)

    h = jnp.maximum(jnp.dot(x_ref[...].astype(BF16), w1_ref[...].astype(BF16), preferred_element_type=F32), 0.0)
    acc_ref[...] += jnp.dot((h * h).astype(BF16), w2_ref[...].astype(BF16), preferred_element_type=F32)

    @pl.when(f == pl.num_programs(1) - 1)
    def _():
        o_ref[...] = _ln(alpha * x_ref[...] + acc_ref[...], g_ref[...], b_ref[...])


def _mlp_ln(x, w1, w2, g, b, alpha, tm=1024, tf=512):
    M, D = x.shape
    FF = w1.shape[1]
    tm = min(tm, M)
    return pl.pallas_call(
        functools.partial(_mlp_ln_kernel, alpha),
        grid=(M // tm, FF // tf),
        in_specs=[pl.BlockSpec((tm, D), lambda m, f: (m, 0)),
                  pl.BlockSpec((D, tf), lambda m, f: (0, f)),
                  pl.BlockSpec((tf, D), lambda m, f: (f, 0)),
                  pl.BlockSpec((1, D), lambda m, f: (0, 0)),
                  pl.BlockSpec((1, D), lambda m, f: (0, 0))],
        out_specs=pl.BlockSpec((tm, D), lambda m, f: (m, 0)),
        out_shape=jax.ShapeDtypeStruct((M, D), F32),
        scratch_shapes=[pltpu.VMEM((tm, D), F32)],
        compiler_params=_cparams(("arbitrary", "arbitrary")),
        name="mlp_ln",
    )(x, w1, w2, g.reshape(1, D), b.reshape(1, D))


def _matmul_kernel(x_ref, w_ref, o_ref):
    o_ref[...] = jnp.dot(x_ref[...].astype(BF16), w_ref[...], preferred_element_type=F32)


def _matmul(x, w):
    M, K = x.shape
    N = w.shape[1]
    return pl.pallas_call(
        _matmul_kernel,
        grid=(1,),
        in_specs=[pl.BlockSpec((M, K), lambda i: (0, 0)), pl.BlockSpec((K, N), lambda i: (0, 0))],
        out_specs=pl.BlockSpec((M, N), lambda i: (0, 0)),
        out_shape=jax.ShapeDtypeStruct((M, N), F32),
        compiler_params=_cparams(("arbitrary",)),
        name="sample_proj",
    )(x, w)


_SEQ_BLOCK = 16


def _dec_index_kernel(topk, n_pages, pt_ref, qi_ref, wi_ref, kiraw_ref, g_ref, b_ref, cidx_ref,
                      selb_ref, newsel_ref, kiln_ref, buf, sem, sc_ref, snew_ref):
    step = pl.program_id(0)
    nseq = pl.num_programs(0) * _SEQ_BLOCK
    L = n_pages * TILE

    def page_copy(seq, slot, p):
        return pltpu.make_async_copy(cidx_ref.at[pt_ref[seq, p]], buf.at[slot, p], sem.at[slot])

    def start(seq, slot):
        for p in range(n_pages):
            page_copy(seq, slot, p).start()

    def wait(seq, slot):
        for p in range(n_pages):
            page_copy(seq, slot, p).wait()

    @pl.when(step == 0)
    def _():
        start(0, 0)

    kiln_ref[...] = _ln(kiraw_ref[...], g_ref[...], b_ref[...])

    def seq_body(i, carry):
        seq = step * _SEQ_BLOCK + i
        slot = i % 2

        @pl.when(seq + 1 < nseq)
        def _():
            start(seq + 1, 1 - slot)

        wait(seq, slot)
        xk = jnp.concatenate([buf[slot, p] for p in range(n_pages)], axis=1).astype(BF16)
        qib = qi_ref[i].astype(BF16)
        s = jnp.dot(qib, xk, preferred_element_type=F32)
        w = wi_ref[i]
        row = jnp.sum(jnp.maximum(s, 0.0) * w, axis=0, keepdims=True) * IDX_SCALE
        sc_ref[pl.ds(i, 1), :] = jnp.where(row == 0.0, 0.0, row)
        kn = kiln_ref[pl.ds(i, 1), :].astype(BF16).astype(F32)
        sn = jnp.sum(qib.astype(F32) * kn, axis=1, keepdims=True)
        snew = jnp.sum(jnp.maximum(sn, 0.0) * w, axis=0, keepdims=True) * IDX_SCALE
        snew_ref[pl.ds(i, 1), :] = jnp.broadcast_to(jnp.where(snew == 0.0, 0.0, snew), (1, TILE))
        return carry

    lax.fori_loop(0, _SEQ_BLOCK, seq_body, 0)
    keys = sc_ref[...]
    knew = snew_ref[:, 0:1]

    def count(pred):
        return (jnp.sum(jnp.where(pred(keys), 1.0, 0.0), axis=1, keepdims=True)
                + jnp.where(pred(knew), 1.0, 0.0))

    t, cnt_t = _kth_largest_search(lambda c: count(lambda k: k >= c), topk, (_SEQ_BLOCK, 1))
    exact_fit = jnp.max(jnp.abs(cnt_t - topk)) == 0.0

    @pl.when(exact_fit)
    def _():
        selb_ref[...] = jnp.where(keys >= t, 1.0, 0.0)
        newsel_ref[...] = jnp.broadcast_to(jnp.where(knew >= t, 1.0, 0.0), (_SEQ_BLOCK, TILE))

    @pl.when(jnp.logical_not(exact_fit))
    def _():
        room = topk - count(lambda k: k > t)
        ii = lax.broadcasted_iota(jnp.int32, (TILE, 2 * TILE), 0)
        jj = lax.broadcasted_iota(jnp.int32, (TILE, 2 * TILE), 1)
        pref = jnp.where(jj >= TILE, 1.0, jnp.where(ii < jj, 1.0, 0.0)).astype(BF16)
        carry = jnp.zeros((_SEQ_BLOCK, TILE), F32)
        for jt in range(n_pages):
            kt = keys[:, jt * TILE:(jt + 1) * TILE]
            eq = kt == t
            res = jnp.dot(jnp.where(eq, 1.0, 0.0).astype(BF16), pref, preferred_element_type=F32)
            take = jnp.where(res[:, :TILE] + carry < room, 1.0, 0.0)
            selb_ref[:, jt * TILE:(jt + 1) * TILE] = jnp.where(kt > t, 1.0, jnp.where(eq, take, 0.0))
            carry = carry + res[:, TILE:]
        newsel = jnp.where(knew > t, 1.0, jnp.where(knew == t, jnp.where(carry[:, :1] < room, 1.0, 0.0), 0.0))
        newsel_ref[...] = jnp.broadcast_to(newsel, (_SEQ_BLOCK, TILE))


def _dec_index(page_table, qi, wi, kiraw, g, b, cidx, topk):
    DB, n_pages = page_table.shape
    L = n_pages * TILE
    sb = _SEQ_BLOCK
    grid_spec = pltpu.PrefetchScalarGridSpec(
        num_scalar_prefetch=1,
        grid=(DB // sb,),
        in_specs=[pl.BlockSpec((sb, IDX_HEADS, IDX_DIM), lambda s, pt: (s, 0, 0)),
                  pl.BlockSpec((sb, IDX_HEADS, 1), lambda s, pt: (s, 0, 0)),
                  pl.BlockSpec((sb, IDX_DIM), lambda s, pt: (s, 0)),
                  pl.BlockSpec((1, IDX_DIM), lambda s, pt: (0, 0)),
                  pl.BlockSpec((1, IDX_DIM), lambda s, pt: (0, 0)),
                  pl.BlockSpec(memory_space=pl.ANY)],
        out_specs=[pl.BlockSpec((sb, L), lambda s, pt: (s, 0)),
                   pl.BlockSpec((sb, TILE), lambda s, pt: (s, 0)),
                   pl.BlockSpec((sb, IDX_DIM), lambda s, pt: (s, 0))],
        scratch_shapes=[pltpu.VMEM((2, n_pages, IDX_DIM, TILE), F32),
                        pltpu.SemaphoreType.DMA((2,)),
                        pltpu.VMEM((sb, L), F32),
                        pltpu.VMEM((sb, TILE), F32)])
    return pl.pallas_call(
        functools.partial(_dec_index_kernel, topk, n_pages),
        grid_spec=grid_spec,
        out_shape=[jax.ShapeDtypeStruct((DB, L), F32),
                   jax.ShapeDtypeStruct((DB, TILE), F32),
                   jax.ShapeDtypeStruct((DB, IDX_DIM), F32)],
        compiler_params=_cparams(("arbitrary",)),
        name="dec_index",
    )(page_table, qi, wi, kiraw, g.reshape(1, -1), b.reshape(1, -1), cidx)


_PAGE_CHUNK = 8


def _decode_attention(n_pages, q_ref, selb_ref, newsel_ref, kn_ref, vn_ref, dec_ref, kbuf, vbuf, o_ref):
    nch = n_pages // _PAGE_CHUNK
    CL = _PAGE_CHUNK * TILE
    qb = q_ref[0]
    far = dec_ref[1]
    near = dec_ref[0]
    pages = lambda buf, c: jnp.concatenate(
        [buf[c * _PAGE_CHUNK + p] for p in range(_PAGE_CHUNK)], axis=1).astype(BF16)
    ss = []
    for c in range(nch):
        s = jnp.dot(qb, pages(kbuf, c), preferred_element_type=F32)
        bias = jnp.concatenate([far] * (_PAGE_CHUNK - 1) + [near if c + 1 == nch else far], axis=1)
        sel = selb_ref[0, :, c * CL:(c + 1) * CL] > 0.5
        ss.append(jnp.where(sel, s + bias, NEG))
    kn = kn_ref[0].astype(BF16).astype(F32)
    sn = jnp.sum(qb.astype(F32) * kn, axis=1, keepdims=True) + dec_ref[2][:, 0:1]
    sn = jnp.where(newsel_ref[0, :, 0:1] > 0.5, sn, NEG)
    m = jnp.maximum(jnp.max(functools.reduce(jnp.maximum, ss), axis=1, keepdims=True), sn)
    pn = jnp.exp(sn - m)
    ps = [jnp.exp(s - m) for s in ss]
    l = jnp.sum(functools.reduce(jnp.add, ps), axis=1, keepdims=True) + pn
    acc = pn * vn_ref[0]
    for c in range(nch):
        acc = acc + lax.dot_general(ps[c].astype(BF16), pages(vbuf, c), _NT, preferred_element_type=F32)
    o_ref[0] = acc / l


def _dsa_mixer_kernel(topk, n_pages, pt_ref,
                      qiT_ref, wiT_ref, ki_ref, qT_ref, kh_ref, vT_ref, toep_ref,
                      qd_ref, selb_ref, newsel_ref, kn_ref, vn_ref, dec_ref, ck_ref, cv_ref,
                      o_ref, od_ref, skey_ref, sel_ref, m_ref, acc_ref, kbuf, vbuf, sem):
    n = pl.program_id(1)
    nq = pl.num_programs(1)
    seq = pl.program_id(0) * nq + n
    nseq = pl.num_programs(0) * nq

    def copies(s_):
        out = []
        for p in range(n_pages):
            phys = pt_ref[s_, p]
            out.append(pltpu.make_async_copy(ck_ref.at[phys], kbuf.at[p], sem.at[0]))
            out.append(pltpu.make_async_copy(cv_ref.at[phys], vbuf.at[p], sem.at[1]))
        return out

    @pl.when(seq == 0)
    def _():
        for cp in copies(0):
            cp.start()

    for cp in copies(seq):
        cp.wait()
    _decode_attention(n_pages, qd_ref, selb_ref, newsel_ref, kn_ref, vn_ref, dec_ref, kbuf, vbuf, od_ref)

    @pl.when(seq + 1 < nseq)
    def _():
        for cp in copies(seq + 1):
            cp.start()

    _dsa_prompt_block(topk, n, qiT_ref, wiT_ref, ki_ref, qT_ref, kh_ref, vT_ref, toep_ref, o_ref,
                      skey_ref, sel_ref, m_ref, acc_ref)


def _dsa_mixer(page_table, qiT, wiT, kib, qT, kh, vT, toep, topk, qbd, selb, newsel, kn, vn, dec, ck, cv):
    B, hq, S = qT.shape
    nq = S // TILE
    DB, n_pages = page_table.shape
    L = n_pages * TILE
    kvd = KV_HEADS_A * HEAD_DIM
    assert nq % _KEY_UNROLL == 0 and n_pages % _PAGE_CHUNK == 0
    assert DB == B * nq
    cw = (N_HEADS // KV_HEADS_A) * TILE
    dseq = lambda b, n, pt: (b * nq + n, 0, 0)
    grid_spec = pltpu.PrefetchScalarGridSpec(
        num_scalar_prefetch=1,
        grid=(B, nq),
        in_specs=[pl.BlockSpec((1, qiT.shape[1], TILE), lambda b, n, pt: (b, 0, n)),
                  pl.BlockSpec((1, IDX_HEADS, TILE), lambda b, n, pt: (b, 0, n)),
                  pl.BlockSpec((1, S, IDX_DIM), lambda b, n, pt: (b, 0, 0)),
                  pl.BlockSpec((1, hq, TILE), lambda b, n, pt: (b, 0, n)),
                  pl.BlockSpec((1, KV_HEADS_A, S, HEAD_DIM), lambda b, n, pt: (b, 0, 0, 0)),
                  pl.BlockSpec((1, kvd, S), lambda b, n, pt: (b, 0, 0)),
                  pl.BlockSpec(toep.shape, lambda b, n, pt: (0, 0, 0)),
                  pl.BlockSpec((1, N_HEADS, kvd), dseq),
                  pl.BlockSpec((1, 1, L), dseq),
                  pl.BlockSpec((1, 1, TILE), dseq),
                  pl.BlockSpec((1, 1, kvd), dseq),
                  pl.BlockSpec((1, 1, kvd), dseq),
                  pl.BlockSpec(dec.shape, lambda b, n, pt: (0, 0, 0)),
                  pl.BlockSpec(memory_space=pl.ANY),
                  pl.BlockSpec(memory_space=pl.ANY)],
        out_specs=[pl.BlockSpec((1, hq, TILE), lambda b, n, pt: (b, 0, n)),
                   pl.BlockSpec((1, N_HEADS, kvd), dseq)],
        scratch_shapes=[pltpu.VMEM((S, TILE), F32), pltpu.VMEM((S, TILE), F32),
                        pltpu.VMEM((KV_HEADS_A, 8, cw), F32),
                        pltpu.VMEM((KV_HEADS_A, HEAD_DIM + _SUM_ROWS, cw), F32),
                        pltpu.VMEM((n_pages, kvd, TILE), F32),
                        pltpu.VMEM((n_pages, kvd, TILE), F32),
                        pltpu.SemaphoreType.DMA((2,))])
    return pl.pallas_call(
        functools.partial(_dsa_mixer_kernel, topk, n_pages),
        grid_spec=grid_spec,
        out_shape=[jax.ShapeDtypeStruct((B, hq, S), BF16),
                   jax.ShapeDtypeStruct((DB, N_HEADS, kvd), F32)],
        compiler_params=_cparams(("arbitrary", "arbitrary"), _VMEM_LIMIT_MIXER),
        name="dsa_mixer",
    )(page_table, qiT, wiT, kib, qT, kh, vT, toep,
      qbd, selb.reshape(DB, 1, L), newsel.reshape(DB, 1, TILE), kn.reshape(DB, 1, kvd), vn.reshape(DB, 1, kvd),
      dec, ck, cv)


def _swa_dec_kernel(q_ref, wk_ref, wv_ref, kn_ref, vn_ref, dec_ref, sink_ref, o_ref):
    lane = lax.broadcasted_iota(jnp.int32, (N_HEADS, TILE), 1)
    bias = dec_ref[0]
    b0 = dec_ref[2][:, 0:1]
    sk = sink_ref[...]
    for i in range(_SEQ_BLOCK):
        qb = q_ref[i]
        s = jnp.dot(qb, wk_ref[i].astype(BF16), preferred_element_type=F32)
        s = jnp.where(lane >= 1, s + bias, NEG)
        kn = kn_ref[i].astype(BF16).astype(F32)
        sn = jnp.sum(qb.astype(F32) * kn, axis=1, keepdims=True) + b0
        m = jnp.maximum(jnp.maximum(jnp.max(s, axis=1, keepdims=True), sn), sk)
        p = jnp.exp(s - m)
        pn = jnp.exp(sn - m)
        l = jnp.sum(p, axis=1, keepdims=True) + pn + jnp.exp(sk - m)
        out = lax.dot_general(p.astype(BF16), wv_ref[i].astype(BF16), _NT, preferred_element_type=F32) + pn * vn_ref[i]
        o_ref[i] = out / l


def _swa_dec(qbd, wk, wv, kn, vn, dec, sink):
    DB = qbd.shape[0]
    kvd = KV_HEADS_B * HEAD_DIM
    sb = _SEQ_BLOCK
    return pl.pallas_call(
        _swa_dec_kernel,
        grid=(DB // sb,),
        in_specs=[pl.BlockSpec((sb, N_HEADS, kvd), lambda s: (s, 0, 0)),
                  pl.BlockSpec((sb, kvd, WINDOW), lambda s: (s, 0, 0)),
                  pl.BlockSpec((sb, kvd, WINDOW), lambda s: (s, 0, 0)),
                  pl.BlockSpec((sb, 1, kvd), lambda s: (s, 0, 0)),
                  pl.BlockSpec((sb, 1, kvd), lambda s: (s, 0, 0)),
                  pl.BlockSpec(dec.shape, lambda s: (0, 0, 0)),
                  pl.BlockSpec((N_HEADS, 1), lambda s: (0, 0))],
        out_specs=pl.BlockSpec((sb, N_HEADS, kvd), lambda s: (s, 0, 0)),
        out_shape=jax.ShapeDtypeStruct((DB, N_HEADS, kvd), F32),
        compiler_params=_cparams(("arbitrary",)),
        name="swa_dec",
    )(qbd, wk, wv, kn.reshape(DB, 1, kvd), vn.reshape(DB, 1, kvd), dec, sink.reshape(N_HEADS, 1))


def _block_diag_q(q, nkv):
    DB = q.shape[0]
    G = N_HEADS // nkv
    qh = (q * ATTN_SCALE).reshape(DB, N_HEADS, 1, HEAD_DIM)
    own = (np.arange(N_HEADS)[:, None] // G == np.arange(nkv)[None, :]).astype(np.float32)
    return (qh * own[None, :, :, None]).reshape(DB, N_HEADS, nkv * HEAD_DIM).astype(BF16)


def _own_group(o, nkv):
    DB = o.shape[0]
    G = N_HEADS // nkv
    o5 = o.reshape(DB, nkv, G, nkv, HEAD_DIM)
    return jnp.stack([o5[:, g, :, g, :] for g in range(nkv)], axis=1).reshape(DB, N_HEADS * HEAD_DIM)


def _pad_cols(w, mult=TILE):
    pad = (-w.shape[1]) % mult
    return jnp.pad(w, ((0, 0), (0, pad))) if pad else w


def _slots_minor(a):
    n, t, h, d = a.shape
    return a.transpose(0, 2, 3, 1).reshape(n, h * d, t)


def _slots_major(a, heads):
    n, hd, t = a.shape
    return a.reshape(n, heads, hd // heads, t).transpose(0, 3, 1, 2)


def kernel(x_prompt, x_sample, cache_k, cache_v, cache_idx_k, state_win_k, state_win_v, page_table, rel_bias_table, a_w_in, a_w_out, a_idx_ln_g, a_idx_ln_b, b_w_in, b_w_out, b_sink, ln1_g, ln1_b, ln2_g, ln2_b, mlp_w1, mlp_w2):
    B, S, D = x_prompt.shape
    DB, DS, _ = x_sample.shape
    depth = ln1_g.shape[0]
    assert depth == 2 and DS == 1 and S % TILE == 0 and DB % _SEQ_BLOCK == 0
    assert state_win_k.shape[2] == WINDOW and cache_k.shape[2] == TILE
    alpha = (2 * depth) ** 0.25
    n_pages = page_table.shape[1]
    past = n_pages * TILE
    hq = N_HEADS * HEAD_DIM
    kva = KV_HEADS_A * HEAD_DIM
    kvb = KV_HEADS_B * HEAD_DIM
    hi = IDX_HEADS * IDX_DIM

    toep, toep2, dec = _bias_tables(rel_bias_table)

    wa = a_w_in[0]
    c0, c1, c2, c3, c4 = hq, hq + kva, hq + 2 * kva, hq + 2 * kva + hi, hq + 2 * kva + hi + IDX_DIM
    wq, wk, wv, wqi, wki, wwi = wa[:, :c0], wa[:, c0:c1], wa[:, c1:c2], wa[:, c2:c3], wa[:, c3:c4], wa[:, c4:]
    a_wo = a_w_out[0].astype(BF16)
    w1 = [mlp_w1[i] for i in range(depth)]
    w2 = [mlp_w2[i] for i in range(depth)]

    qT, kTp, vTp, kh, vT, qiT, kiTp, kib, wiT = _prompt_proj(
        x_prompt, wq, wk, wv, KV_HEADS_A, idx=(wqi, wki, wwi, a_idx_ln_g[0], a_idx_ln_b[0]),
        q_scale=ATTN_SCALE * LOG2E)
    xs = x_sample.reshape(DB, D)
    proj = _matmul(xs, _pad_cols(wa).astype(BF16))
    q_s, k_s, v_s = proj[:, :c0], proj[:, c0:c1], proj[:, c1:c2]
    qi_s, kiraw_s, wi_s = proj[:, c2:c3], proj[:, c3:c4], proj[:, c4:c4 + IDX_HEADS]
    selb, newsel, ki_s = _dec_index(
        page_table, qi_s.reshape(DB, IDX_HEADS, IDX_DIM), wi_s.reshape(DB, IDX_HEADS, 1), kiraw_s,
        a_idx_ln_g[0], a_idx_ln_b[0], cache_idx_k[0].transpose(0, 2, 1), min(TOPK_MAX, (past + DS) // 4))
    oT, o_s = _dsa_mixer(page_table, qiT, wiT, kib, qT, kh, vT, toep2, min(TOPK_MAX, S // 4),
                         _block_diag_q(q_s, KV_HEADS_A), selb, newsel, k_s, v_s, dec,
                         _slots_minor(cache_k[0]), _slots_minor(cache_v[0]))

    attn = oT.transpose(0, 2, 1).reshape(B * S, hq)
    y_p = _outproj_ln(x_prompt.reshape(B * S, D), attn, a_wo, ln1_g[0], ln1_b[0], alpha)
    y_p = _mlp_ln(y_p, w1[0], w2[0], ln2_g[0], ln2_b[0], alpha)

    y_s = _outproj_ln(xs, _own_group(o_s, KV_HEADS_A).astype(BF16), a_wo, ln1_g[0], ln1_b[0], alpha)
    y_s = _mlp_ln(y_s, w1[0], w2[0], ln2_g[0], ln2_b[0], alpha)

    wb = b_w_in[0]
    bq, bk, bv = wb[:, :hq], wb[:, hq:hq + kvb], wb[:, hq + kvb:]
    b_wo = b_w_out[0].astype(BF16)

    qT1, k1Tp, v1Tp, kh1, vT1 = _prompt_proj(y_p.reshape(B, S, D), bq, bk, bv, KV_HEADS_B,
                                             q_scale=ATTN_SCALE * LOG2E)
    oT1 = _swa_prompt(qT1, kh1, vT1, toep2, b_sink[0])
    attn1 = oT1.transpose(0, 2, 1).reshape(B * S, hq)
    y_p = _outproj_ln(y_p, attn1, b_wo, ln1_g[1], ln1_b[1], alpha)
    y_p = _mlp_ln(y_p, w1[1], w2[1], ln2_g[1], ln2_b[1], alpha)

    proj1 = _matmul(y_s, wb.astype(BF16))
    q1_s, k1_s, v1_s = proj1[:, :hq], proj1[:, hq:hq + kvb], proj1[:, hq + kvb:]
    win_k = _slots_minor(state_win_k[0])
    win_v = _slots_minor(state_win_v[0])
    o1_s = _swa_dec(_block_diag_q(q1_s, KV_HEADS_B), win_k, win_v, k1_s, v1_s, dec, b_sink[0])
    y_s = _outproj_ln(y_s, _own_group(o1_s, KV_HEADS_B).astype(BF16), b_wo, ln1_g[1], ln1_b[1], alpha)
    y_s = _mlp_ln(y_s, w1[1], w2[1], ln2_g[1], ln2_b[1], alpha)

    npg = S // TILE
    assert min(WINDOW, S) == TILE
    pages = lambda a, heads: _slots_major(a.reshape(B * npg, -1, TILE), heads).reshape(1, B, npg, TILE, heads, HEAD_DIM)
    new_wk_s = jnp.concatenate([win_k[:, :, 1:], k1_s[:, :, None]], axis=2)
    new_wv_s = jnp.concatenate([win_v[:, :, 1:], v1_s[:, :, None]], axis=2)
    return (y_p.reshape(B, S, D),
            y_s.reshape(DB, DS, D),
            pages(kTp, KV_HEADS_A),
            pages(vTp, KV_HEADS_A),
            kiTp.transpose(0, 1, 3, 2).reshape(1, B, npg, TILE, IDX_DIM),
            k_s.reshape(1, DB, DS, KV_HEADS_A, HEAD_DIM),
            v_s.reshape(1, DB, DS, KV_HEADS_A, HEAD_DIM),
            ki_s.reshape(1, DB, DS, IDX_DIM),
            _slots_major(k1Tp[:, npg - 1], KV_HEADS_B)[None],
            _slots_major(v1Tp[:, npg - 1], KV_HEADS_B)[None],
            _slots_major(new_wk_s, KV_HEADS_B)[None],
            _slots_major(new_wv_s, KV_HEADS_B)[None])
```

```python
import functools
import math

import numpy as np
import jax
import jax.numpy as jnp
from jax import lax
from jax.experimental import pallas as pl
from jax.experimental.pallas import tpu as pltpu

F32 = jnp.float32
BF16 = jnp.bfloat16

N_HEADS = 16
HEAD_DIM = 64
KV_HEADS_A = 4
KV_HEADS_B = 2
IDX_HEADS = 8
IDX_DIM = 64
TOPK_MAX = 256
WINDOW = 128
TILE = 128
N_BUCKETS = 32
MAX_DISTANCE = 128
LN_EPS = 1e-5
NEG = -1e30
INT_MIN = -(2 ** 31)
ATTN_SCALE = HEAD_DIM ** -0.5
LOG2E = math.log2(math.e)
IDX_SCALE = IDX_HEADS ** -0.5 * IDX_DIM ** -0.5
FAR_BUCKET = N_BUCKETS - 1
FAR_DISTANCE = 113

_NT = (((1,), (1,)), ((), ()))
_VMEM_LIMIT = 48 * 1024 * 1024
_VMEM_LIMIT_MIXER = 58 * 1024 * 1024


def _cparams(sem, vmem_limit=_VMEM_LIMIT):
    return pltpu.CompilerParams(dimension_semantics=sem, vmem_limit_bytes=vmem_limit)


def _bucket_np(dist):
    n = np.maximum(dist, 0)
    max_exact = N_BUCKETS // 2
    nf = np.maximum(n, max_exact).astype(np.float64)
    val = np.log(nf / max_exact) / math.log(MAX_DISTANCE / max_exact) * (N_BUCKETS - max_exact)
    frac = val - np.floor(val)
    interior = (n > max_exact) & (n < MAX_DISTANCE)
    assert not np.any(interior & ((frac < 1e-6) | (frac > 1 - 1e-6))), "bucket boundary too close to an integer"
    large = np.minimum(max_exact + val.astype(np.int32), N_BUCKETS - 1)
    out = np.where(n < max_exact, n, large).astype(np.int32)
    assert np.all(out[n >= FAR_DISTANCE] == FAR_BUCKET)
    return out


def _key_to_float(k):
    return lax.bitcast_convert_type(k ^ (lax.shift_right_arithmetic(k, 31) & 0x7FFFFFFF), F32)


def _kth_largest_search(count_ge, topk, shape):
    def bit_body(b, carry):
        t, cnt_t = carry
        cand = t ^ lax.shift_left(jnp.int32(1), 31 - b)
        cnt = count_ge(_key_to_float(cand))
        ok = cnt >= topk
        return jnp.where(ok, cand, t), jnp.where(ok, cnt, cnt_t)

    init = (jnp.full(shape, INT_MIN, jnp.int32), jnp.full(shape, 3e38, F32))
    t, cnt_t = lax.fori_loop(0, 32, bit_body, init)
    return _key_to_float(t), cnt_t


def _ln(y, g, b):
    mu = jnp.mean(y, axis=-1, keepdims=True)
    yc = y - mu
    var = jnp.mean(yc * yc, axis=-1, keepdims=True)
    return yc * lax.rsqrt(var + LN_EPS) * g + b


def _bias_kernel(table_ref, bk_ref, bkd_ref, toep_ref, toep2_ref, dec_ref):
    bk = bk_ref[...]
    bkd = bkd_ref[...]
    dec_rows = []
    for h in range(N_HEADS):
        acc = jnp.zeros(bk.shape, F32)
        accd = jnp.zeros(bkd.shape, F32)
        for b in range(N_BUCKETS):
            val = table_ref[b, h]
            acc = jnp.where(bk == b, val, acc)
            accd = jnp.where(bkd == b, val, accd)
        toep_ref[h] = acc
        toep2_ref[h] = acc * LOG2E
        dec_rows.append(accd)
    for kind in range(3):
        dec_ref[kind] = jnp.concatenate([d[kind:kind + 1, :] for d in dec_rows], axis=0)


def _bias_tables(table):
    i = np.arange(TILE)[None, :]
    j = np.arange(2 * TILE)[:, None]
    bk = np.concatenate([np.full((TILE, TILE), FAR_BUCKET, np.int32), _bucket_np(i + TILE - j)], axis=0)
    bkd = np.zeros((8, TILE), np.int32)
    bkd[0] = _bucket_np(TILE - np.arange(TILE))
    bkd[1] = FAR_BUCKET
    bkd[2] = 0
    return pl.pallas_call(
        _bias_kernel,
        grid=(1,),
        in_specs=[pl.BlockSpec(memory_space=pltpu.SMEM),
                  pl.BlockSpec((3 * TILE, TILE), lambda h: (0, 0)),
                  pl.BlockSpec((8, TILE), lambda h: (0, 0))],
        out_specs=[pl.BlockSpec((N_HEADS, 3 * TILE, TILE), lambda h: (0, 0, 0)),
                   pl.BlockSpec((N_HEADS, 3 * TILE, TILE), lambda h: (0, 0, 0)),
                   pl.BlockSpec((3, N_HEADS, TILE), lambda h: (0, 0, 0))],
        out_shape=[jax.ShapeDtypeStruct((N_HEADS, 3 * TILE, TILE), F32),
                   jax.ShapeDtypeStruct((N_HEADS, 3 * TILE, TILE), F32),
                   jax.ShapeDtypeStruct((3, N_HEADS, TILE), F32)],
        compiler_params=_cparams(("arbitrary",)),
        name="bias_tables",
    )(table, jnp.asarray(bk), jnp.asarray(bkd))


def _normalize_cols(y):
    mu = jnp.mean(y, axis=0, keepdims=True)
    yc = y - mu
    var = jnp.mean(yc * yc, axis=0, keepdims=True)
    return yc * lax.rsqrt(var + LN_EPS)


def _proj_kernel(nkv, has_idx, *refs):
    if has_idx:
        (x_ref, wqT_ref, wkT_ref, wvT_ref, wkh_ref, wqiT_ref, wkiT_ref, wki_ref, wwiT_ref, g_ref, b_ref,
         gc_ref, bc_ref, qT_ref, kTp_ref, vTp_ref, kh_ref, vT_ref, qiT_ref, kiTp_ref, kib_ref, wiT_ref) = refs
    else:
        (x_ref, wqT_ref, wkT_ref, wvT_ref, wkh_ref, qT_ref, kTp_ref, vTp_ref, kh_ref, vT_ref) = refs
    xb = x_ref[0].astype(BF16)
    npg = xb.shape[0] // TILE
    qT_ref[0] = lax.dot_general(wqT_ref[...], xb, _NT, preferred_element_type=F32).astype(BF16)
    kT = lax.dot_general(wkT_ref[...], xb, _NT, preferred_element_type=F32)
    vT = lax.dot_general(wvT_ref[...], xb, _NT, preferred_element_type=F32)
    vT_ref[0] = vT.astype(BF16)
    for j in range(npg):
        kTp_ref[0, j] = kT[:, j * TILE:(j + 1) * TILE]
        vTp_ref[0, j] = vT[:, j * TILE:(j + 1) * TILE]
    for g in range(nkv):
        kh_ref[0, g] = jnp.dot(xb, wkh_ref[g], preferred_element_type=F32).astype(BF16)
    if has_idx:
        qiT_ref[0] = lax.dot_general(wqiT_ref[...], xb, _NT, preferred_element_type=F32).astype(BF16)
        kiT = _normalize_cols(lax.dot_general(wkiT_ref[...], xb, _NT, preferred_element_type=F32))
        for j in range(npg):
            kiTp_ref[0, j] = kiT[:, j * TILE:(j + 1) * TILE] * gc_ref[...] + bc_ref[...]
        ki = _ln(jnp.dot(xb, wki_ref[...], preferred_element_type=F32), g_ref[...], b_ref[...])
        kib_ref[0] = ki.astype(BF16)
        wiT_ref[0] = lax.dot_general(wwiT_ref[...], xb, _NT, preferred_element_type=F32)


def _prompt_proj(x, wq, wk, wv, nkv, idx=None, q_scale=ATTN_SCALE, tm=512):
    B, S, D = x.shape
    tm = min(tm, S)
    kvd = nkv * HEAD_DIM
    hq = N_HEADS * HEAD_DIM
    wqT = (wq * q_scale).T.astype(BF16)
    wkh = wk.reshape(D, nkv, HEAD_DIM).transpose(1, 0, 2).astype(BF16)
    full2 = lambda a: pl.BlockSpec(a.shape, lambda b, m: (0,) * a.ndim)
    ins = [x, wqT, wk.T.astype(BF16), wv.T.astype(BF16), wkh]
    in_specs = [pl.BlockSpec((1, tm, D), lambda b, m: (b, m, 0))] + [full2(a) for a in ins[1:]]
    out_shape = [jax.ShapeDtypeStruct((B, hq, S), BF16),
                 jax.ShapeDtypeStruct((B, S // TILE, kvd, TILE), F32),
                 jax.ShapeDtypeStruct((B, S // TILE, kvd, TILE), F32),
                 jax.ShapeDtypeStruct((B, nkv, S, HEAD_DIM), BF16),
                 jax.ShapeDtypeStruct((B, kvd, S), BF16)]
    out_specs = [pl.BlockSpec((1, hq, tm), lambda b, m: (b, 0, m)),
                 pl.BlockSpec((1, tm // TILE, kvd, TILE), lambda b, m: (b, m, 0, 0)),
                 pl.BlockSpec((1, tm // TILE, kvd, TILE), lambda b, m: (b, m, 0, 0)),
                 pl.BlockSpec((1, nkv, tm, HEAD_DIM), lambda b, m: (b, 0, m, 0)),
                 pl.BlockSpec((1, kvd, tm), lambda b, m: (b, 0, m))]
    if idx is not None:
        wqi, wki, wwi, g, bb = idx
        extra = [wqi.T.astype(BF16), wki.T.astype(BF16), wki.astype(BF16), wwi.T.astype(BF16),
                 g.reshape(1, -1), bb.reshape(1, -1),
                 jnp.broadcast_to(g.reshape(-1, 1), (IDX_DIM, TILE)), jnp.broadcast_to(bb.reshape(-1, 1), (IDX_DIM, TILE))]
        ins += extra
        in_specs += [full2(a) for a in extra]
        hi = IDX_HEADS * IDX_DIM
        out_shape += [jax.ShapeDtypeStruct((B, hi, S), BF16),
                      jax.ShapeDtypeStruct((B, S // TILE, IDX_DIM, TILE), F32),
                      jax.ShapeDtypeStruct((B, S, IDX_DIM), BF16),
                      jax.ShapeDtypeStruct((B, IDX_HEADS, S), F32)]
        out_specs += [pl.BlockSpec((1, hi, tm), lambda b, m: (b, 0, m)),
                      pl.BlockSpec((1, tm // TILE, IDX_DIM, TILE), lambda b, m: (b, m, 0, 0)),
                      pl.BlockSpec((1, tm, IDX_DIM), lambda b, m: (b, m, 0)),
                      pl.BlockSpec((1, IDX_HEADS, tm), lambda b, m: (b, 0, m))]
    return pl.pallas_call(
        functools.partial(_proj_kernel, nkv, idx is not None),
        grid=(B, S // tm),
        in_specs=in_specs,
        out_specs=out_specs,
        out_shape=out_shape,
        compiler_params=_cparams(("arbitrary", "arbitrary")),
        name="prompt_proj_idx" if idx is not None else "prompt_proj",
    )(*ins)


_KEY_UNROLL = 4
_SUM_ROWS = 16
_ATT_TILES = 4


def _dsa_prompt_block(topk, n, qiT_ref, wiT_ref, ki_ref, qT_ref, kh_ref, vT_ref, toep_ref, o_ref,
                      skey_ref, sel_ref, m_ref, acc_ref):
    ST = _KEY_UNROLL * TILE
    ns = n // _KEY_UNROLL + 1
    G = N_HEADS // KV_HEADS_A
    rows = lax.broadcasted_iota(jnp.int32, (ST, TILE), 0)
    lanes = lax.broadcasted_iota(jnp.int32, (ST, TILE), 1)
    qpos = n * TILE + lanes

    qi = qiT_ref[0]
    wi = wiT_ref[0]

    def score_body(u, carry):
        off = pl.multiple_of(u * ST, ST)
        kij = ki_ref[0, pl.ds(off, ST), :]
        acc = jnp.zeros((ST, TILE), F32)
        for h in range(0, IDX_HEADS, 2):
            qpair = jnp.concatenate([qi[IDX_DIM * h:IDX_DIM * (h + 1), :],
                                     qi[IDX_DIM * (h + 1):IDX_DIM * (h + 2), :]], axis=1)
            s = jnp.dot(kij, qpair, preferred_element_type=F32)
            acc = acc + jnp.maximum(s[:, :TILE], 0.0) * wi[h:h + 1, :]
            acc = acc + jnp.maximum(s[:, TILE:], 0.0) * wi[h + 1:h + 2, :]
        sc = acc * IDX_SCALE
        sc = jnp.where(off + rows <= qpos, sc, NEG)
        sc = jnp.where(sc == 0.0, 0.0, sc)
        skey_ref[pl.ds(off, ST), :] = sc
        return carry

    lax.fori_loop(0, ns, score_body, 0)

    def count(pred):
        def body(u, c):
            off = pl.multiple_of(u * ST, ST)
            x = jnp.where(pred(skey_ref[pl.ds(off, ST), :]), 1.0, 0.0).reshape(ST // 8, 8, TILE)
            while x.shape[0] > 1:
                half = x.shape[0] // 2
                x = x[:half] + x[half:]
            return c + x[0]
        c8 = lax.fori_loop(0, ns, body, jnp.zeros((8, TILE), F32))
        return jnp.sum(c8, axis=0, keepdims=True)

    t, cnt_t = _kth_largest_search(lambda c: count(lambda kt: kt >= c), topk, (1, TILE))
    exact_fit = jnp.max(jnp.abs(cnt_t - topk)) == 0.0

    @pl.when(exact_fit)
    def _():
        def sel_body(u, carry):
            off = pl.multiple_of(u * ST, ST)
            sel = jnp.where(skey_ref[pl.ds(off, ST), :] >= t, 1.0, 0.0)
            sel_ref[pl.ds(off, ST), :] = jnp.where(off + rows <= qpos, sel, 0.0)
            return carry

        lax.fori_loop(0, ns, sel_body, 0)

    @pl.when(jnp.logical_not(exact_fit))
    def _():
        room = topk - count(lambda kt: kt > t)
        ii = lax.broadcasted_iota(jnp.int32, (TILE, TILE), 0)
        jj = lax.broadcasted_iota(jnp.int32, (TILE, TILE), 1)
        lower = jnp.where(jj < ii, 1.0, 0.0).astype(BF16)
        ones8 = jnp.ones((8, TILE), BF16)

        def sel_body(u, carry):
            for k in range(_KEY_UNROLL):
                off = pl.multiple_of(u * ST + k * TILE, TILE)
                kt = skey_ref[pl.ds(off, TILE), :]
                eq = kt == t
                eqb = jnp.where(eq, 1.0, 0.0).astype(BF16)
                before = jnp.dot(lower, eqb, preferred_element_type=F32) + carry
                tot = jnp.dot(ones8, eqb, preferred_element_type=F32)[0:1]
                sel = jnp.where(kt > t, 1.0, jnp.where(eq, jnp.where(before < room, 1.0, 0.0), 0.0))
                sel_ref[pl.ds(off, TILE), :] = jnp.where(off + ii <= n * TILE + jj, sel, 0.0)
                carry = carry + tot
            return carry

        lax.fori_loop(0, ns, sel_body, jnp.zeros((1, TILE), F32))

    q = qT_ref[0]
    qgs = [jnp.concatenate([q[(g * G + r) * HEAD_DIM:(g * G + r + 1) * HEAD_DIM, :] for r in range(G)], axis=1)
           for g in range(KV_HEADS_A)]
    m_ref[...] = jnp.full(m_ref.shape, NEG, F32)
    acc_ref[...] = jnp.zeros(acc_ref.shape, F32)

    AT = _ATT_TILES * TILE
    ones_rows = jnp.ones((_SUM_ROWS, AT), BF16)

    def att_body(u, carry):
        off = pl.multiple_of(u * AT, AT)
        tiles = [u * _ATT_TILES + k for k in range(_ATT_TILES)]
        roffs = [pl.multiple_of(jnp.where(j == n, 2 * TILE, jnp.where(j == n - 1, TILE, 0)), TILE) for j in tiles]
        selv = sel_ref[pl.ds(off, AT), :] > 0.5
        ss = [jnp.dot(kh_ref[0, g, pl.ds(off, AT), :], qgs[g], preferred_element_type=F32)
              for g in range(KV_HEADS_A)]
        for g in range(KV_HEADS_A):
            s = ss[g]
            s = jnp.concatenate(
                [jnp.where(selv,
                           s[:, r * TILE:(r + 1) * TILE]
                           + jnp.concatenate([toep_ref[g * G + r, pl.ds(ro, TILE), :] for ro in roffs], axis=0),
                           NEG)
                 for r in range(G)], axis=1)
            m = m_ref[g]
            m_new = jnp.maximum(m, jnp.max(s, axis=0, keepdims=True))
            alpha = jnp.exp2(m - m_new)
            p = jnp.exp2(s - m_new[0:1])
            m_ref[g] = m_new
            vt = jnp.concatenate([vT_ref[0, g * HEAD_DIM:(g + 1) * HEAD_DIM, pl.ds(off, AT)], ones_rows], axis=0)
            acc_ref[g] = acc_ref[g] * alpha[0:1] + jnp.dot(vt, p.astype(BF16), preferred_element_type=F32)
        return carry

    lax.fori_loop(0, n // _ATT_TILES + 1, att_body, 0)
    for g in range(KV_HEADS_A):
        out = acc_ref[g, 0:HEAD_DIM, :] / acc_ref[g, HEAD_DIM:HEAD_DIM + 1, :]
        for r in range(G):
            h = g * G + r
            o_ref[0, h * HEAD_DIM:(h + 1) * HEAD_DIM, :] = out[:, r * TILE:(r + 1) * TILE].astype(o_ref.dtype)


def _swa_prompt_kernel(sink_ref, qT_ref, khp_ref, khc_ref, vTp_ref, vTc_ref, toep_ref, o_ref):
    n = pl.program_id(1)
    G = N_HEADS // KV_HEADS_B
    rows = lax.broadcasted_iota(jnp.int32, (2 * TILE, TILE), 0)
    lanes = lax.broadcasted_iota(jnp.int32, (2 * TILE, TILE), 1)
    dj = rows - lanes
    inwin = jnp.where(dj >= TILE - WINDOW + 1, jnp.where(dj <= TILE, 1.0, 0.0), 0.0)
    inwin = jnp.where(rows >= TILE, inwin, jnp.where(n > 0, inwin, 0.0))
    maskb = jnp.concatenate([inwin] * G, axis=1) > 0.5
    q = qT_ref[0]
    ones_rows = jnp.ones((_SUM_ROWS, 2 * TILE), BF16)
    for g in range(KV_HEADS_B):
        heads = [g * G + r for r in range(G)]
        qg = jnp.concatenate([q[h * HEAD_DIM:(h + 1) * HEAD_DIM, :] for h in heads], axis=1)
        kk = jnp.concatenate([khp_ref[0, g], khc_ref[0, g]], axis=0)
        s = jnp.dot(kk, qg, preferred_element_type=F32)
        bias = jnp.concatenate([toep_ref[h, TILE:3 * TILE, :] for h in heads], axis=1)
        s = jnp.where(maskb, s + bias, NEG)
        sink = jnp.concatenate([jnp.full((1, TILE), sink_ref[h] * LOG2E, F32) for h in heads], axis=1)
        m = jnp.maximum(jnp.max(s, axis=0, keepdims=True), sink)
        p = jnp.exp2(s - m)
        vv = jnp.concatenate([vTp_ref[0, g * HEAD_DIM:(g + 1) * HEAD_DIM, :],
                              vTc_ref[0, g * HEAD_DIM:(g + 1) * HEAD_DIM, :]], axis=1)
        pv = jnp.dot(jnp.concatenate([vv, ones_rows], axis=0), p.astype(BF16), preferred_element_type=F32)
        out = pv[0:HEAD_DIM, :] / (pv[HEAD_DIM:HEAD_DIM + 1, :] + jnp.exp2(sink - m))
        for r, h in enumerate(heads):
            o_ref[0, h * HEAD_DIM:(h + 1) * HEAD_DIM, :] = out[:, r * TILE:(r + 1) * TILE].astype(o_ref.dtype)


def _swa_prompt(qT, kh, vT, toep, sink):
    B, hq, S = qT.shape
    nb = S // TILE
    prev = lambda n: jnp.maximum(n - 1, 0)
    return pl.pallas_call(
        _swa_prompt_kernel,
        grid=(B, nb),
        in_specs=[pl.BlockSpec(memory_space=pltpu.SMEM),
                  pl.BlockSpec((1, hq, TILE), lambda b, n: (b, 0, n)),
                  pl.BlockSpec((1, KV_HEADS_B, TILE, HEAD_DIM), lambda b, n: (b, 0, prev(n), 0)),
                  pl.BlockSpec((1, KV_HEADS_B, TILE, HEAD_DIM), lambda b, n: (b, 0, n, 0)),
                  pl.BlockSpec((1, KV_HEADS_B * HEAD_DIM, TILE), lambda b, n: (b, 0, prev(n))),
                  pl.BlockSpec((1, KV_HEADS_B * HEAD_DIM, TILE), lambda b, n: (b, 0, n)),
                  pl.BlockSpec(toep.shape, lambda b, n: (0, 0, 0))],
        out_specs=pl.BlockSpec((1, hq, TILE), lambda b, n: (b, 0, n)),
        out_shape=jax.ShapeDtypeStruct((B, hq, S), BF16),
        compiler_params=_cparams(("arbitrary", "arbitrary")),
        name="swa_prompt",
    )(sink, qT, kh, kh, vT, vT, toep)


def _residual_kernel(alpha, x_ref, a_ref, wo_ref, g1_ref, b1_ref, w1_ref, w2_ref, g2_ref, b2_ref, o_ref,
                     y_ref, acc_ref):
    f = pl.program_id(1)

    @pl.when(f == 0)
    def _():
        y = alpha * x_ref[...] + jnp.dot(a_ref[...], wo_ref[...], preferred_element_type=F32)
        y_ref[...] = _ln(y, g1_ref[...], b1_ref[...])
        acc_ref[...] = jnp.zeros_like(acc_ref)

    h = jnp.maximum(jnp.dot(y_ref[...].astype(BF16), w1_ref[0].astype(BF16), preferred_element_type=F32), 0.0)
    acc_ref[...] += jnp.dot((h * h).astype(BF16), w2_ref[0].astype(BF16), preferred_element_type=F32)

    @pl.when(f == pl.num_programs(1) - 1)
    def _():
        o_ref[...] = _ln(alpha * y_ref[...] + acc_ref[...], g2_ref[...], b2_ref[...])


def _residual_blocks(x, a, wo, g1, b1, w1, w2, layer, g2, b2, alpha, tm=1024, tf=512):
    M, D = x.shape
    FF = w1.shape[2]
    tm = min(tm, M)
    row = lambda v: v.reshape(1, D)
    vec = pl.BlockSpec((1, D), lambda m, f: (0, 0))
    return pl.pallas_call(
        functools.partial(_residual_kernel, alpha),
        grid=(M // tm, FF // tf),
        in_specs=[pl.BlockSpec((tm, D), lambda m, f: (m, 0)),
                  pl.BlockSpec((tm, a.shape[1]), lambda m, f: (m, 0)),
                  pl.BlockSpec(wo.shape, lambda m, f: (0, 0)),
                  vec, vec,
                  pl.BlockSpec((1, D, tf), lambda m, f: (layer, 0, f)),
                  pl.BlockSpec((1, tf, D), lambda m, f: (layer, f, 0)),
                  vec, vec],
        out_specs=pl.BlockSpec((tm, D), lambda m, f: (m, 0)),
        out_shape=jax.ShapeDtypeStruct((M, D), F32),
        scratch_shapes=[pltpu.VMEM((tm, D), F32), pltpu.VMEM((tm, D), F32)],
        compiler_params=_cparams(("arbitrary", "arbitrary")),
        name="residual_blocks",
    )(x, a, wo, row(g1), row(b1), w1, w2, row(g2), row(b2))


def _matmul_kernel(x_ref, w_ref, o_ref):
    o_ref[...] = jnp.dot(x_ref[...].astype(BF16), w_ref[...], preferred_element_type=F32)


def _matmul(x, w):
    M, K = x.shape
    N = w.shape[1]
    return pl.pallas_call(
        _matmul_kernel,
        grid=(1,),
        in_specs=[pl.BlockSpec((M, K), lambda i: (0, 0)), pl.BlockSpec((K, N), lambda i: (0, 0))],
        out_specs=pl.BlockSpec((M, N), lambda i: (0, 0)),
        out_shape=jax.ShapeDtypeStruct((M, N), F32),
        compiler_params=_cparams(("arbitrary",)),
        name="sample_proj",
    )(x, w)


_SEQ_BLOCK = 16


def _dec_index_kernel(topk, n_pages, pt_ref, qi_ref, wi_ref, kiraw_ref, g_ref, b_ref, cidx_ref,
                      selb_ref, newsel_ref, kiln_ref, buf, sem, sc_ref, snew_ref):
    step = pl.program_id(0)
    nseq = pl.num_programs(0) * _SEQ_BLOCK
    L = n_pages * TILE

    def page_copy(seq, slot, p):
        return pltpu.make_async_copy(cidx_ref.at[pt_ref[seq, p]], buf.at[slot, p], sem.at[slot])

    def start(seq, slot):
        for p in range(n_pages):
            page_copy(seq, slot, p).start()

    def wait(seq, slot):
        for p in range(n_pages):
            page_copy(seq, slot, p).wait()

    @pl.when(step == 0)
    def _():
        start(0, 0)

    kiln_ref[...] = _ln(kiraw_ref[...], g_ref[...], b_ref[...])

    def seq_body(i, carry):
        seq = step * _SEQ_BLOCK + i
        slot = i % 2

        @pl.when(seq + 1 < nseq)
        def _():
            start(seq + 1, 1 - slot)

        wait(seq, slot)
        xk = jnp.concatenate([buf[slot, p] for p in range(n_pages)], axis=1).astype(BF16)
        qib = qi_ref[i].astype(BF16)
        s = jnp.dot(qib, xk, preferred_element_type=F32)
        w = wi_ref[i]
        row = jnp.sum(jnp.maximum(s, 0.0) * w, axis=0, keepdims=True) * IDX_SCALE
        sc_ref[pl.ds(i, 1), :] = jnp.where(row == 0.0, 0.0, row)
        kn = kiln_ref[pl.ds(i, 1), :].astype(BF16).astype(F32)
        sn = jnp.sum(qib.astype(F32) * kn, axis=1, keepdims=True)
        snew = jnp.sum(jnp.maximum(sn, 0.0) * w, axis=0, keepdims=True) * IDX_SCALE
        snew_ref[pl.ds(i, 1), :] = jnp.broadcast_to(jnp.where(snew == 0.0, 0.0, snew), (1, TILE))
        return carry

    lax.fori_loop(0, _SEQ_BLOCK, seq_body, 0)
    keys = sc_ref[...]
    knew = snew_ref[:, 0:1]

    def count(pred):
        return (jnp.sum(jnp.where(pred(keys), 1.0, 0.0), axis=1, keepdims=True)
                + jnp.where(pred(knew), 1.0, 0.0))

    t, cnt_t = _kth_largest_search(lambda c: count(lambda k: k >= c), topk, (_SEQ_BLOCK, 1))
    exact_fit = jnp.max(jnp.abs(cnt_t - topk)) == 0.0

    @pl.when(exact_fit)
    def _():
        selb_ref[...] = jnp.where(keys >= t, 1.0, 0.0)
        newsel_ref[...] = jnp.broadcast_to(jnp.where(knew >= t, 1.0, 0.0), (_SEQ_BLOCK, TILE))

    @pl.when(jnp.logical_not(exact_fit))
    def _():
        room = topk - count(lambda k: k > t)
        ii = lax.broadcasted_iota(jnp.int32, (TILE, 2 * TILE), 0)
        jj = lax.broadcasted_iota(jnp.int32, (TILE, 2 * TILE), 1)
        pref = jnp.where(jj >= TILE, 1.0, jnp.where(ii < jj, 1.0, 0.0)).astype(BF16)
        carry = jnp.zeros((_SEQ_BLOCK, TILE), F32)
        for jt in range(n_pages):
            kt = keys[:, jt * TILE:(jt + 1) * TILE]
            eq = kt == t
            res = jnp.dot(jnp.where(eq, 1.0, 0.0).astype(BF16), pref, preferred_element_type=F32)
            take = jnp.where(res[:, :TILE] + carry < room, 1.0, 0.0)
            selb_ref[:, jt * TILE:(jt + 1) * TILE] = jnp.where(kt > t, 1.0, jnp.where(eq, take, 0.0))
            carry = carry + res[:, TILE:]
        newsel = jnp.where(knew > t, 1.0, jnp.where(knew == t, jnp.where(carry[:, :1] < room, 1.0, 0.0), 0.0))
        newsel_ref[...] = jnp.broadcast_to(newsel, (_SEQ_BLOCK, TILE))


def _dec_index(page_table, qi, wi, kiraw, g, b, cidx, topk):
    DB, n_pages = page_table.shape
    L = n_pages * TILE
    sb = _SEQ_BLOCK
    grid_spec = pltpu.PrefetchScalarGridSpec(
        num_scalar_prefetch=1,
        grid=(DB // sb,),
        in_specs=[pl.BlockSpec((sb, IDX_HEADS, IDX_DIM), lambda s, pt: (s, 0, 0)),
                  pl.BlockSpec((sb, IDX_HEADS, 1), lambda s, pt: (s, 0, 0)),
                  pl.BlockSpec((sb, IDX_DIM), lambda s, pt: (s, 0)),
                  pl.BlockSpec((1, IDX_DIM), lambda s, pt: (0, 0)),
                  pl.BlockSpec((1, IDX_DIM), lambda s, pt: (0, 0)),
                  pl.BlockSpec(memory_space=pl.ANY)],
        out_specs=[pl.BlockSpec((sb, L), lambda s, pt: (s, 0)),
                   pl.BlockSpec((sb, TILE), lambda s, pt: (s, 0)),
                   pl.BlockSpec((sb, IDX_DIM), lambda s, pt: (s, 0))],
        scratch_shapes=[pltpu.VMEM((2, n_pages, IDX_DIM, TILE), F32),
                        pltpu.SemaphoreType.DMA((2,)),
                        pltpu.VMEM((sb, L), F32),
                        pltpu.VMEM((sb, TILE), F32)])
    return pl.pallas_call(
        functools.partial(_dec_index_kernel, topk, n_pages),
        grid_spec=grid_spec,
        out_shape=[jax.ShapeDtypeStruct((DB, L), F32),
                   jax.ShapeDtypeStruct((DB, TILE), F32),
                   jax.ShapeDtypeStruct((DB, IDX_DIM), F32)],
        compiler_params=_cparams(("arbitrary",)),
        name="dec_index",
    )(page_table, qi, wi, kiraw, g.reshape(1, -1), b.reshape(1, -1), cidx)


_PAGE_CHUNK = 8


def _decode_attention(n_pages, q_ref, selb_ref, newsel_ref, kn_ref, vn_ref, dec_ref, kbuf, vbuf, o_ref):
    nch = n_pages // _PAGE_CHUNK
    CL = _PAGE_CHUNK * TILE
    qb = q_ref[0]
    far = dec_ref[1]
    near = dec_ref[0]
    pages = lambda buf, c: jnp.concatenate(
        [buf[c * _PAGE_CHUNK + p] for p in range(_PAGE_CHUNK)], axis=1).astype(BF16)
    ss = []
    for c in range(nch):
        s = jnp.dot(qb, pages(kbuf, c), preferred_element_type=F32)
        bias = jnp.concatenate([far] * (_PAGE_CHUNK - 1) + [near if c + 1 == nch else far], axis=1)
        sel = selb_ref[0, :, c * CL:(c + 1) * CL] > 0.5
        ss.append(jnp.where(sel, s + bias, NEG))
    kn = kn_ref[0].astype(BF16).astype(F32)
    sn = jnp.sum(qb.astype(F32) * kn, axis=1, keepdims=True) + dec_ref[2][:, 0:1]
    sn = jnp.where(newsel_ref[0, :, 0:1] > 0.5, sn, NEG)
    m = jnp.maximum(jnp.max(functools.reduce(jnp.maximum, ss), axis=1, keepdims=True), sn)
    pn = jnp.exp(sn - m)
    ps = [jnp.exp(s - m) for s in ss]
    l = jnp.sum(functools.reduce(jnp.add, ps), axis=1, keepdims=True) + pn
    acc = pn * vn_ref[0]
    for c in range(nch):
        acc = acc + lax.dot_general(ps[c].astype(BF16), pages(vbuf, c), _NT, preferred_element_type=F32)
    o_ref[0] = acc / l


def _dsa_mixer_kernel(topk, n_pages, pt_ref,
                      qiT_ref, wiT_ref, ki_ref, qT_ref, kh_ref, vT_ref, toep_ref,
                      qd_ref, selb_ref, newsel_ref, kn_ref, vn_ref, dec_ref, ck_ref, cv_ref,
                      o_ref, od_ref, skey_ref, sel_ref, m_ref, acc_ref, kbuf, vbuf, sem):
    n = pl.program_id(1)
    nq = pl.num_programs(1)
    seq = pl.program_id(0) * nq + n
    nseq = pl.num_programs(0) * nq

    def copies(s_):
        out = []
        for p in range(n_pages):
            phys = pt_ref[s_, p]
            out.append(pltpu.make_async_copy(ck_ref.at[phys], kbuf.at[p], sem.at[0]))
            out.append(pltpu.make_async_copy(cv_ref.at[phys], vbuf.at[p], sem.at[1]))
        return out

    @pl.when(seq == 0)
    def _():
        for cp in copies(0):
            cp.start()

    for cp in copies(seq):
        cp.wait()
    _decode_attention(n_pages, qd_ref, selb_ref, newsel_ref, kn_ref, vn_ref, dec_ref, kbuf, vbuf, od_ref)

    @pl.when(seq + 1 < nseq)
    def _():
        for cp in copies(seq + 1):
            cp.start()

    _dsa_prompt_block(topk, n, qiT_ref, wiT_ref, ki_ref, qT_ref, kh_ref, vT_ref, toep_ref, o_ref,
                      skey_ref, sel_ref, m_ref, acc_ref)


def _dsa_mixer(page_table, qiT, wiT, kib, qT, kh, vT, toep, topk, qbd, selb, newsel, kn, vn, dec, ck, cv):
    B, hq, S = qT.shape
    nq = S // TILE
    DB, n_pages = page_table.shape
    L = n_pages * TILE
    kvd = KV_HEADS_A * HEAD_DIM
    assert nq % _KEY_UNROLL == 0 and n_pages % _PAGE_CHUNK == 0
    assert DB == B * nq
    cw = (N_HEADS // KV_HEADS_A) * TILE
    dseq = lambda b, n, pt: (b * nq + n, 0, 0)
    grid_spec = pltpu.PrefetchScalarGridSpec(
        num_scalar_prefetch=1,
        grid=(B, nq),
        in_specs=[pl.BlockSpec((1, qiT.shape[1], TILE), lambda b, n, pt: (b, 0, n)),
                  pl.BlockSpec((1, IDX_HEADS, TILE), lambda b, n, pt: (b, 0, n)),
                  pl.BlockSpec((1, S, IDX_DIM), lambda b, n, pt: (b, 0, 0)),
                  pl.BlockSpec((1, hq, TILE), lambda b, n, pt: (b, 0, n)),
                  pl.BlockSpec((1, KV_HEADS_A, S, HEAD_DIM), lambda b, n, pt: (b, 0, 0, 0)),
                  pl.BlockSpec((1, kvd, S), lambda b, n, pt: (b, 0, 0)),
                  pl.BlockSpec(toep.shape, lambda b, n, pt: (0, 0, 0)),
                  pl.BlockSpec((1, N_HEADS, kvd), dseq),
                  pl.BlockSpec((1, 1, L), dseq),
                  pl.BlockSpec((1, 1, TILE), dseq),
                  pl.BlockSpec((1, 1, kvd), dseq),
                  pl.BlockSpec((1, 1, kvd), dseq),
                  pl.BlockSpec(dec.shape, lambda b, n, pt: (0, 0, 0)),
                  pl.BlockSpec(memory_space=pl.ANY),
                  pl.BlockSpec(memory_space=pl.ANY)],
        out_specs=[pl.BlockSpec((1, hq, TILE), lambda b, n, pt: (b, 0, n)),
                   pl.BlockSpec((1, N_HEADS, kvd), dseq)],
        scratch_shapes=[pltpu.VMEM((S, TILE), F32), pltpu.VMEM((S, TILE), F32),
                        pltpu.VMEM((KV_HEADS_A, 8, cw), F32),
                        pltpu.VMEM((KV_HEADS_A, HEAD_DIM + _SUM_ROWS, cw), F32),
                        pltpu.VMEM((n_pages, kvd, TILE), F32),
                        pltpu.VMEM((n_pages, kvd, TILE), F32),
                        pltpu.SemaphoreType.DMA((2,))])
    return pl.pallas_call(
        functools.partial(_dsa_mixer_kernel, topk, n_pages),
        grid_spec=grid_spec,
        out_shape=[jax.ShapeDtypeStruct((B, hq, S), BF16),
                   jax.ShapeDtypeStruct((DB, N_HEADS, kvd), F32)],
        compiler_params=_cparams(("arbitrary", "arbitrary"), _VMEM_LIMIT_MIXER),
        name="dsa_mixer",
    )(page_table, qiT, wiT, kib, qT, kh, vT, toep,
      qbd, selb.reshape(DB, 1, L), newsel.reshape(DB, 1, TILE), kn.reshape(DB, 1, kvd), vn.reshape(DB, 1, kvd),
      dec, ck, cv)


def _swa_dec_kernel(q_ref, wk_ref, wv_ref, kn_ref, vn_ref, dec_ref, sink_ref, o_ref):
    lane = lax.broadcasted_iota(jnp.int32, (N_HEADS, TILE), 1)
    bias = dec_ref[0]
    b0 = dec_ref[2][:, 0:1]
    sk = sink_ref[...]
    for i in range(_SEQ_BLOCK):
        qb = q_ref[i]
        s = jnp.dot(qb, wk_ref[i].astype(BF16), preferred_element_type=F32)
        s = jnp.where(lane >= 1, s + bias, NEG)
        kn = kn_ref[i].astype(BF16).astype(F32)
        sn = jnp.sum(qb.astype(F32) * kn, axis=1, keepdims=True) + b0
        m = jnp.maximum(jnp.maximum(jnp.max(s, axis=1, keepdims=True), sn), sk)
        p = jnp.exp(s - m)
        pn = jnp.exp(sn - m)
        l = jnp.sum(p, axis=1, keepdims=True) + pn + jnp.exp(sk - m)
        out = lax.dot_general(p.astype(BF16), wv_ref[i].astype(BF16), _NT, preferred_element_type=F32) + pn * vn_ref[i]
        o_ref[i] = out / l


def _swa_dec(qbd, wk, wv, kn, vn, dec, sink):
    DB = qbd.shape[0]
    kvd = KV_HEADS_B * HEAD_DIM
    sb = _SEQ_BLOCK
    return pl.pallas_call(
        _swa_dec_kernel,
        grid=(DB // sb,),
        in_specs=[pl.BlockSpec((sb, N_HEADS, kvd), lambda s: (s, 0, 0)),
                  pl.BlockSpec((sb, kvd, WINDOW), lambda s: (s, 0, 0)),
                  pl.BlockSpec((sb, kvd, WINDOW), lambda s: (s, 0, 0)),
                  pl.BlockSpec((sb, 1, kvd), lambda s: (s, 0, 0)),
                  pl.BlockSpec((sb, 1, kvd), lambda s: (s, 0, 0)),
                  pl.BlockSpec(dec.shape, lambda s: (0, 0, 0)),
                  pl.BlockSpec((N_HEADS, 1), lambda s: (0, 0))],
        out_specs=pl.BlockSpec((sb, N_HEADS, kvd), lambda s: (s, 0, 0)),
        out_shape=jax.ShapeDtypeStruct((DB, N_HEADS, kvd), F32),
        compiler_params=_cparams(("arbitrary",)),
        name="swa_dec",
    )(qbd, wk, wv, kn.reshape(DB, 1, kvd), vn.reshape(DB, 1, kvd), dec, sink.reshape(N_HEADS, 1))


def _block_diag_q(q, nkv):
    DB = q.shape[0]
    G = N_HEADS // nkv
    qh = (q * ATTN_SCALE).reshape(DB, N_HEADS, 1, HEAD_DIM)
    own = (np.arange(N_HEADS)[:, None] // G == np.arange(nkv)[None, :]).astype(np.float32)
    return (qh * own[None, :, :, None]).reshape(DB, N_HEADS, nkv * HEAD_DIM).astype(BF16)


def _own_group(o, nkv):
    DB = o.shape[0]
    G = N_HEADS // nkv
    o5 = o.reshape(DB, nkv, G, nkv, HEAD_DIM)
    return jnp.stack([o5[:, g, :, g, :] for g in range(nkv)], axis=1).reshape(DB, N_HEADS * HEAD_DIM)


def _pad_cols(w, mult=TILE):
    pad = (-w.shape[1]) % mult
    return jnp.pad(w, ((0, 0), (0, pad))) if pad else w


def _slots_minor(a):
    n, t, h, d = a.shape
    return a.transpose(0, 2, 3, 1).reshape(n, h * d, t)


def _slots_major(a, heads):
    n, hd, t = a.shape
    return a.reshape(n, heads, hd // heads, t).transpose(0, 3, 1, 2)


def kernel(x_prompt, x_sample, cache_k, cache_v, cache_idx_k, state_win_k, state_win_v, page_table, rel_bias_table, a_w_in, a_w_out, a_idx_ln_g, a_idx_ln_b, b_w_in, b_w_out, b_sink, ln1_g, ln1_b, ln2_g, ln2_b, mlp_w1, mlp_w2):
    B, S, D = x_prompt.shape
    DB, DS, _ = x_sample.shape
    depth = ln1_g.shape[0]
    assert depth == 2 and DS == 1 and S % TILE == 0 and DB % _SEQ_BLOCK == 0
    assert state_win_k.shape[2] == WINDOW and cache_k.shape[2] == TILE
    alpha = (2 * depth) ** 0.25
    n_pages = page_table.shape[1]
    past = n_pages * TILE
    hq = N_HEADS * HEAD_DIM
    kva = KV_HEADS_A * HEAD_DIM
    kvb = KV_HEADS_B * HEAD_DIM
    hi = IDX_HEADS * IDX_DIM

    toep, toep2, dec = _bias_tables(rel_bias_table)

    wa = a_w_in[0]
    c0, c1, c2, c3, c4 = hq, hq + kva, hq + 2 * kva, hq + 2 * kva + hi, hq + 2 * kva + hi + IDX_DIM
    wq, wk, wv, wqi, wki, wwi = wa[:, :c0], wa[:, c0:c1], wa[:, c1:c2], wa[:, c2:c3], wa[:, c3:c4], wa[:, c4:]
    a_wo = a_w_out[0].astype(BF16)

    qT, kTp, vTp, kh, vT, qiT, kiTp, kib, wiT = _prompt_proj(
        x_prompt, wq, wk, wv, KV_HEADS_A, idx=(wqi, wki, wwi, a_idx_ln_g[0], a_idx_ln_b[0]),
        q_scale=ATTN_SCALE * LOG2E)
    xs = x_sample.reshape(DB, D)
    proj = _matmul(xs, _pad_cols(wa).astype(BF16))
    q_s, k_s, v_s = proj[:, :c0], proj[:, c0:c1], proj[:, c1:c2]
    qi_s, kiraw_s, wi_s = proj[:, c2:c3], proj[:, c3:c4], proj[:, c4:c4 + IDX_HEADS]
    selb, newsel, ki_s = _dec_index(
        page_table, qi_s.reshape(DB, IDX_HEADS, IDX_DIM), wi_s.reshape(DB, IDX_HEADS, 1), kiraw_s,
        a_idx_ln_g[0], a_idx_ln_b[0], cache_idx_k[0].transpose(0, 2, 1), min(TOPK_MAX, (past + DS) // 4))
    oT, o_s = _dsa_mixer(page_table, qiT, wiT, kib, qT, kh, vT, toep2, min(TOPK_MAX, S // 4),
                         _block_diag_q(q_s, KV_HEADS_A), selb, newsel, k_s, v_s, dec,
                         _slots_minor(cache_k[0]), _slots_minor(cache_v[0]))

    attn = oT.transpose(0, 2, 1).reshape(B * S, hq)
    y_p = _residual_blocks(x_prompt.reshape(B * S, D), attn, a_wo, ln1_g[0], ln1_b[0],
                           mlp_w1, mlp_w2, 0, ln2_g[0], ln2_b[0], alpha)

    y_s = _residual_blocks(xs, _own_group(o_s, KV_HEADS_A).astype(BF16), a_wo, ln1_g[0], ln1_b[0],
                           mlp_w1, mlp_w2, 0, ln2_g[0], ln2_b[0], alpha)

    wb = b_w_in[0]
    bq, bk, bv = wb[:, :hq], wb[:, hq:hq + kvb], wb[:, hq + kvb:]
    b_wo = b_w_out[0].astype(BF16)

    qT1, k1Tp, v1Tp, kh1, vT1 = _prompt_proj(y_p.reshape(B, S, D), bq, bk, bv, KV_HEADS_B,
                                             q_scale=ATTN_SCALE * LOG2E)
    oT1 = _swa_prompt(qT1, kh1, vT1, toep2, b_sink[0])
    attn1 = oT1.transpose(0, 2, 1).reshape(B * S, hq)
    y_p = _residual_blocks(y_p, attn1, b_wo, ln1_g[1], ln1_b[1], mlp_w1, mlp_w2, 1, ln2_g[1], ln2_b[1], alpha)

    proj1 = _matmul(y_s, wb.astype(BF16))
    q1_s, k1_s, v1_s = proj1[:, :hq], proj1[:, hq:hq + kvb], proj1[:, hq + kvb:]
    win_k = _slots_minor(state_win_k[0])
    win_v = _slots_minor(state_win_v[0])
    o1_s = _swa_dec(_block_diag_q(q1_s, KV_HEADS_B), win_k, win_v, k1_s, v1_s, dec, b_sink[0])
    y_s = _residual_blocks(y_s, _own_group(o1_s, KV_HEADS_B).astype(BF16), b_wo, ln1_g[1], ln1_b[1],
                           mlp_w1, mlp_w2, 1, ln2_g[1], ln2_b[1], alpha)

    npg = S // TILE
    assert min(WINDOW, S) == TILE
    pages = lambda a, heads: _slots_major(a.reshape(B * npg, -1, TILE), heads).reshape(1, B, npg, TILE, heads, HEAD_DIM)
    new_wk_s = jnp.concatenate([win_k[:, :, 1:], k1_s[:, :, None]], axis=2)
    new_wv_s = jnp.concatenate([win_v[:, :, 1:], v1_s[:, :, None]], axis=2)
    return (y_p.reshape(B, S, D),
            y_s.reshape(DB, DS, D),
            pages(kTp, KV_HEADS_A),
            pages(vTp, KV_HEADS_A),
            kiTp.transpose(0, 1, 3, 2).reshape(1, B, npg, TILE, IDX_DIM),
            k_s.reshape(1, DB, DS, KV_HEADS_A, HEAD_DIM),
            v_s.reshape(1, DB, DS, KV_HEADS_A, HEAD_DIM),
            ki_s.reshape(1, DB, DS, IDX_DIM),
            _slots_major(k1Tp[:, npg - 1], KV_HEADS_B)[None],
            _slots_major(v1Tp[:, npg - 1], KV_HEADS_B)[None],
            _slots_major(new_wk_s, KV_HEADS_B)[None],
            _slots_major(new_wv_s, KV_HEADS_B)[None])
```

```python
import functools
import math

import numpy as np
import jax
import jax.numpy as jnp
from jax import lax
from jax.experimental import pallas as pl
from jax.experimental.pallas import tpu as pltpu

F32 = jnp.float32
BF16 = jnp.bfloat16

N_HEADS = 16
HEAD_DIM = 64
KV_HEADS_A = 4
KV_HEADS_B = 2
IDX_HEADS = 8
IDX_DIM = 64
TOPK_MAX = 256
WINDOW = 128
TILE = 128
N_BUCKETS = 32
MAX_DISTANCE = 128
LN_EPS = 1e-5
NEG = -1e30
INT_MIN = -(2 ** 31)
ATTN_SCALE = HEAD_DIM ** -0.5
LOG2E = math.log2(math.e)
IDX_SCALE = IDX_HEADS ** -0.5 * IDX_DIM ** -0.5
FAR_BUCKET = N_BUCKETS - 1
FAR_DISTANCE = 113

_NT = (((1,), (1,)), ((), ()))
_VMEM_LIMIT = 48 * 1024 * 1024
_VMEM_LIMIT_MIXER = 58 * 1024 * 1024


def _cparams(sem, vmem_limit=_VMEM_LIMIT):
    return pltpu.CompilerParams(dimension_semantics=sem, vmem_limit_bytes=vmem_limit)


def _bucket_np(dist):
    n = np.maximum(dist, 0)
    max_exact = N_BUCKETS // 2
    nf = np.maximum(n, max_exact).astype(np.float64)
    val = np.log(nf / max_exact) / math.log(MAX_DISTANCE / max_exact) * (N_BUCKETS - max_exact)
    frac = val - np.floor(val)
    interior = (n > max_exact) & (n < MAX_DISTANCE)
    assert not np.any(interior & ((frac < 1e-6) | (frac > 1 - 1e-6))), "bucket boundary too close to an integer"
    large = np.minimum(max_exact + val.astype(np.int32), N_BUCKETS - 1)
    out = np.where(n < max_exact, n, large).astype(np.int32)
    assert np.all(out[n >= FAR_DISTANCE] == FAR_BUCKET)
    return out


def _key_to_float(k):
    return lax.bitcast_convert_type(k ^ (lax.shift_right_arithmetic(k, 31) & 0x7FFFFFFF), F32)


def _kth_largest_search(count_ge, topk, shape):
    def bit_body(b, carry):
        t, cnt_t = carry
        cand = t ^ lax.shift_left(jnp.int32(1), 31 - b)
        cnt = count_ge(_key_to_float(cand))
        ok = cnt >= topk
        return jnp.where(ok, cand, t), jnp.where(ok, cnt, cnt_t)

    init = (jnp.full(shape, INT_MIN, jnp.int32), jnp.full(shape, 3e38, F32))
    t, cnt_t = lax.fori_loop(0, 32, bit_body, init)
    return _key_to_float(t), cnt_t


def _ln(y, g, b):
    mu = jnp.mean(y, axis=-1, keepdims=True)
    yc = y - mu
    var = jnp.mean(yc * yc, axis=-1, keepdims=True)
    return yc * lax.rsqrt(var + LN_EPS) * g + b


def _bias_kernel(table_ref, bk_ref, bkd_ref, toep_ref, toep2_ref, dec_ref):
    bk = bk_ref[...]
    bkd = bkd_ref[...]
    dec_rows = []
    for h in range(N_HEADS):
        acc = jnp.zeros(bk.shape, F32)
        accd = jnp.zeros(bkd.shape, F32)
        for b in range(N_BUCKETS):
            val = table_ref[b, h]
            acc = jnp.where(bk == b, val, acc)
            accd = jnp.where(bkd == b, val, accd)
        toep_ref[h] = acc
        toep2_ref[h] = acc * LOG2E
        dec_rows.append(accd)
    for kind in range(3):
        dec_ref[kind] = jnp.concatenate([d[kind:kind + 1, :] for d in dec_rows], axis=0)


def _bias_tables(table):
    i = np.arange(TILE)[None, :]
    j = np.arange(2 * TILE)[:, None]
    bk = np.concatenate([np.full((TILE, TILE), FAR_BUCKET, np.int32), _bucket_np(i + TILE - j)], axis=0)
    bkd = np.zeros((8, TILE), np.int32)
    bkd[0] = _bucket_np(TILE - np.arange(TILE))
    bkd[1] = FAR_BUCKET
    bkd[2] = 0
    return pl.pallas_call(
        _bias_kernel,
        grid=(1,),
        in_specs=[pl.BlockSpec(memory_space=pltpu.SMEM),
                  pl.BlockSpec((3 * TILE, TILE), lambda h: (0, 0)),
                  pl.BlockSpec((8, TILE), lambda h: (0, 0))],
        out_specs=[pl.BlockSpec((N_HEADS, 3 * TILE, TILE), lambda h: (0, 0, 0)),
                   pl.BlockSpec((N_HEADS, 3 * TILE, TILE), lambda h: (0, 0, 0)),
                   pl.BlockSpec((3, N_HEADS, TILE), lambda h: (0, 0, 0))],
        out_shape=[jax.ShapeDtypeStruct((N_HEADS, 3 * TILE, TILE), F32),
                   jax.ShapeDtypeStruct((N_HEADS, 3 * TILE, TILE), F32),
                   jax.ShapeDtypeStruct((3, N_HEADS, TILE), F32)],
        compiler_params=_cparams(("arbitrary",)),
        name="bias_tables",
    )(table, jnp.asarray(bk), jnp.asarray(bkd))


def _normalize_cols(y):
    mu = jnp.mean(y, axis=0, keepdims=True)
    yc = y - mu
    var = jnp.mean(yc * yc, axis=0, keepdims=True)
    return yc * lax.rsqrt(var + LN_EPS)


def _proj_kernel(nkv, has_idx, *refs):
    if has_idx:
        (x_ref, wqT_ref, wkT_ref, wvT_ref, wkh_ref, wqiT_ref, wkiT_ref, wki_ref, wwiT_ref, g_ref, b_ref,
         gc_ref, bc_ref, qT_ref, kTp_ref, vTp_ref, kh_ref, vT_ref, qiT_ref, kiTp_ref, kib_ref, wiT_ref) = refs
    else:
        (x_ref, wqT_ref, wkT_ref, wvT_ref, wkh_ref, qT_ref, kTp_ref, vTp_ref, kh_ref, vT_ref) = refs
    xb = x_ref[0].astype(BF16)
    npg = xb.shape[0] // TILE
    qT_ref[0] = lax.dot_general(wqT_ref[...], xb, _NT, preferred_element_type=F32).astype(BF16)
    kT = lax.dot_general(wkT_ref[...], xb, _NT, preferred_element_type=F32)
    vT = lax.dot_general(wvT_ref[...], xb, _NT, preferred_element_type=F32)
    vT_ref[0] = vT.astype(BF16)
    for j in range(npg):
        kTp_ref[0, j] = kT[:, j * TILE:(j + 1) * TILE]
        vTp_ref[0, j] = vT[:, j * TILE:(j + 1) * TILE]
    for g in range(nkv):
        kh_ref[0, g] = jnp.dot(xb, wkh_ref[g], preferred_element_type=F32).astype(BF16)
    if has_idx:
        qiT_ref[0] = lax.dot_general(wqiT_ref[...], xb, _NT, preferred_element_type=F32).astype(BF16)
        kiT = _normalize_cols(lax.dot_general(wkiT_ref[...], xb, _NT, preferred_element_type=F32))
        for j in range(npg):
            kiTp_ref[0, j] = kiT[:, j * TILE:(j + 1) * TILE] * gc_ref[...] + bc_ref[...]
        ki = _ln(jnp.dot(xb, wki_ref[...], preferred_element_type=F32), g_ref[...], b_ref[...])
        kib_ref[0] = ki.astype(BF16)
        wiT_ref[0] = lax.dot_general(wwiT_ref[...], xb, _NT, preferred_element_type=F32)


def _prompt_proj(x, wq, wk, wv, nkv, idx=None, q_scale=ATTN_SCALE, tm=512):
    B, S, D = x.shape
    tm = min(tm, S)
    kvd = nkv * HEAD_DIM
    hq = N_HEADS * HEAD_DIM
    wqT = (wq * q_scale).T.astype(BF16)
    wkh = wk.reshape(D, nkv, HEAD_DIM).transpose(1, 0, 2).astype(BF16)
    full2 = lambda a: pl.BlockSpec(a.shape, lambda b, m: (0,) * a.ndim)
    ins = [x, wqT, wk.T.astype(BF16), wv.T.astype(BF16), wkh]
    in_specs = [pl.BlockSpec((1, tm, D), lambda b, m: (b, m, 0))] + [full2(a) for a in ins[1:]]
    out_shape = [jax.ShapeDtypeStruct((B, hq, S), BF16),
                 jax.ShapeDtypeStruct((B, S // TILE, kvd, TILE), F32),
                 jax.ShapeDtypeStruct((B, S // TILE, kvd, TILE), F32),
                 jax.ShapeDtypeStruct((B, nkv, S, HEAD_DIM), BF16),
                 jax.ShapeDtypeStruct((B, kvd, S), BF16)]
    out_specs = [pl.BlockSpec((1, hq, tm), lambda b, m: (b, 0, m)),
                 pl.BlockSpec((1, tm // TILE, kvd, TILE), lambda b, m: (b, m, 0, 0)),
                 pl.BlockSpec((1, tm // TILE, kvd, TILE), lambda b, m: (b, m, 0, 0)),
                 pl.BlockSpec((1, nkv, tm, HEAD_DIM), lambda b, m: (b, 0, m, 0)),
                 pl.BlockSpec((1, kvd, tm), lambda b, m: (b, 0, m))]
    if idx is not None:
        wqi, wki, wwi, g, bb = idx
        extra = [wqi.T.astype(BF16), wki.T.astype(BF16), wki.astype(BF16), wwi.T.astype(BF16),
                 g.reshape(1, -1), bb.reshape(1, -1),
                 jnp.broadcast_to(g.reshape(-1, 1), (IDX_DIM, TILE)), jnp.broadcast_to(bb.reshape(-1, 1), (IDX_DIM, TILE))]
        ins += extra
        in_specs += [full2(a) for a in extra]
        hi = IDX_HEADS * IDX_DIM
        out_shape += [jax.ShapeDtypeStruct((B, hi, S), BF16),
                      jax.ShapeDtypeStruct((B, S // TILE, IDX_DIM, TILE), F32),
                      jax.ShapeDtypeStruct((B, S, IDX_DIM), BF16),
                      jax.ShapeDtypeStruct((B, IDX_HEADS, S), F32)]
        out_specs += [pl.BlockSpec((1, hi, tm), lambda b, m: (b, 0, m)),
                      pl.BlockSpec((1, tm // TILE, IDX_DIM, TILE), lambda b, m: (b, m, 0, 0)),
                      pl.BlockSpec((1, tm, IDX_DIM), lambda b, m: (b, m, 0)),
                      pl.BlockSpec((1, IDX_HEADS, tm), lambda b, m: (b, 0, m))]
    return pl.pallas_call(
        functools.partial(_proj_kernel, nkv, idx is not None),
        grid=(B, S // tm),
        in_specs=in_specs,
        out_specs=out_specs,
        out_shape=out_shape,
        compiler_params=_cparams(("arbitrary", "arbitrary")),
        name="prompt_proj_idx" if idx is not None else "prompt_proj",
    )(*ins)


_KEY_UNROLL = 4
_SUM_ROWS = 16
_ATT_TILES = 4


def _dsa_prompt_block(topk, n, qiT_ref, wiT_ref, ki_ref, qT_ref, kh_ref, vT_ref, toep_ref, o_ref,
                      skey_ref, sel_ref, m_ref, acc_ref):
    ST = _KEY_UNROLL * TILE
    ns = n // _KEY_UNROLL + 1
    G = N_HEADS // KV_HEADS_A
    rows = lax.broadcasted_iota(jnp.int32, (ST, TILE), 0)
    lanes = lax.broadcasted_iota(jnp.int32, (ST, TILE), 1)
    qpos = n * TILE + lanes

    qi = qiT_ref[0]
    wi = wiT_ref[0]

    def score_body(u, carry):
        off = pl.multiple_of(u * ST, ST)
        kij = ki_ref[0, pl.ds(off, ST), :]
        acc = jnp.zeros((ST, TILE), F32)
        for h in range(0, IDX_HEADS, 2):
            qpair = jnp.concatenate([qi[IDX_DIM * h:IDX_DIM * (h + 1), :],
                                     qi[IDX_DIM * (h + 1):IDX_DIM * (h + 2), :]], axis=1)
            s = jnp.dot(kij, qpair, preferred_element_type=F32)
            acc = acc + jnp.maximum(s[:, :TILE], 0.0) * wi[h:h + 1, :]
            acc = acc + jnp.maximum(s[:, TILE:], 0.0) * wi[h + 1:h + 2, :]
        sc = acc * IDX_SCALE
        sc = jnp.where(off + rows <= qpos, sc, NEG)
        sc = jnp.where(sc == 0.0, 0.0, sc)
        skey_ref[pl.ds(off, ST), :] = sc
        return carry

    lax.fori_loop(0, ns, score_body, 0)

    def count(pred):
        def body(u, c):
            off = pl.multiple_of(u * ST, ST)
            x = jnp.where(pred(skey_ref[pl.ds(off, ST), :]), 1.0, 0.0).reshape(ST // 8, 8, TILE)
            while x.shape[0] > 1:
                half = x.shape[0] // 2
                x = x[:half] + x[half:]
            return c + x[0]
        c8 = lax.fori_loop(0, ns, body, jnp.zeros((8, TILE), F32))
        return jnp.sum(c8, axis=0, keepdims=True)

    t, cnt_t = _kth_largest_search(lambda c: count(lambda kt: kt >= c), topk, (1, TILE))
    exact_fit = jnp.max(jnp.abs(cnt_t - topk)) == 0.0

    @pl.when(exact_fit)
    def _():
        def sel_body(u, carry):
            off = pl.multiple_of(u * ST, ST)
            sel = jnp.where(skey_ref[pl.ds(off, ST), :] >= t, 1.0, 0.0)
            sel_ref[pl.ds(off, ST), :] = jnp.where(off + rows <= qpos, sel, 0.0)
            return carry

        lax.fori_loop(0, ns, sel_body, 0)

    @pl.when(jnp.logical_not(exact_fit))
    def _():
        room = topk - count(lambda kt: kt > t)
        ii = lax.broadcasted_iota(jnp.int32, (TILE, TILE), 0)
        jj = lax.broadcasted_iota(jnp.int32, (TILE, TILE), 1)
        lower = jnp.where(jj < ii, 1.0, 0.0).astype(BF16)
        ones8 = jnp.ones((8, TILE), BF16)

        def sel_body(u, carry):
            for k in range(_KEY_UNROLL):
                off = pl.multiple_of(u * ST + k * TILE, TILE)
                kt = skey_ref[pl.ds(off, TILE), :]
                eq = kt == t
                eqb = jnp.where(eq, 1.0, 0.0).astype(BF16)
                before = jnp.dot(lower, eqb, preferred_element_type=F32) + carry
                tot = jnp.dot(ones8, eqb, preferred_element_type=F32)[0:1]
                sel = jnp.where(kt > t, 1.0, jnp.where(eq, jnp.where(before < room, 1.0, 0.0), 0.0))
                sel_ref[pl.ds(off, TILE), :] = jnp.where(off + ii <= n * TILE + jj, sel, 0.0)
                carry = carry + tot
            return carry

        lax.fori_loop(0, ns, sel_body, jnp.zeros((1, TILE), F32))

    q = qT_ref[0]
    qgs = [jnp.concatenate([q[(g * G + r) * HEAD_DIM:(g * G + r + 1) * HEAD_DIM, :] for r in range(G)], axis=1)
           for g in range(KV_HEADS_A)]
    m_ref[...] = jnp.full(m_ref.shape, NEG, F32)
    acc_ref[...] = jnp.zeros(acc_ref.shape, F32)

    AT = _ATT_TILES * TILE
    ones_rows = jnp.ones((_SUM_ROWS, AT), BF16)

    def att_body(u, carry):
        off = pl.multiple_of(u * AT, AT)
        tiles = [u * _ATT_TILES + k for k in range(_ATT_TILES)]
        roffs = [pl.multiple_of(jnp.where(j == n, 2 * TILE, jnp.where(j == n - 1, TILE, 0)), TILE) for j in tiles]
        selv = sel_ref[pl.ds(off, AT), :] > 0.5
        ss = [jnp.dot(kh_ref[0, g, pl.ds(off, AT), :], qgs[g], preferred_element_type=F32)
              for g in range(KV_HEADS_A)]
        for g in range(KV_HEADS_A):
            s = ss[g]
            s = jnp.concatenate(
                [jnp.where(selv,
                           s[:, r * TILE:(r + 1) * TILE]
                           + jnp.concatenate([toep_ref[g * G + r, pl.ds(ro, TILE), :] for ro in roffs], axis=0),
                           NEG)
                 for r in range(G)], axis=1)
            m = m_ref[g]
            m_new = jnp.maximum(m, jnp.max(s, axis=0, keepdims=True))
            alpha = jnp.exp2(m - m_new)
            p = jnp.exp2(s - m_new[0:1])
            m_ref[g] = m_new
            vt = jnp.concatenate([vT_ref[0, g * HEAD_DIM:(g + 1) * HEAD_DIM, pl.ds(off, AT)], ones_rows], axis=0)
            acc_ref[g] = acc_ref[g] * alpha[0:1] + jnp.dot(vt, p.astype(BF16), preferred_element_type=F32)
        return carry

    lax.fori_loop(0, n // _ATT_TILES + 1, att_body, 0)
    for g in range(KV_HEADS_A):
        out = acc_ref[g, 0:HEAD_DIM, :] / acc_ref[g, HEAD_DIM:HEAD_DIM + 1, :]
        for r in range(G):
            h = g * G + r
            o_ref[0, h * HEAD_DIM:(h + 1) * HEAD_DIM, :] = out[:, r * TILE:(r + 1) * TILE].astype(o_ref.dtype)


def _swa_prompt_kernel(sink_ref, qT_ref, khp_ref, khc_ref, vTp_ref, vTc_ref, toep_ref, o_ref):
    n = pl.program_id(1)
    G = N_HEADS // KV_HEADS_B
    rows = lax.broadcasted_iota(jnp.int32, (2 * TILE, TILE), 0)
    lanes = lax.broadcasted_iota(jnp.int32, (2 * TILE, TILE), 1)
    dj = rows - lanes
    inwin = jnp.where(dj >= TILE - WINDOW + 1, jnp.where(dj <= TILE, 1.0, 0.0), 0.0)
    inwin = jnp.where(rows >= TILE, inwin, jnp.where(n > 0, inwin, 0.0))
    maskb = jnp.concatenate([inwin] * G, axis=1) > 0.5
    q = qT_ref[0]
    ones_rows = jnp.ones((_SUM_ROWS, 2 * TILE), BF16)
    for g in range(KV_HEADS_B):
        heads = [g * G + r for r in range(G)]
        qg = jnp.concatenate([q[h * HEAD_DIM:(h + 1) * HEAD_DIM, :] for h in heads], axis=1)
        kk = jnp.concatenate([khp_ref[0, g], khc_ref[0, g]], axis=0)
        s = jnp.dot(kk, qg, preferred_element_type=F32)
        bias = jnp.concatenate([toep_ref[h, TILE:3 * TILE, :] for h in heads], axis=1)
        s = jnp.where(maskb, s + bias, NEG)
        sink = jnp.concatenate([jnp.full((1, TILE), sink_ref[h] * LOG2E, F32) for h in heads], axis=1)
        m = jnp.maximum(jnp.max(s, axis=0, keepdims=True), sink)
        p = jnp.exp2(s - m)
        vv = jnp.concatenate([vTp_ref[0, g * HEAD_DIM:(g + 1) * HEAD_DIM, :],
                              vTc_ref[0, g * HEAD_DIM:(g + 1) * HEAD_DIM, :]], axis=1)
        pv = jnp.dot(jnp.concatenate([vv, ones_rows], axis=0), p.astype(BF16), preferred_element_type=F32)
        out = pv[0:HEAD_DIM, :] / (pv[HEAD_DIM:HEAD_DIM + 1, :] + jnp.exp2(sink - m))
        for r, h in enumerate(heads):
            o_ref[0, h * HEAD_DIM:(h + 1) * HEAD_DIM, :] = out[:, r * TILE:(r + 1) * TILE].astype(o_ref.dtype)


def _swa_prompt(qT, kh, vT, toep, sink):
    B, hq, S = qT.shape
    nb = S // TILE
    prev = lambda n: jnp.maximum(n - 1, 0)
    return pl.pallas_call(
        _swa_prompt_kernel,
        grid=(B, nb),
        in_specs=[pl.BlockSpec(memory_space=pltpu.SMEM),
                  pl.BlockSpec((1, hq, TILE), lambda b, n: (b, 0, n)),
                  pl.BlockSpec((1, KV_HEADS_B, TILE, HEAD_DIM), lambda b, n: (b, 0, prev(n), 0)),
                  pl.BlockSpec((1, KV_HEADS_B, TILE, HEAD_DIM), lambda b, n: (b, 0, n, 0)),
                  pl.BlockSpec((1, KV_HEADS_B * HEAD_DIM, TILE), lambda b, n: (b, 0, prev(n))),
                  pl.BlockSpec((1, KV_HEADS_B * HEAD_DIM, TILE), lambda b, n: (b, 0, n)),
                  pl.BlockSpec(toep.shape, lambda b, n: (0, 0, 0))],
        out_specs=pl.BlockSpec((1, hq, TILE), lambda b, n: (b, 0, n)),
        out_shape=jax.ShapeDtypeStruct((B, hq, S), BF16),
        compiler_params=_cparams(("arbitrary", "arbitrary")),
        name="swa_prompt",
    )(sink, qT, kh, kh, vT, vT, toep)


def _residual_kernel(alpha, x_ref, a_ref, wo_ref, g1_ref, b1_ref, w1_ref, w2_ref, g2_ref, b2_ref, o_ref,
                     y_ref, acc_ref):
    f = pl.program_id(1)

    @pl.when(f == 0)
    def _():
        y = alpha * x_ref[...] + jnp.dot(a_ref[...], wo_ref[...], preferred_element_type=F32)
        y_ref[...] = _ln(y, g1_ref[...], b1_ref[...])
        acc_ref[...] = jnp.zeros_like(acc_ref)

    h = jnp.maximum(jnp.dot(y_ref[...].astype(BF16), w1_ref[0].astype(BF16), preferred_element_type=F32), 0.0)
    acc_ref[...] += jnp.dot((h * h).astype(BF16), w2_ref[0].astype(BF16), preferred_element_type=F32)

    @pl.when(f == pl.num_programs(1) - 1)
    def _():
        o_ref[...] = _ln(alpha * y_ref[...] + acc_ref[...], g2_ref[...], b2_ref[...])


def _residual_blocks(x, a, wo, g1, b1, w1, w2, layer, g2, b2, alpha, tm=1024, tf=512):
    M, D = x.shape
    FF = w1.shape[2]
    tm = min(tm, M)
    row = lambda v: v.reshape(1, D)
    vec = pl.BlockSpec((1, D), lambda m, f: (0, 0))
    return pl.pallas_call(
        functools.partial(_residual_kernel, alpha),
        grid=(M // tm, FF // tf),
        in_specs=[pl.BlockSpec((tm, D), lambda m, f: (m, 0)),
                  pl.BlockSpec((tm, a.shape[1]), lambda m, f: (m, 0)),
                  pl.BlockSpec(wo.shape, lambda m, f: (0, 0)),
                  vec, vec,
                  pl.BlockSpec((1, D, tf), lambda m, f: (layer, 0, f)),
                  pl.BlockSpec((1, tf, D), lambda m, f: (layer, f, 0)),
                  vec, vec],
        out_specs=pl.BlockSpec((tm, D), lambda m, f: (m, 0)),
        out_shape=jax.ShapeDtypeStruct((M, D), F32),
        scratch_shapes=[pltpu.VMEM((tm, D), F32), pltpu.VMEM((tm, D), F32)],
        compiler_params=_cparams(("arbitrary", "arbitrary")),
        name="residual_blocks",
    )(x, a, wo, row(g1), row(b1), w1, w2, row(g2), row(b2))


def _matmul_kernel(x_ref, w_ref, o_ref):
    o_ref[...] = jnp.dot(x_ref[...].astype(BF16), w_ref[...], preferred_element_type=F32)


def _matmul(x, w):
    M, K = x.shape
    N = w.shape[1]
    return pl.pallas_call(
        _matmul_kernel,
        grid=(1,),
        in_specs=[pl.BlockSpec((M, K), lambda i: (0, 0)), pl.BlockSpec((K, N), lambda i: (0, 0))],
        out_specs=pl.BlockSpec((M, N), lambda i: (0, 0)),
        out_shape=jax.ShapeDtypeStruct((M, N), F32),
        compiler_params=_cparams(("arbitrary",)),
        name="sample_proj",
    )(x, w)


_SEQ_BLOCK = 16
_IDX_SLOTS = 4


def _dec_index_kernel(topk, n_pages, pt_ref, qi_ref, wi_ref, kiraw_ref, g_ref, b_ref, cidx_ref,
                      selb_ref, newsel_ref, kiln_ref, buf, sem, sc_ref, snew_ref):
    step = pl.program_id(0)
    nseq = pl.num_programs(0) * _SEQ_BLOCK
    L = n_pages * TILE

    def page_copy(seq, slot, p):
        return pltpu.make_async_copy(cidx_ref.at[pt_ref[seq, p]], buf.at[slot, p], sem.at[slot])

    def start(seq, slot):
        for p in range(n_pages):
            page_copy(seq, slot, p).start()

    def wait(seq, slot):
        for p in range(n_pages):
            page_copy(seq, slot, p).wait()

    ahead = _IDX_SLOTS - 1

    @pl.when(step == 0)
    def _():
        for s0 in range(ahead):
            start(s0, s0)

    kiln_ref[...] = _ln(kiraw_ref[...], g_ref[...], b_ref[...])

    def seq_body(i, carry):
        seq = step * _SEQ_BLOCK + i
        slot = i % _IDX_SLOTS

        @pl.when(seq + ahead < nseq)
        def _():
            start(seq + ahead, (i + ahead) % _IDX_SLOTS)

        wait(seq, slot)
        xk = jnp.concatenate([buf[slot, p] for p in range(n_pages)], axis=1).astype(BF16)
        qib = qi_ref[i].astype(BF16)
        s = jnp.dot(qib, xk, preferred_element_type=F32)
        w = wi_ref[i]
        row = jnp.sum(jnp.maximum(s, 0.0) * w, axis=0, keepdims=True) * IDX_SCALE
        sc_ref[pl.ds(i, 1), :] = jnp.where(row == 0.0, 0.0, row)
        kn = kiln_ref[pl.ds(i, 1), :].astype(BF16).astype(F32)
        sn = jnp.sum(qib.astype(F32) * kn, axis=1, keepdims=True)
        snew = jnp.sum(jnp.maximum(sn, 0.0) * w, axis=0, keepdims=True) * IDX_SCALE
        snew_ref[pl.ds(i, 1), :] = jnp.broadcast_to(jnp.where(snew == 0.0, 0.0, snew), (1, TILE))
        return carry

    lax.fori_loop(0, _SEQ_BLOCK, seq_body, 0)
    keys = sc_ref[...]
    knew = snew_ref[:, 0:1]

    def count(pred):
        return (jnp.sum(jnp.where(pred(keys), 1.0, 0.0), axis=1, keepdims=True)
                + jnp.where(pred(knew), 1.0, 0.0))

    t, cnt_t = _kth_largest_search(lambda c: count(lambda k: k >= c), topk, (_SEQ_BLOCK, 1))
    exact_fit = jnp.max(jnp.abs(cnt_t - topk)) == 0.0

    @pl.when(exact_fit)
    def _():
        selb_ref[...] = jnp.where(keys >= t, 1.0, 0.0)
        newsel_ref[...] = jnp.broadcast_to(jnp.where(knew >= t, 1.0, 0.0), (_SEQ_BLOCK, TILE))

    @pl.when(jnp.logical_not(exact_fit))
    def _():
        room = topk - count(lambda k: k > t)
        ii = lax.broadcasted_iota(jnp.int32, (TILE, 2 * TILE), 0)
        jj = lax.broadcasted_iota(jnp.int32, (TILE, 2 * TILE), 1)
        pref = jnp.where(jj >= TILE, 1.0, jnp.where(ii < jj, 1.0, 0.0)).astype(BF16)
        carry = jnp.zeros((_SEQ_BLOCK, TILE), F32)
        for jt in range(n_pages):
            kt = keys[:, jt * TILE:(jt + 1) * TILE]
            eq = kt == t
            res = jnp.dot(jnp.where(eq, 1.0, 0.0).astype(BF16), pref, preferred_element_type=F32)
            take = jnp.where(res[:, :TILE] + carry < room, 1.0, 0.0)
            selb_ref[:, jt * TILE:(jt + 1) * TILE] = jnp.where(kt > t, 1.0, jnp.where(eq, take, 0.0))
            carry = carry + res[:, TILE:]
        newsel = jnp.where(knew > t, 1.0, jnp.where(knew == t, jnp.where(carry[:, :1] < room, 1.0, 0.0), 0.0))
        newsel_ref[...] = jnp.broadcast_to(newsel, (_SEQ_BLOCK, TILE))


def _dec_index(page_table, qi, wi, kiraw, g, b, cidx, topk):
    DB, n_pages = page_table.shape
    L = n_pages * TILE
    sb = _SEQ_BLOCK
    grid_spec = pltpu.PrefetchScalarGridSpec(
        num_scalar_prefetch=1,
        grid=(DB // sb,),
        in_specs=[pl.BlockSpec((sb, IDX_HEADS, IDX_DIM), lambda s, pt: (s, 0, 0)),
                  pl.BlockSpec((sb, IDX_HEADS, 1), lambda s, pt: (s, 0, 0)),
                  pl.BlockSpec((sb, IDX_DIM), lambda s, pt: (s, 0)),
                  pl.BlockSpec((1, IDX_DIM), lambda s, pt: (0, 0)),
                  pl.BlockSpec((1, IDX_DIM), lambda s, pt: (0, 0)),
                  pl.BlockSpec(memory_space=pl.ANY)],
        out_specs=[pl.BlockSpec((sb, L), lambda s, pt: (s, 0)),
                   pl.BlockSpec((sb, TILE), lambda s, pt: (s, 0)),
                   pl.BlockSpec((sb, IDX_DIM), lambda s, pt: (s, 0))],
        scratch_shapes=[pltpu.VMEM((_IDX_SLOTS, n_pages, IDX_DIM, TILE), F32),
                        pltpu.SemaphoreType.DMA((_IDX_SLOTS,)),
                        pltpu.VMEM((sb, L), F32),
                        pltpu.VMEM((sb, TILE), F32)])
    return pl.pallas_call(
        functools.partial(_dec_index_kernel, topk, n_pages),
        grid_spec=grid_spec,
        out_shape=[jax.ShapeDtypeStruct((DB, L), F32),
                   jax.ShapeDtypeStruct((DB, TILE), F32),
                   jax.ShapeDtypeStruct((DB, IDX_DIM), F32)],
        compiler_params=_cparams(("arbitrary",)),
        name="dec_index",
    )(page_table, qi, wi, kiraw, g.reshape(1, -1), b.reshape(1, -1), cidx)


_PAGE_CHUNK = 8


def _decode_attention(n_pages, q_ref, selb_ref, newsel_ref, kn_ref, vn_ref, dec_ref, kbuf, vbuf, o_ref):
    nch = n_pages // _PAGE_CHUNK
    CL = _PAGE_CHUNK * TILE
    qb = q_ref[0]
    far = dec_ref[1]
    near = dec_ref[0]
    pages = lambda buf, c: jnp.concatenate(
        [buf[c * _PAGE_CHUNK + p] for p in range(_PAGE_CHUNK)], axis=1).astype(BF16)
    ss = []
    for c in range(nch):
        s = jnp.dot(qb, pages(kbuf, c), preferred_element_type=F32)
        bias = jnp.concatenate([far] * (_PAGE_CHUNK - 1) + [near if c + 1 == nch else far], axis=1)
        sel = selb_ref[0, :, c * CL:(c + 1) * CL] > 0.5
        ss.append(jnp.where(sel, s + bias, NEG))
    kn = kn_ref[0].astype(BF16).astype(F32)
    sn = jnp.sum(qb.astype(F32) * kn, axis=1, keepdims=True) + dec_ref[2][:, 0:1]
    sn = jnp.where(newsel_ref[0, :, 0:1] > 0.5, sn, NEG)
    m = jnp.maximum(jnp.max(functools.reduce(jnp.maximum, ss), axis=1, keepdims=True), sn)
    pn = jnp.exp(sn - m)
    ps = [jnp.exp(s - m) for s in ss]
    l = jnp.sum(functools.reduce(jnp.add, ps), axis=1, keepdims=True) + pn
    acc = pn * vn_ref[0]
    for c in range(nch):
        acc = acc + lax.dot_general(ps[c].astype(BF16), pages(vbuf, c), _NT, preferred_element_type=F32)
    o_ref[0] = acc / l


def _dsa_mixer_kernel(topk, n_pages, pt_ref,
                      qiT_ref, wiT_ref, ki_ref, qT_ref, kh_ref, vT_ref, toep_ref,
                      qd_ref, selb_ref, newsel_ref, kn_ref, vn_ref, dec_ref, ck_ref, cv_ref,
                      o_ref, od_ref, skey_ref, sel_ref, m_ref, acc_ref, kbuf, vbuf, sem):
    n = pl.program_id(1)
    nq = pl.num_programs(1)
    seq = pl.program_id(0) * nq + n
    nseq = pl.num_programs(0) * nq

    def copies(s_):
        out = []
        for p in range(n_pages):
            phys = pt_ref[s_, p]
            out.append(pltpu.make_async_copy(ck_ref.at[phys], kbuf.at[p], sem.at[0]))
            out.append(pltpu.make_async_copy(cv_ref.at[phys], vbuf.at[p], sem.at[1]))
        return out

    @pl.when(seq == 0)
    def _():
        for cp in copies(0):
            cp.start()

    for cp in copies(seq):
        cp.wait()
    _decode_attention(n_pages, qd_ref, selb_ref, newsel_ref, kn_ref, vn_ref, dec_ref, kbuf, vbuf, od_ref)

    @pl.when(seq + 1 < nseq)
    def _():
        for cp in copies(seq + 1):
            cp.start()

    _dsa_prompt_block(topk, n, qiT_ref, wiT_ref, ki_ref, qT_ref, kh_ref, vT_ref, toep_ref, o_ref,
                      skey_ref, sel_ref, m_ref, acc_ref)


def _dsa_mixer(page_table, qiT, wiT, kib, qT, kh, vT, toep, topk, qbd, selb, newsel, kn, vn, dec, ck, cv):
    B, hq, S = qT.shape
    nq = S // TILE
    DB, n_pages = page_table.shape
    L = n_pages * TILE
    kvd = KV_HEADS_A * HEAD_DIM
    assert nq % _KEY_UNROLL == 0 and n_pages % _PAGE_CHUNK == 0
    assert DB == B * nq
    cw = (N_HEADS // KV_HEADS_A) * TILE
    dseq = lambda b, n, pt: (b * nq + n, 0, 0)
    grid_spec = pltpu.PrefetchScalarGridSpec(
        num_scalar_prefetch=1,
        grid=(B, nq),
        in_specs=[pl.BlockSpec((1, qiT.shape[1], TILE), lambda b, n, pt: (b, 0, n)),
                  pl.BlockSpec((1, IDX_HEADS, TILE), lambda b, n, pt: (b, 0, n)),
                  pl.BlockSpec((1, S, IDX_DIM), lambda b, n, pt: (b, 0, 0)),
                  pl.BlockSpec((1, hq, TILE), lambda b, n, pt: (b, 0, n)),
                  pl.BlockSpec((1, KV_HEADS_A, S, HEAD_DIM), lambda b, n, pt: (b, 0, 0, 0)),
                  pl.BlockSpec((1, kvd, S), lambda b, n, pt: (b, 0, 0)),
                  pl.BlockSpec(toep.shape, lambda b, n, pt: (0, 0, 0)),
                  pl.BlockSpec((1, N_HEADS, kvd), dseq),
                  pl.BlockSpec((1, 1, L), dseq),
                  pl.BlockSpec((1, 1, TILE), dseq),
                  pl.BlockSpec((1, 1, kvd), dseq),
                  pl.BlockSpec((1, 1, kvd), dseq),
                  pl.BlockSpec(dec.shape, lambda b, n, pt: (0, 0, 0)),
                  pl.BlockSpec(memory_space=pl.ANY),
                  pl.BlockSpec(memory_space=pl.ANY)],
        out_specs=[pl.BlockSpec((1, hq, TILE), lambda b, n, pt: (b, 0, n)),
                   pl.BlockSpec((1, N_HEADS, kvd), dseq)],
        scratch_shapes=[pltpu.VMEM((S, TILE), F32), pltpu.VMEM((S, TILE), F32),
                        pltpu.VMEM((KV_HEADS_A, 8, cw), F32),
                        pltpu.VMEM((KV_HEADS_A, HEAD_DIM + _SUM_ROWS, cw), F32),
                        pltpu.VMEM((n_pages, kvd, TILE), F32),
                        pltpu.VMEM((n_pages, kvd, TILE), F32),
                        pltpu.SemaphoreType.DMA((2,))])
    return pl.pallas_call(
        functools.partial(_dsa_mixer_kernel, topk, n_pages),
        grid_spec=grid_spec,
        out_shape=[jax.ShapeDtypeStruct((B, hq, S), BF16),
                   jax.ShapeDtypeStruct((DB, N_HEADS, kvd), F32)],
        compiler_params=_cparams(("arbitrary", "arbitrary"), _VMEM_LIMIT_MIXER),
        name="dsa_mixer",
    )(page_table, qiT, wiT, kib, qT, kh, vT, toep,
      qbd, selb.reshape(DB, 1, L), newsel.reshape(DB, 1, TILE), kn.reshape(DB, 1, kvd), vn.reshape(DB, 1, kvd),
      dec, ck, cv)


def _swa_dec_kernel(q_ref, wk_ref, wv_ref, kn_ref, vn_ref, kcol_ref, vcol_ref, dec_ref, sink_ref,
                    o_ref, nwk_ref, nwv_ref):
    lane = lax.broadcasted_iota(jnp.int32, (N_HEADS, TILE), 1)
    newest = lax.broadcasted_iota(jnp.int32, wk_ref.shape[1:], 1) == WINDOW - 1
    bias = dec_ref[0]
    b0 = dec_ref[2][:, 0:1]
    sk = sink_ref[...]
    for i in range(_SEQ_BLOCK):
        nwk_ref[i] = jnp.where(newest, kcol_ref[i], pltpu.roll(wk_ref[i], WINDOW - 1, 1))
        nwv_ref[i] = jnp.where(newest, vcol_ref[i], pltpu.roll(wv_ref[i], WINDOW - 1, 1))
        qb = q_ref[i]
        s = jnp.dot(qb, wk_ref[i].astype(BF16), preferred_element_type=F32)
        s = jnp.where(lane >= 1, s + bias, NEG)
        kn = kn_ref[i].astype(BF16).astype(F32)
        sn = jnp.sum(qb.astype(F32) * kn, axis=1, keepdims=True) + b0
        m = jnp.maximum(jnp.maximum(jnp.max(s, axis=1, keepdims=True), sn), sk)
        p = jnp.exp(s - m)
        pn = jnp.exp(sn - m)
        l = jnp.sum(p, axis=1, keepdims=True) + pn + jnp.exp(sk - m)
        out = lax.dot_general(p.astype(BF16), wv_ref[i].astype(BF16), _NT, preferred_element_type=F32) + pn * vn_ref[i]
        o_ref[i] = out / l


def _swa_dec(qbd, wk, wv, kn, vn, dec, sink):
    DB = qbd.shape[0]
    kvd = KV_HEADS_B * HEAD_DIM
    sb = _SEQ_BLOCK
    return pl.pallas_call(
        _swa_dec_kernel,
        grid=(DB // sb,),
        in_specs=[pl.BlockSpec((sb, N_HEADS, kvd), lambda s: (s, 0, 0)),
                  pl.BlockSpec((sb, kvd, WINDOW), lambda s: (s, 0, 0)),
                  pl.BlockSpec((sb, kvd, WINDOW), lambda s: (s, 0, 0)),
                  pl.BlockSpec((sb, 1, kvd), lambda s: (s, 0, 0)),
                  pl.BlockSpec((sb, 1, kvd), lambda s: (s, 0, 0)),
                  pl.BlockSpec((sb, kvd, 1), lambda s: (s, 0, 0)),
                  pl.BlockSpec((sb, kvd, 1), lambda s: (s, 0, 0)),
                  pl.BlockSpec(dec.shape, lambda s: (0, 0, 0)),
                  pl.BlockSpec((N_HEADS, 1), lambda s: (0, 0))],
        out_specs=[pl.BlockSpec((sb, N_HEADS, kvd), lambda s: (s, 0, 0)),
                   pl.BlockSpec((sb, kvd, WINDOW), lambda s: (s, 0, 0)),
                   pl.BlockSpec((sb, kvd, WINDOW), lambda s: (s, 0, 0))],
        out_shape=[jax.ShapeDtypeStruct((DB, N_HEADS, kvd), F32),
                   jax.ShapeDtypeStruct((DB, kvd, WINDOW), F32),
                   jax.ShapeDtypeStruct((DB, kvd, WINDOW), F32)],
        compiler_params=_cparams(("arbitrary",)),
        name="swa_dec",
    )(qbd, wk, wv, kn.reshape(DB, 1, kvd), vn.reshape(DB, 1, kvd), kn.reshape(DB, kvd, 1), vn.reshape(DB, kvd, 1),
      dec, sink.reshape(N_HEADS, 1))


def _block_diag_q(q, nkv):
    DB = q.shape[0]
    G = N_HEADS // nkv
    qh = (q * ATTN_SCALE).reshape(DB, N_HEADS, 1, HEAD_DIM)
    own = (np.arange(N_HEADS)[:, None] // G == np.arange(nkv)[None, :]).astype(np.float32)
    return (qh * own[None, :, :, None]).reshape(DB, N_HEADS, nkv * HEAD_DIM).astype(BF16)


def _own_group(o, nkv):
    DB = o.shape[0]
    G = N_HEADS // nkv
    o5 = o.reshape(DB, nkv, G, nkv, HEAD_DIM)
    return jnp.stack([o5[:, g, :, g, :] for g in range(nkv)], axis=1).reshape(DB, N_HEADS * HEAD_DIM)


def _pad_cols(w, mult=TILE):
    pad = (-w.shape[1]) % mult
    return jnp.pad(w, ((0, 0), (0, pad))) if pad else w


def _slots_minor(a):
    n, t, h, d = a.shape
    return a.transpose(0, 2, 3, 1).reshape(n, h * d, t)


def _slots_major(a, heads):
    n, hd, t = a.shape
    return a.reshape(n, heads, hd // heads, t).transpose(0, 3, 1, 2)


def kernel(x_prompt, x_sample, cache_k, cache_v, cache_idx_k, state_win_k, state_win_v, page_table, rel_bias_table, a_w_in, a_w_out, a_idx_ln_g, a_idx_ln_b, b_w_in, b_w_out, b_sink, ln1_g, ln1_b, ln2_g, ln2_b, mlp_w1, mlp_w2):
    B, S, D = x_prompt.shape
    DB, DS, _ = x_sample.shape
    depth = ln1_g.shape[0]
    assert depth == 2 and DS == 1 and S % TILE == 0 and DB % _SEQ_BLOCK == 0
    assert state_win_k.shape[2] == WINDOW and cache_k.shape[2] == TILE
    alpha = (2 * depth) ** 0.25
    n_pages = page_table.shape[1]
    past = n_pages * TILE
    hq = N_HEADS * HEAD_DIM
    kva = KV_HEADS_A * HEAD_DIM
    kvb = KV_HEADS_B * HEAD_DIM
    hi = IDX_HEADS * IDX_DIM

    toep, toep2, dec = _bias_tables(rel_bias_table)

    wa = a_w_in[0]
    c0, c1, c2, c3, c4 = hq, hq + kva, hq + 2 * kva, hq + 2 * kva + hi, hq + 2 * kva + hi + IDX_DIM
    wq, wk, wv, wqi, wki, wwi = wa[:, :c0], wa[:, c0:c1], wa[:, c1:c2], wa[:, c2:c3], wa[:, c3:c4], wa[:, c4:]
    a_wo = a_w_out[0].astype(BF16)

    qT, kTp, vTp, kh, vT, qiT, kiTp, kib, wiT = _prompt_proj(
        x_prompt, wq, wk, wv, KV_HEADS_A, idx=(wqi, wki, wwi, a_idx_ln_g[0], a_idx_ln_b[0]),
        q_scale=ATTN_SCALE * LOG2E)
    xs = x_sample.reshape(DB, D)
    proj = _matmul(xs, _pad_cols(wa).astype(BF16))
    q_s, k_s, v_s = proj[:, :c0], proj[:, c0:c1], proj[:, c1:c2]
    qi_s, kiraw_s, wi_s = proj[:, c2:c3], proj[:, c3:c4], proj[:, c4:c4 + IDX_HEADS]
    selb, newsel, ki_s = _dec_index(
        page_table, qi_s.reshape(DB, IDX_HEADS, IDX_DIM), wi_s.reshape(DB, IDX_HEADS, 1), kiraw_s,
        a_idx_ln_g[0], a_idx_ln_b[0], cache_idx_k[0].transpose(0, 2, 1), min(TOPK_MAX, (past + DS) // 4))
    oT, o_s = _dsa_mixer(page_table, qiT, wiT, kib, qT, kh, vT, toep2, min(TOPK_MAX, S // 4),
                         _block_diag_q(q_s, KV_HEADS_A), selb, newsel, k_s, v_s, dec,
                         _slots_minor(cache_k[0]), _slots_minor(cache_v[0]))

    attn = oT.transpose(0, 2, 1).reshape(B * S, hq)
    y_p = _residual_blocks(x_prompt.reshape(B * S, D), attn, a_wo, ln1_g[0], ln1_b[0],
                           mlp_w1, mlp_w2, 0, ln2_g[0], ln2_b[0], alpha)

    y_s = _residual_blocks(xs, _own_group(o_s, KV_HEADS_A).astype(BF16), a_wo, ln1_g[0], ln1_b[0],
                           mlp_w1, mlp_w2, 0, ln2_g[0], ln2_b[0], alpha)

    wb = b_w_in[0]
    bq, bk, bv = wb[:, :hq], wb[:, hq:hq + kvb], wb[:, hq + kvb:]
    b_wo = b_w_out[0].astype(BF16)

    qT1, k1Tp, v1Tp, kh1, vT1 = _prompt_proj(y_p.reshape(B, S, D), bq, bk, bv, KV_HEADS_B,
                                             q_scale=ATTN_SCALE * LOG2E)
    oT1 = _swa_prompt(qT1, kh1, vT1, toep2, b_sink[0])
    attn1 = oT1.transpose(0, 2, 1).reshape(B * S, hq)
    y_p = _residual_blocks(y_p, attn1, b_wo, ln1_g[1], ln1_b[1], mlp_w1, mlp_w2, 1, ln2_g[1], ln2_b[1], alpha)

    proj1 = _matmul(y_s, wb.astype(BF16))
    q1_s, k1_s, v1_s = proj1[:, :hq], proj1[:, hq:hq + kvb], proj1[:, hq + kvb:]
    win_k = _slots_minor(state_win_k[0])
    win_v = _slots_minor(state_win_v[0])
    o1_s, new_wk_s, new_wv_s = _swa_dec(_block_diag_q(q1_s, KV_HEADS_B), win_k, win_v, k1_s, v1_s, dec, b_sink[0])
    y_s = _residual_blocks(y_s, _own_group(o1_s, KV_HEADS_B).astype(BF16), b_wo, ln1_g[1], ln1_b[1],
                           mlp_w1, mlp_w2, 1, ln2_g[1], ln2_b[1], alpha)

    npg = S // TILE
    assert min(WINDOW, S) == TILE
    pages = lambda a, heads: _slots_major(a.reshape(B * npg, -1, TILE), heads).reshape(1, B, npg, TILE, heads, HEAD_DIM)
    return (y_p.reshape(B, S, D),
            y_s.reshape(DB, DS, D),
            pages(kTp, KV_HEADS_A),
            pages(vTp, KV_HEADS_A),
            kiTp.transpose(0, 1, 3, 2).reshape(1, B, npg, TILE, IDX_DIM),
            k_s.reshape(1, DB, DS, KV_HEADS_A, HEAD_DIM),
            v_s.reshape(1, DB, DS, KV_HEADS_A, HEAD_DIM),
            ki_s.reshape(1, DB, DS, IDX_DIM),
            _slots_major(k1Tp[:, npg - 1], KV_HEADS_B)[None],
            _slots_major(v1Tp[:, npg - 1], KV_HEADS_B)[None],
            _slots_major(new_wk_s, KV_HEADS_B)[None],
            _slots_major(new_wv_s, KV_HEADS_B)[None])
```

```python
import functools
import math

import numpy as np
import jax
import jax.numpy as jnp
from jax import lax
from jax.experimental import pallas as pl
from jax.experimental.pallas import tpu as pltpu

F32 = jnp.float32
BF16 = jnp.bfloat16

N_HEADS = 16
HEAD_DIM = 64
KV_HEADS_A = 4
KV_HEADS_B = 2
IDX_HEADS = 8
IDX_DIM = 64
TOPK_MAX = 256
WINDOW = 128
TILE = 128
N_BUCKETS = 32
MAX_DISTANCE = 128
LN_EPS = 1e-5
NEG = -1e30
INT_MIN = -(2 ** 31)
ATTN_SCALE = HEAD_DIM ** -0.5
LOG2E = math.log2(math.e)
IDX_SCALE = IDX_HEADS ** -0.5 * IDX_DIM ** -0.5
FAR_BUCKET = N_BUCKETS - 1
FAR_DISTANCE = 113

_NT = (((1,), (1,)), ((), ()))
_VMEM_LIMIT = 48 * 1024 * 1024
_VMEM_LIMIT_MIXER = 58 * 1024 * 1024


def _cparams(sem, vmem_limit=_VMEM_LIMIT):
    return pltpu.CompilerParams(dimension_semantics=sem, vmem_limit_bytes=vmem_limit)


def _bucket_np(dist):
    n = np.maximum(dist, 0)
    max_exact = N_BUCKETS // 2
    nf = np.maximum(n, max_exact).astype(np.float64)
    val = np.log(nf / max_exact) / math.log(MAX_DISTANCE / max_exact) * (N_BUCKETS - max_exact)
    frac = val - np.floor(val)
    interior = (n > max_exact) & (n < MAX_DISTANCE)
    assert not np.any(interior & ((frac < 1e-6) | (frac > 1 - 1e-6))), "bucket boundary too close to an integer"
    large = np.minimum(max_exact + val.astype(np.int32), N_BUCKETS - 1)
    out = np.where(n < max_exact, n, large).astype(np.int32)
    assert np.all(out[n >= FAR_DISTANCE] == FAR_BUCKET)
    return out


def _key_to_float(k):
    return lax.bitcast_convert_type(k ^ (lax.shift_right_arithmetic(k, 31) & 0x7FFFFFFF), F32)


def _kth_largest_search(count_ge, topk, shape):
    def bit_body(b, carry):
        t, cnt_t = carry
        cand = t ^ lax.shift_left(jnp.int32(1), 31 - b)
        cnt = count_ge(_key_to_float(cand))
        ok = cnt >= topk
        return jnp.where(ok, cand, t), jnp.where(ok, cnt, cnt_t)

    init = (jnp.full(shape, INT_MIN, jnp.int32), jnp.full(shape, 3e38, F32))
    t, cnt_t = lax.fori_loop(0, 32, bit_body, init)
    return _key_to_float(t), cnt_t


def _ln(y, g, b):
    mu = jnp.mean(y, axis=-1, keepdims=True)
    yc = y - mu
    var = jnp.mean(yc * yc, axis=-1, keepdims=True)
    return yc * lax.rsqrt(var + LN_EPS) * g + b


def _bias_kernel(table_ref, bk_ref, bkd_ref, toep2_ref, dec_ref):
    bk = bk_ref[...]
    bkd = bkd_ref[...]
    dec_rows = []
    for h in range(N_HEADS):
        acc = jnp.zeros(bk.shape, F32)
        accd = jnp.zeros(bkd.shape, F32)
        for b in range(N_BUCKETS):
            val = table_ref[b, h]
            acc = jnp.where(bk == b, val, acc)
            accd = jnp.where(bkd == b, val, accd)
        toep2_ref[h] = acc * LOG2E
        dec_rows.append(accd)
    for kind in range(3):
        dec_ref[kind] = jnp.concatenate([d[kind:kind + 1, :] for d in dec_rows], axis=0)


def _bias_tables(table):
    i = np.arange(TILE)[None, :]
    j = np.arange(2 * TILE)[:, None]
    bk = np.concatenate([np.full((TILE, TILE), FAR_BUCKET, np.int32), _bucket_np(i + TILE - j)], axis=0)
    bkd = np.zeros((8, TILE), np.int32)
    bkd[0] = _bucket_np(TILE - np.arange(TILE))
    bkd[1] = FAR_BUCKET
    bkd[2] = 0
    return pl.pallas_call(
        _bias_kernel,
        grid=(1,),
        in_specs=[pl.BlockSpec(memory_space=pltpu.SMEM),
                  pl.BlockSpec((3 * TILE, TILE), lambda h: (0, 0)),
                  pl.BlockSpec((8, TILE), lambda h: (0, 0))],
        out_specs=[pl.BlockSpec((N_HEADS, 3 * TILE, TILE), lambda h: (0, 0, 0)),
                   pl.BlockSpec((3, N_HEADS, TILE), lambda h: (0, 0, 0))],
        out_shape=[jax.ShapeDtypeStruct((N_HEADS, 3 * TILE, TILE), F32),
                   jax.ShapeDtypeStruct((3, N_HEADS, TILE), F32)],
        compiler_params=_cparams(("arbitrary",)),
        name="bias_tables",
    )(table, jnp.asarray(bk), jnp.asarray(bkd))


def _normalize_cols(y):
    mu = jnp.mean(y, axis=0, keepdims=True)
    yc = y - mu
    var = jnp.mean(yc * yc, axis=0, keepdims=True)
    return yc * lax.rsqrt(var + LN_EPS)


def _proj_kernel(nkv, has_idx, *refs):
    if has_idx:
        (x_ref, wqT_ref, wkT_ref, wvT_ref, wkh_ref, wqiT_ref, wkiT_ref, wki_ref, wwiT_ref, g_ref, b_ref,
         gc_ref, bc_ref, qT_ref, kTp_ref, vTp_ref, kh_ref, vT_ref, qiT_ref, kiTp_ref, kib_ref, wiT_ref) = refs
    else:
        (x_ref, wqT_ref, wkT_ref, wvT_ref, wkh_ref, qT_ref, kTp_ref, vTp_ref, kh_ref, vT_ref) = refs
    xb = x_ref[0].astype(BF16)
    npg = xb.shape[0] // TILE
    qT_ref[0] = lax.dot_general(wqT_ref[...], xb, _NT, preferred_element_type=F32).astype(BF16)
    kT = lax.dot_general(wkT_ref[...], xb, _NT, preferred_element_type=F32)
    vT = lax.dot_general(wvT_ref[...], xb, _NT, preferred_element_type=F32)
    vT_ref[0] = vT.astype(BF16)
    for j in range(npg):
        kTp_ref[0, j] = kT[:, j * TILE:(j + 1) * TILE]
        vTp_ref[0, j] = vT[:, j * TILE:(j + 1) * TILE]
    for g in range(nkv):
        kh_ref[0, g] = jnp.dot(xb, wkh_ref[g], preferred_element_type=F32).astype(BF16)
    if has_idx:
        qiT_ref[0] = lax.dot_general(wqiT_ref[...], xb, _NT, preferred_element_type=F32).astype(BF16)
        kiT = _normalize_cols(lax.dot_general(wkiT_ref[...], xb, _NT, preferred_element_type=F32))
        for j in range(npg):
            kiTp_ref[0, j] = kiT[:, j * TILE:(j + 1) * TILE] * gc_ref[...] + bc_ref[...]
        ki = _ln(jnp.dot(xb, wki_ref[...], preferred_element_type=F32), g_ref[...], b_ref[...])
        kib_ref[0] = ki.astype(BF16)
        wiT_ref[0] = lax.dot_general(wwiT_ref[...], xb, _NT, preferred_element_type=F32)


def _prompt_proj(x, wq, wk, wv, nkv, idx=None, q_scale=ATTN_SCALE, tm=1024):
    B, S, D = x.shape
    tm = min(tm, S)
    kvd = nkv * HEAD_DIM
    hq = N_HEADS * HEAD_DIM
    wqT = (wq * q_scale).T.astype(BF16)
    wkh = wk.reshape(D, nkv, HEAD_DIM).transpose(1, 0, 2).astype(BF16)
    full2 = lambda a: pl.BlockSpec(a.shape, lambda b, m: (0,) * a.ndim)
    ins = [x, wqT, wk.T.astype(BF16), wv.T.astype(BF16), wkh]
    in_specs = [pl.BlockSpec((1, tm, D), lambda b, m: (b, m, 0))] + [full2(a) for a in ins[1:]]
    out_shape = [jax.ShapeDtypeStruct((B, hq, S), BF16),
                 jax.ShapeDtypeStruct((B, S // TILE, kvd, TILE), F32),
                 jax.ShapeDtypeStruct((B, S // TILE, kvd, TILE), F32),
                 jax.ShapeDtypeStruct((B, nkv, S, HEAD_DIM), BF16),
                 jax.ShapeDtypeStruct((B, kvd, S), BF16)]
    out_specs = [pl.BlockSpec((1, hq, tm), lambda b, m: (b, 0, m)),
                 pl.BlockSpec((1, tm // TILE, kvd, TILE), lambda b, m: (b, m, 0, 0)),
                 pl.BlockSpec((1, tm // TILE, kvd, TILE), lambda b, m: (b, m, 0, 0)),
                 pl.BlockSpec((1, nkv, tm, HEAD_DIM), lambda b, m: (b, 0, m, 0)),
                 pl.BlockSpec((1, kvd, tm), lambda b, m: (b, 0, m))]
    if idx is not None:
        wqi, wki, wwi, g, bb = idx
        extra = [wqi.T.astype(BF16), wki.T.astype(BF16), wki.astype(BF16), wwi.T.astype(BF16),
                 g.reshape(1, -1), bb.reshape(1, -1),
                 jnp.broadcast_to(g.reshape(-1, 1), (IDX_DIM, TILE)), jnp.broadcast_to(bb.reshape(-1, 1), (IDX_DIM, TILE))]
        ins += extra
        in_specs += [full2(a) for a in extra]
        hi = IDX_HEADS * IDX_DIM
        out_shape += [jax.ShapeDtypeStruct((B, hi, S), BF16),
                      jax.ShapeDtypeStruct((B, S // TILE, IDX_DIM, TILE), F32),
                      jax.ShapeDtypeStruct((B, S, IDX_DIM), BF16),
                      jax.ShapeDtypeStruct((B, IDX_HEADS, S), F32)]
        out_specs += [pl.BlockSpec((1, hi, tm), lambda b, m: (b, 0, m)),
                      pl.BlockSpec((1, tm // TILE, IDX_DIM, TILE), lambda b, m: (b, m, 0, 0)),
                      pl.BlockSpec((1, tm, IDX_DIM), lambda b, m: (b, m, 0)),
                      pl.BlockSpec((1, IDX_HEADS, tm), lambda b, m: (b, 0, m))]
    return pl.pallas_call(
        functools.partial(_proj_kernel, nkv, idx is not None),
        grid=(B, S // tm),
        in_specs=in_specs,
        out_specs=out_specs,
        out_shape=out_shape,
        compiler_params=_cparams(("arbitrary", "arbitrary")),
        name="prompt_proj_idx" if idx is not None else "prompt_proj",
    )(*ins)


_KEY_UNROLL = 4
_SUM_ROWS = 16
_ATT_TILES = 4


def _dsa_prompt_block(topk, n, qiT_ref, wiT_ref, ki_ref, qT_ref, kh_ref, vT_ref, toep_ref, o_ref,
                      skey_ref, sel_ref, m_ref, acc_ref):
    ST = _KEY_UNROLL * TILE
    ns = n // _KEY_UNROLL + 1
    G = N_HEADS // KV_HEADS_A
    rows = lax.broadcasted_iota(jnp.int32, (ST, TILE), 0)
    lanes = lax.broadcasted_iota(jnp.int32, (ST, TILE), 1)
    qpos = n * TILE + lanes

    qi = qiT_ref[0]
    wi = wiT_ref[0]

    def score_body(u, carry):
        off = pl.multiple_of(u * ST, ST)
        kij = ki_ref[0, pl.ds(off, ST), :]
        acc = jnp.zeros((ST, TILE), F32)
        for h in range(0, IDX_HEADS, 2):
            qpair = jnp.concatenate([qi[IDX_DIM * h:IDX_DIM * (h + 1), :],
                                     qi[IDX_DIM * (h + 1):IDX_DIM * (h + 2), :]], axis=1)
            s = jnp.dot(kij, qpair, preferred_element_type=F32)
            acc = acc + jnp.maximum(s[:, :TILE], 0.0) * wi[h:h + 1, :]
            acc = acc + jnp.maximum(s[:, TILE:], 0.0) * wi[h + 1:h + 2, :]
        sc = acc * IDX_SCALE
        sc = jnp.where(off + rows <= qpos, sc, NEG)
        sc = jnp.where(sc == 0.0, 0.0, sc)
        skey_ref[pl.ds(off, ST), :] = sc
        return carry

    lax.fori_loop(0, ns, score_body, 0)

    def count(pred):
        def body(u, c):
            off = pl.multiple_of(u * ST, ST)
            x = jnp.where(pred(skey_ref[pl.ds(off, ST), :]), 1.0, 0.0).reshape(ST // 8, 8, TILE)
            while x.shape[0] > 1:
                half = x.shape[0] // 2
                x = x[:half] + x[half:]
            return c + x[0]
        c8 = lax.fori_loop(0, ns, body, jnp.zeros((8, TILE), F32))
        return jnp.sum(c8, axis=0, keepdims=True)

    t, cnt_t = _kth_largest_search(lambda c: count(lambda kt: kt >= c), topk, (1, TILE))
    exact_fit = jnp.max(jnp.abs(cnt_t - topk)) == 0.0

    @pl.when(exact_fit)
    def _():
        def sel_body(u, carry):
            off = pl.multiple_of(u * ST, ST)
            sel = jnp.where(skey_ref[pl.ds(off, ST), :] >= t, 1.0, 0.0)
            sel_ref[pl.ds(off, ST), :] = jnp.where(off + rows <= qpos, sel, 0.0)
            return carry

        lax.fori_loop(0, ns, sel_body, 0)

    @pl.when(jnp.logical_not(exact_fit))
    def _():
        room = topk - count(lambda kt: kt > t)
        ii = lax.broadcasted_iota(jnp.int32, (TILE, TILE), 0)
        jj = lax.broadcasted_iota(jnp.int32, (TILE, TILE), 1)
        lower = jnp.where(jj < ii, 1.0, 0.0).astype(BF16)
        ones8 = jnp.ones((8, TILE), BF16)

        def sel_body(u, carry):
            for k in range(_KEY_UNROLL):
                off = pl.multiple_of(u * ST + k * TILE, TILE)
                kt = skey_ref[pl.ds(off, TILE), :]
                eq = kt == t
                eqb = jnp.where(eq, 1.0, 0.0).astype(BF16)
                before = jnp.dot(lower, eqb, preferred_element_type=F32) + carry
                tot = jnp.dot(ones8, eqb, preferred_element_type=F32)[0:1]
                sel = jnp.where(kt > t, 1.0, jnp.where(eq, jnp.where(before < room, 1.0, 0.0), 0.0))
                sel_ref[pl.ds(off, TILE), :] = jnp.where(off + ii <= n * TILE + jj, sel, 0.0)
                carry = carry + tot
            return carry

        lax.fori_loop(0, ns, sel_body, jnp.zeros((1, TILE), F32))

    q = qT_ref[0]
    qgs = [jnp.concatenate([q[(g * G + r) * HEAD_DIM:(g * G + r + 1) * HEAD_DIM, :] for r in range(G)], axis=1)
           for g in range(KV_HEADS_A)]
    m_ref[...] = jnp.full(m_ref.shape, NEG, F32)
    acc_ref[...] = jnp.zeros(acc_ref.shape, F32)

    AT = _ATT_TILES * TILE
    ones_rows = jnp.ones((_SUM_ROWS, AT), BF16)

    def att_body(u, carry):
        off = pl.multiple_of(u * AT, AT)
        tiles = [u * _ATT_TILES + k for k in range(_ATT_TILES)]
        roffs = [pl.multiple_of(jnp.where(j == n, 2 * TILE, jnp.where(j == n - 1, TILE, 0)), TILE) for j in tiles]
        selv = sel_ref[pl.ds(off, AT), :] > 0.5
        ss = [jnp.dot(kh_ref[0, g, pl.ds(off, AT), :], qgs[g], preferred_element_type=F32)
              for g in range(KV_HEADS_A)]
        for g in range(KV_HEADS_A):
            s = ss[g]
            s = jnp.concatenate(
                [jnp.where(selv,
                           s[:, r * TILE:(r + 1) * TILE]
                           + jnp.concatenate([toep_ref[g * G + r, pl.ds(ro, TILE), :] for ro in roffs], axis=0),
                           NEG)
                 for r in range(G)], axis=1)
            m = m_ref[g]
            m_new = jnp.maximum(m, jnp.max(s, axis=0, keepdims=True))
            alpha = jnp.exp2(m - m_new)
            p = jnp.exp2(s - m_new[0:1])
            m_ref[g] = m_new
            vt = jnp.concatenate([vT_ref[0, g * HEAD_DIM:(g + 1) * HEAD_DIM, pl.ds(off, AT)], ones_rows], axis=0)
            acc_ref[g] = acc_ref[g] * alpha[0:1] + jnp.dot(vt, p.astype(BF16), preferred_element_type=F32)
        return carry

    lax.fori_loop(0, n // _ATT_TILES + 1, att_body, 0)
    for g in range(KV_HEADS_A):
        out = acc_ref[g, 0:HEAD_DIM, :] / acc_ref[g, HEAD_DIM:HEAD_DIM + 1, :]
        for r in range(G):
            h = g * G + r
            o_ref[0, h * HEAD_DIM:(h + 1) * HEAD_DIM, :] = out[:, r * TILE:(r + 1) * TILE].astype(o_ref.dtype)


def _swa_prompt_kernel(sink_ref, qT_ref, khp_ref, khc_ref, vTp_ref, vTc_ref, toep_ref, o_ref):
    n = pl.program_id(1)
    G = N_HEADS // KV_HEADS_B
    rows = lax.broadcasted_iota(jnp.int32, (2 * TILE, TILE), 0)
    lanes = lax.broadcasted_iota(jnp.int32, (2 * TILE, TILE), 1)
    dj = rows - lanes
    inwin = jnp.where(dj >= TILE - WINDOW + 1, jnp.where(dj <= TILE, 1.0, 0.0), 0.0)
    inwin = jnp.where(rows >= TILE, inwin, jnp.where(n > 0, inwin, 0.0))
    maskb = jnp.concatenate([inwin] * G, axis=1) > 0.5
    q = qT_ref[0]
    ones_rows = jnp.ones((_SUM_ROWS, 2 * TILE), BF16)
    for g in range(KV_HEADS_B):
        heads = [g * G + r for r in range(G)]
        qg = jnp.concatenate([q[h * HEAD_DIM:(h + 1) * HEAD_DIM, :] for h in heads], axis=1)
        kk = jnp.concatenate([khp_ref[0, g], khc_ref[0, g]], axis=0)
        s = jnp.dot(kk, qg, preferred_element_type=F32)
        bias = jnp.concatenate([toep_ref[h, TILE:3 * TILE, :] for h in heads], axis=1)
        s = jnp.where(maskb, s + bias, NEG)
        sink = jnp.concatenate([jnp.full((1, TILE), sink_ref[h] * LOG2E, F32) for h in heads], axis=1)
        m = jnp.maximum(jnp.max(s, axis=0, keepdims=True), sink)
        p = jnp.exp2(s - m)
        vv = jnp.concatenate([vTp_ref[0, g * HEAD_DIM:(g + 1) * HEAD_DIM, :],
                              vTc_ref[0, g * HEAD_DIM:(g + 1) * HEAD_DIM, :]], axis=1)
        pv = jnp.dot(jnp.concatenate([vv, ones_rows], axis=0), p.astype(BF16), preferred_element_type=F32)
        out = pv[0:HEAD_DIM, :] / (pv[HEAD_DIM:HEAD_DIM + 1, :] + jnp.exp2(sink - m))
        for r, h in enumerate(heads):
            o_ref[0, h * HEAD_DIM:(h + 1) * HEAD_DIM, :] = out[:, r * TILE:(r + 1) * TILE].astype(o_ref.dtype)


def _swa_prompt(qT, kh, vT, toep, sink):
    B, hq, S = qT.shape
    nb = S // TILE
    prev = lambda n: jnp.maximum(n - 1, 0)
    return pl.pallas_call(
        _swa_prompt_kernel,
        grid=(B, nb),
        in_specs=[pl.BlockSpec(memory_space=pltpu.SMEM),
                  pl.BlockSpec((1, hq, TILE), lambda b, n: (b, 0, n)),
                  pl.BlockSpec((1, KV_HEADS_B, TILE, HEAD_DIM), lambda b, n: (b, 0, prev(n), 0)),
                  pl.BlockSpec((1, KV_HEADS_B, TILE, HEAD_DIM), lambda b, n: (b, 0, n, 0)),
                  pl.BlockSpec((1, KV_HEADS_B * HEAD_DIM, TILE), lambda b, n: (b, 0, prev(n))),
                  pl.BlockSpec((1, KV_HEADS_B * HEAD_DIM, TILE), lambda b, n: (b, 0, n)),
                  pl.BlockSpec(toep.shape, lambda b, n: (0, 0, 0))],
        out_specs=pl.BlockSpec((1, hq, TILE), lambda b, n: (b, 0, n)),
        out_shape=jax.ShapeDtypeStruct((B, hq, S), BF16),
        compiler_params=_cparams(("arbitrary", "arbitrary")),
        name="swa_prompt",
    )(sink, qT, kh, kh, vT, vT, toep)


def _residual_kernel(alpha, x_ref, a_ref, wo_ref, g1_ref, b1_ref, w1_ref, w2_ref, g2_ref, b2_ref, o_ref,
                     y_ref, acc_ref):
    f = pl.program_id(1)

    @pl.when(f == 0)
    def _():
        y = alpha * x_ref[...] + jnp.dot(a_ref[...], wo_ref[...], preferred_element_type=F32)
        y_ref[...] = _ln(y, g1_ref[...], b1_ref[...])
        acc_ref[...] = jnp.zeros_like(acc_ref)

    h = jnp.maximum(jnp.dot(y_ref[...].astype(BF16), w1_ref[0].astype(BF16), preferred_element_type=F32), 0.0)
    acc_ref[...] += jnp.dot((h * h).astype(BF16), w2_ref[0].astype(BF16), preferred_element_type=F32)

    @pl.when(f == pl.num_programs(1) - 1)
    def _():
        o_ref[...] = _ln(alpha * y_ref[...] + acc_ref[...], g2_ref[...], b2_ref[...])


def _residual_blocks(x, a, wo, g1, b1, w1, w2, layer, g2, b2, alpha, tm=1024, tf=512):
    M, D = x.shape
    FF = w1.shape[2]
    tm = min(tm, M)
    row = lambda v: v.reshape(1, D)
    vec = pl.BlockSpec((1, D), lambda m, f: (0, 0))
    return pl.pallas_call(
        functools.partial(_residual_kernel, alpha),
        grid=(M // tm, FF // tf),
        in_specs=[pl.BlockSpec((tm, D), lambda m, f: (m, 0)),
                  pl.BlockSpec((tm, a.shape[1]), lambda m, f: (m, 0)),
                  pl.BlockSpec(wo.shape, lambda m, f: (0, 0)),
                  vec, vec,
                  pl.BlockSpec((1, D, tf), lambda m, f: (layer, 0, f)),
                  pl.BlockSpec((1, tf, D), lambda m, f: (layer, f, 0)),
                  vec, vec],
        out_specs=pl.BlockSpec((tm, D), lambda m, f: (m, 0)),
        out_shape=jax.ShapeDtypeStruct((M, D), F32),
        scratch_shapes=[pltpu.VMEM((tm, D), F32), pltpu.VMEM((tm, D), F32)],
        compiler_params=_cparams(("arbitrary", "arbitrary")),
        name="residual_blocks",
    )(x, a, wo, row(g1), row(b1), w1, w2, row(g2), row(b2))


def _matmul_kernel(x_ref, w_ref, o_ref):
    o_ref[...] = jnp.dot(x_ref[...].astype(BF16), w_ref[...], preferred_element_type=F32)


def _matmul(x, w):
    M, K = x.shape
    N = w.shape[1]
    return pl.pallas_call(
        _matmul_kernel,
        grid=(1,),
        in_specs=[pl.BlockSpec((M, K), lambda i: (0, 0)), pl.BlockSpec((K, N), lambda i: (0, 0))],
        out_specs=pl.BlockSpec((M, N), lambda i: (0, 0)),
        out_shape=jax.ShapeDtypeStruct((M, N), F32),
        compiler_params=_cparams(("arbitrary",)),
        name="sample_proj",
    )(x, w)


_SEQ_BLOCK = 16
_IDX_SLOTS = 4


def _dec_index_kernel(topk, n_pages, pt_ref, qi_ref, wi_ref, kiraw_ref, g_ref, b_ref, cidx_ref,
                      selb_ref, newsel_ref, kiln_ref, buf, sem, sc_ref, snew_ref):
    step = pl.program_id(0)
    nseq = pl.num_programs(0) * _SEQ_BLOCK
    L = n_pages * TILE

    def page_copy(seq, slot, p):
        return pltpu.make_async_copy(cidx_ref.at[pt_ref[seq, p]], buf.at[slot, p], sem.at[slot])

    def start(seq, slot):
        for p in range(n_pages):
            page_copy(seq, slot, p).start()

    def wait(seq, slot):
        for p in range(n_pages):
            page_copy(seq, slot, p).wait()

    ahead = _IDX_SLOTS - 1

    @pl.when(step == 0)
    def _():
        for s0 in range(ahead):
            start(s0, s0)

    kiln_ref[...] = _ln(kiraw_ref[...], g_ref[...], b_ref[...])

    def seq_body(i, carry):
        seq = step * _SEQ_BLOCK + i
        slot = i % _IDX_SLOTS

        @pl.when(seq + ahead < nseq)
        def _():
            start(seq + ahead, (i + ahead) % _IDX_SLOTS)

        wait(seq, slot)
        xk = jnp.concatenate([buf[slot, p] for p in range(n_pages)], axis=1).astype(BF16)
        qib = qi_ref[i].astype(BF16)
        s = jnp.dot(qib, xk, preferred_element_type=F32)
        w = wi_ref[i]
        row = jnp.sum(jnp.maximum(s, 0.0) * w, axis=0, keepdims=True) * IDX_SCALE
        sc_ref[pl.ds(i, 1), :] = jnp.where(row == 0.0, 0.0, row)
        kn = kiln_ref[pl.ds(i, 1), :].astype(BF16).astype(F32)
        sn = jnp.sum(qib.astype(F32) * kn, axis=1, keepdims=True)
        snew = jnp.sum(jnp.maximum(sn, 0.0) * w, axis=0, keepdims=True) * IDX_SCALE
        snew_ref[pl.ds(i, 1), :] = jnp.broadcast_to(jnp.where(snew == 0.0, 0.0, snew), (1, TILE))
        return carry

    lax.fori_loop(0, _SEQ_BLOCK, seq_body, 0)
    keys = sc_ref[...]
    knew = snew_ref[:, 0:1]

    def count(pred):
        return (jnp.sum(jnp.where(pred(keys), 1.0, 0.0), axis=1, keepdims=True)
                + jnp.where(pred(knew), 1.0, 0.0))

    t, cnt_t = _kth_largest_search(lambda c: count(lambda k: k >= c), topk, (_SEQ_BLOCK, 1))
    exact_fit = jnp.max(jnp.abs(cnt_t - topk)) == 0.0

    @pl.when(exact_fit)
    def _():
        selb_ref[...] = jnp.where(keys >= t, 1.0, 0.0)
        newsel_ref[...] = jnp.broadcast_to(jnp.where(knew >= t, 1.0, 0.0), (_SEQ_BLOCK, TILE))

    @pl.when(jnp.logical_not(exact_fit))
    def _():
        room = topk - count(lambda k: k > t)
        ii = lax.broadcasted_iota(jnp.int32, (TILE, 2 * TILE), 0)
        jj = lax.broadcasted_iota(jnp.int32, (TILE, 2 * TILE), 1)
        pref = jnp.where(jj >= TILE, 1.0, jnp.where(ii < jj, 1.0, 0.0)).astype(BF16)
        carry = jnp.zeros((_SEQ_BLOCK, TILE), F32)
        for jt in range(n_pages):
            kt = keys[:, jt * TILE:(jt + 1) * TILE]
            eq = kt == t
            res = jnp.dot(jnp.where(eq, 1.0, 0.0).astype(BF16), pref, preferred_element_type=F32)
            take = jnp.where(res[:, :TILE] + carry < room, 1.0, 0.0)
            selb_ref[:, jt * TILE:(jt + 1) * TILE] = jnp.where(kt > t, 1.0, jnp.where(eq, take, 0.0))
            carry = carry + res[:, TILE:]
        newsel = jnp.where(knew > t, 1.0, jnp.where(knew == t, jnp.where(carry[:, :1] < room, 1.0, 0.0), 0.0))
        newsel_ref[...] = jnp.broadcast_to(newsel, (_SEQ_BLOCK, TILE))


def _dec_index(page_table, qi, wi, kiraw, g, b, cidx, topk):
    DB, n_pages = page_table.shape
    L = n_pages * TILE
    sb = _SEQ_BLOCK
    grid_spec = pltpu.PrefetchScalarGridSpec(
        num_scalar_prefetch=1,
        grid=(DB // sb,),
        in_specs=[pl.BlockSpec((sb, IDX_HEADS, IDX_DIM), lambda s, pt: (s, 0, 0)),
                  pl.BlockSpec((sb, IDX_HEADS, 1), lambda s, pt: (s, 0, 0)),
                  pl.BlockSpec((sb, IDX_DIM), lambda s, pt: (s, 0)),
                  pl.BlockSpec((1, IDX_DIM), lambda s, pt: (0, 0)),
                  pl.BlockSpec((1, IDX_DIM), lambda s, pt: (0, 0)),
                  pl.BlockSpec(memory_space=pl.ANY)],
        out_specs=[pl.BlockSpec((sb, L), lambda s, pt: (s, 0)),
                   pl.BlockSpec((sb, TILE), lambda s, pt: (s, 0)),
                   pl.BlockSpec((sb, IDX_DIM), lambda s, pt: (s, 0))],
        scratch_shapes=[pltpu.VMEM((_IDX_SLOTS, n_pages, IDX_DIM, TILE), F32),
                        pltpu.SemaphoreType.DMA((_IDX_SLOTS,)),
                        pltpu.VMEM((sb, L), F32),
                        pltpu.VMEM((sb, TILE), F32)])
    return pl.pallas_call(
        functools.partial(_dec_index_kernel, topk, n_pages),
        grid_spec=grid_spec,
        out_shape=[jax.ShapeDtypeStruct((DB, L), F32),
                   jax.ShapeDtypeStruct((DB, TILE), F32),
                   jax.ShapeDtypeStruct((DB, IDX_DIM), F32)],
        compiler_params=_cparams(("arbitrary",)),
        name="dec_index",
    )(page_table, qi, wi, kiraw, g.reshape(1, -1), b.reshape(1, -1), cidx)


_PAGE_CHUNK = 8


def _decode_attention(n_pages, q_ref, selb_ref, newsel_ref, kn_ref, vn_ref, dec_ref, kbuf, vbuf, o_ref):
    nch = n_pages // _PAGE_CHUNK
    CL = _PAGE_CHUNK * TILE
    qb = q_ref[0]
    far = dec_ref[1]
    near = dec_ref[0]
    pages = lambda buf, c: jnp.concatenate(
        [buf[c * _PAGE_CHUNK + p] for p in range(_PAGE_CHUNK)], axis=1).astype(BF16)
    ss = []
    for c in range(nch):
        s = jnp.dot(qb, pages(kbuf, c), preferred_element_type=F32)
        bias = jnp.concatenate([far] * (_PAGE_CHUNK - 1) + [near if c + 1 == nch else far], axis=1)
        sel = selb_ref[0, :, c * CL:(c + 1) * CL] > 0.5
        ss.append(jnp.where(sel, s + bias, NEG))
    kn = kn_ref[0].astype(BF16).astype(F32)
    sn = jnp.sum(qb.astype(F32) * kn, axis=1, keepdims=True) + dec_ref[2][:, 0:1]
    sn = jnp.where(newsel_ref[0, :, 0:1] > 0.5, sn, NEG)
    m = jnp.maximum(jnp.max(functools.reduce(jnp.maximum, ss), axis=1, keepdims=True), sn)
    pn = jnp.exp(sn - m)
    ps = [jnp.exp(s - m) for s in ss]
    l = jnp.sum(functools.reduce(jnp.add, ps), axis=1, keepdims=True) + pn
    acc = pn * vn_ref[0]
    for c in range(nch):
        acc = acc + lax.dot_general(ps[c].astype(BF16), pages(vbuf, c), _NT, preferred_element_type=F32)
    o_ref[0] = acc / l


def _dsa_mixer_kernel(topk, n_pages, pt_ref,
                      qiT_ref, wiT_ref, ki_ref, qT_ref, kh_ref, vT_ref, toep_ref,
                      qd_ref, selb_ref, newsel_ref, kn_ref, vn_ref, dec_ref, ck_ref, cv_ref,
                      o_ref, od_ref, skey_ref, sel_ref, m_ref, acc_ref, kbuf, vbuf, sem):
    n = pl.program_id(1)
    nq = pl.num_programs(1)
    seq = pl.program_id(0) * nq + n
    nseq = pl.num_programs(0) * nq

    def copies(s_):
        out = []
        for p in range(n_pages):
            phys = pt_ref[s_, p]
            out.append(pltpu.make_async_copy(ck_ref.at[phys], kbuf.at[p], sem.at[0]))
            out.append(pltpu.make_async_copy(cv_ref.at[phys], vbuf.at[p], sem.at[1]))
        return out

    @pl.when(seq == 0)
    def _():
        for cp in copies(0):
            cp.start()

    for cp in copies(seq):
        cp.wait()
    _decode_attention(n_pages, qd_ref, selb_ref, newsel_ref, kn_ref, vn_ref, dec_ref, kbuf, vbuf, od_ref)

    @pl.when(seq + 1 < nseq)
    def _():
        for cp in copies(seq + 1):
            cp.start()

    _dsa_prompt_block(topk, n, qiT_ref, wiT_ref, ki_ref, qT_ref, kh_ref, vT_ref, toep_ref, o_ref,
                      skey_ref, sel_ref, m_ref, acc_ref)


def _dsa_mixer(page_table, qiT, wiT, kib, qT, kh, vT, toep, topk, qbd, selb, newsel, kn, vn, dec, ck, cv):
    B, hq, S = qT.shape
    nq = S // TILE
    DB, n_pages = page_table.shape
    L = n_pages * TILE
    kvd = KV_HEADS_A * HEAD_DIM
    assert nq % _KEY_UNROLL == 0 and n_pages % _PAGE_CHUNK == 0
    assert DB == B * nq
    cw = (N_HEADS // KV_HEADS_A) * TILE
    dseq = lambda b, n, pt: (b * nq + n, 0, 0)
    grid_spec = pltpu.PrefetchScalarGridSpec(
        num_scalar_prefetch=1,
        grid=(B, nq),
        in_specs=[pl.BlockSpec((1, qiT.shape[1], TILE), lambda b, n, pt: (b, 0, n)),
                  pl.BlockSpec((1, IDX_HEADS, TILE), lambda b, n, pt: (b, 0, n)),
                  pl.BlockSpec((1, S, IDX_DIM), lambda b, n, pt: (b, 0, 0)),
                  pl.BlockSpec((1, hq, TILE), lambda b, n, pt: (b, 0, n)),
                  pl.BlockSpec((1, KV_HEADS_A, S, HEAD_DIM), lambda b, n, pt: (b, 0, 0, 0)),
                  pl.BlockSpec((1, kvd, S), lambda b, n, pt: (b, 0, 0)),
                  pl.BlockSpec(toep.shape, lambda b, n, pt: (0, 0, 0)),
                  pl.BlockSpec((1, N_HEADS, kvd), dseq),
                  pl.BlockSpec((1, 1, L), dseq),
                  pl.BlockSpec((1, 1, TILE), dseq),
                  pl.BlockSpec((1, 1, kvd), dseq),
                  pl.BlockSpec((1, 1, kvd), dseq),
                  pl.BlockSpec(dec.shape, lambda b, n, pt: (0, 0, 0)),
                  pl.BlockSpec(memory_space=pl.ANY),
                  pl.BlockSpec(memory_space=pl.ANY)],
        out_specs=[pl.BlockSpec((1, hq, TILE), lambda b, n, pt: (b, 0, n)),
                   pl.BlockSpec((1, N_HEADS, kvd), dseq)],
        scratch_shapes=[pltpu.VMEM((S, TILE), F32), pltpu.VMEM((S, TILE), F32),
                        pltpu.VMEM((KV_HEADS_A, 8, cw), F32),
                        pltpu.VMEM((KV_HEADS_A, HEAD_DIM + _SUM_ROWS, cw), F32),
                        pltpu.VMEM((n_pages, kvd, TILE), F32),
                        pltpu.VMEM((n_pages, kvd, TILE), F32),
                        pltpu.SemaphoreType.DMA((2,))])
    return pl.pallas_call(
        functools.partial(_dsa_mixer_kernel, topk, n_pages),
        grid_spec=grid_spec,
        out_shape=[jax.ShapeDtypeStruct((B, hq, S), BF16),
                   jax.ShapeDtypeStruct((DB, N_HEADS, kvd), F32)],
        compiler_params=_cparams(("arbitrary", "arbitrary"), _VMEM_LIMIT_MIXER),
        name="dsa_mixer",
    )(page_table, qiT, wiT, kib, qT, kh, vT, toep,
      qbd, selb.reshape(DB, 1, L), newsel.reshape(DB, 1, TILE), kn.reshape(DB, 1, kvd), vn.reshape(DB, 1, kvd),
      dec, ck, cv)


def _swa_dec_kernel(q_ref, wk_ref, wv_ref, kn_ref, vn_ref, kcol_ref, vcol_ref, dec_ref, sink_ref,
                    o_ref, nwk_ref, nwv_ref):
    lane = lax.broadcasted_iota(jnp.int32, (N_HEADS, TILE), 1)
    newest = lax.broadcasted_iota(jnp.int32, wk_ref.shape[1:], 1) == WINDOW - 1
    bias = dec_ref[0]
    b0 = dec_ref[2][:, 0:1]
    sk = sink_ref[...]
    for i in range(_SEQ_BLOCK):
        nwk_ref[i] = jnp.where(newest, kcol_ref[i], pltpu.roll(wk_ref[i], WINDOW - 1, 1))
        nwv_ref[i] = jnp.where(newest, vcol_ref[i], pltpu.roll(wv_ref[i], WINDOW - 1, 1))
        qb = q_ref[i]
        s = jnp.dot(qb, wk_ref[i].astype(BF16), preferred_element_type=F32)
        s = jnp.where(lane >= 1, s + bias, NEG)
        kn = kn_ref[i].astype(BF16).astype(F32)
        sn = jnp.sum(qb.astype(F32) * kn, axis=1, keepdims=True) + b0
        m = jnp.maximum(jnp.maximum(jnp.max(s, axis=1, keepdims=True), sn), sk)
        p = jnp.exp(s - m)
        pn = jnp.exp(sn - m)
        l = jnp.sum(p, axis=1, keepdims=True) + pn + jnp.exp(sk - m)
        out = lax.dot_general(p.astype(BF16), wv_ref[i].astype(BF16), _NT, preferred_element_type=F32) + pn * vn_ref[i]
        o_ref[i] = out / l


def _swa_dec(qbd, wk, wv, kn, vn, dec, sink):
    DB = qbd.shape[0]
    kvd = KV_HEADS_B * HEAD_DIM
    sb = _SEQ_BLOCK
    return pl.pallas_call(
        _swa_dec_kernel,
        grid=(DB // sb,),
        in_specs=[pl.BlockSpec((sb, N_HEADS, kvd), lambda s: (s, 0, 0)),
                  pl.BlockSpec((sb, kvd, WINDOW), lambda s: (s, 0, 0)),
                  pl.BlockSpec((sb, kvd, WINDOW), lambda s: (s, 0, 0)),
                  pl.BlockSpec((sb, 1, kvd), lambda s: (s, 0, 0)),
                  pl.BlockSpec((sb, 1, kvd), lambda s: (s, 0, 0)),
                  pl.BlockSpec((sb, kvd, 1), lambda s: (s, 0, 0)),
                  pl.BlockSpec((sb, kvd, 1), lambda s: (s, 0, 0)),
                  pl.BlockSpec(dec.shape, lambda s: (0, 0, 0)),
                  pl.BlockSpec((N_HEADS, 1), lambda s: (0, 0))],
        out_specs=[pl.BlockSpec((sb, N_HEADS, kvd), lambda s: (s, 0, 0)),
                   pl.BlockSpec((sb, kvd, WINDOW), lambda s: (s, 0, 0)),
                   pl.BlockSpec((sb, kvd, WINDOW), lambda s: (s, 0, 0))],
        out_shape=[jax.ShapeDtypeStruct((DB, N_HEADS, kvd), F32),
                   jax.ShapeDtypeStruct((DB, kvd, WINDOW), F32),
                   jax.ShapeDtypeStruct((DB, kvd, WINDOW), F32)],
        compiler_params=_cparams(("arbitrary",)),
        name="swa_dec",
    )(qbd, wk, wv, kn.reshape(DB, 1, kvd), vn.reshape(DB, 1, kvd), kn.reshape(DB, kvd, 1), vn.reshape(DB, kvd, 1),
      dec, sink.reshape(N_HEADS, 1))


def _block_diag_q(q, nkv):
    DB = q.shape[0]
    G = N_HEADS // nkv
    qh = (q * ATTN_SCALE).reshape(DB, N_HEADS, 1, HEAD_DIM)
    own = (np.arange(N_HEADS)[:, None] // G == np.arange(nkv)[None, :]).astype(np.float32)
    return (qh * own[None, :, :, None]).reshape(DB, N_HEADS, nkv * HEAD_DIM).astype(BF16)


def _own_group(o, nkv):
    DB = o.shape[0]
    G = N_HEADS // nkv
    o5 = o.reshape(DB, nkv, G, nkv, HEAD_DIM)
    return jnp.stack([o5[:, g, :, g, :] for g in range(nkv)], axis=1).reshape(DB, N_HEADS * HEAD_DIM)


def _pad_cols(w, mult=TILE):
    pad = (-w.shape[1]) % mult
    return jnp.pad(w, ((0, 0), (0, pad))) if pad else w


def _slots_minor(a):
    n, t, h, d = a.shape
    return a.transpose(0, 2, 3, 1).reshape(n, h * d, t)


def _slots_major(a, heads):
    n, hd, t = a.shape
    return a.reshape(n, heads, hd // heads, t).transpose(0, 3, 1, 2)


def kernel(x_prompt, x_sample, cache_k, cache_v, cache_idx_k, state_win_k, state_win_v, page_table, rel_bias_table, a_w_in, a_w_out, a_idx_ln_g, a_idx_ln_b, b_w_in, b_w_out, b_sink, ln1_g, ln1_b, ln2_g, ln2_b, mlp_w1, mlp_w2):
    B, S, D = x_prompt.shape
    DB, DS, _ = x_sample.shape
    depth = ln1_g.shape[0]
    assert depth == 2 and DS == 1 and S % TILE == 0 and DB % _SEQ_BLOCK == 0
    assert state_win_k.shape[2] == WINDOW and cache_k.shape[2] == TILE
    alpha = (2 * depth) ** 0.25
    n_pages = page_table.shape[1]
    past = n_pages * TILE
    hq = N_HEADS * HEAD_DIM
    kva = KV_HEADS_A * HEAD_DIM
    kvb = KV_HEADS_B * HEAD_DIM
    hi = IDX_HEADS * IDX_DIM

    toep2, dec = _bias_tables(rel_bias_table)

    wa = a_w_in[0]
    c0, c1, c2, c3, c4 = hq, hq + kva, hq + 2 * kva, hq + 2 * kva + hi, hq + 2 * kva + hi + IDX_DIM
    wq, wk, wv, wqi, wki, wwi = wa[:, :c0], wa[:, c0:c1], wa[:, c1:c2], wa[:, c2:c3], wa[:, c3:c4], wa[:, c4:]
    a_wo = a_w_out[0].astype(BF16)

    qT, kTp, vTp, kh, vT, qiT, kiTp, kib, wiT = _prompt_proj(
        x_prompt, wq, wk, wv, KV_HEADS_A, idx=(wqi, wki, wwi, a_idx_ln_g[0], a_idx_ln_b[0]),
        q_scale=ATTN_SCALE * LOG2E)
    xs = x_sample.reshape(DB, D)
    proj = _matmul(xs, _pad_cols(wa).astype(BF16))
    q_s, k_s, v_s = proj[:, :c0], proj[:, c0:c1], proj[:, c1:c2]
    qi_s, kiraw_s, wi_s = proj[:, c2:c3], proj[:, c3:c4], proj[:, c4:c4 + IDX_HEADS]
    selb, newsel, ki_s = _dec_index(
        page_table, qi_s.reshape(DB, IDX_HEADS, IDX_DIM), wi_s.reshape(DB, IDX_HEADS, 1), kiraw_s,
        a_idx_ln_g[0], a_idx_ln_b[0], cache_idx_k[0].transpose(0, 2, 1), min(TOPK_MAX, (past + DS) // 4))
    oT, o_s = _dsa_mixer(page_table, qiT, wiT, kib, qT, kh, vT, toep2, min(TOPK_MAX, S // 4),
                         _block_diag_q(q_s, KV_HEADS_A), selb, newsel, k_s, v_s, dec,
                         _slots_minor(cache_k[0]), _slots_minor(cache_v[0]))

    attn = oT.transpose(0, 2, 1).reshape(B * S, hq)
    y_p = _residual_blocks(x_prompt.reshape(B * S, D), attn, a_wo, ln1_g[0], ln1_b[0],
                           mlp_w1, mlp_w2, 0, ln2_g[0], ln2_b[0], alpha)

    y_s = _residual_blocks(xs, _own_group(o_s, KV_HEADS_A).astype(BF16), a_wo, ln1_g[0], ln1_b[0],
                           mlp_w1, mlp_w2, 0, ln2_g[0], ln2_b[0], alpha)

    wb = b_w_in[0]
    bq, bk, bv = wb[:, :hq], wb[:, hq:hq + kvb], wb[:, hq + kvb:]
    b_wo = b_w_out[0].astype(BF16)

    qT1, k1Tp, v1Tp, kh1, vT1 = _prompt_proj(y_p.reshape(B, S, D), bq, bk, bv, KV_HEADS_B,
                                             q_scale=ATTN_SCALE * LOG2E)
    oT1 = _swa_prompt(qT1, kh1, vT1, toep2, b_sink[0])
    attn1 = oT1.transpose(0, 2, 1).reshape(B * S, hq)
    y_p = _residual_blocks(y_p, attn1, b_wo, ln1_g[1], ln1_b[1], mlp_w1, mlp_w2, 1, ln2_g[1], ln2_b[1], alpha)

    proj1 = _matmul(y_s, wb.astype(BF16))
    q1_s, k1_s, v1_s = proj1[:, :hq], proj1[:, hq:hq + kvb], proj1[:, hq + kvb:]
    win_k = _slots_minor(state_win_k[0])
    win_v = _slots_minor(state_win_v[0])
    o1_s, new_wk_s, new_wv_s = _swa_dec(_block_diag_q(q1_s, KV_HEADS_B), win_k, win_v, k1_s, v1_s, dec, b_sink[0])
    y_s = _residual_blocks(y_s, _own_group(o1_s, KV_HEADS_B).astype(BF16), b_wo, ln1_g[1], ln1_b[1],
                           mlp_w1, mlp_w2, 1, ln2_g[1], ln2_b[1], alpha)

    npg = S // TILE
    assert min(WINDOW, S) == TILE
    pages = lambda a, heads: _slots_major(a.reshape(B * npg, -1, TILE), heads).reshape(1, B, npg, TILE, heads, HEAD_DIM)
    return (y_p.reshape(B, S, D),
            y_s.reshape(DB, DS, D),
            pages(kTp, KV_HEADS_A),
            pages(vTp, KV_HEADS_A),
            kiTp.transpose(0, 1, 3, 2).reshape(1, B, npg, TILE, IDX_DIM),
            k_s.reshape(1, DB, DS, KV_HEADS_A, HEAD_DIM),
            v_s.reshape(1, DB, DS, KV_HEADS_A, HEAD_DIM),
            ki_s.reshape(1, DB, DS, IDX_DIM),
            _slots_major(k1Tp[:, npg - 1], KV_HEADS_B)[None],
            _slots_major(v1Tp[:, npg - 1], KV_HEADS_B)[None],
            _slots_major(new_wk_s, KV_HEADS_B)[None],
            _slots_major(new_wv_s, KV_HEADS_B)[None])
```

```python
import functools
import math

import numpy as np
import jax
import jax.numpy as jnp
from jax import lax
from jax.experimental import pallas as pl
from jax.experimental.pallas import tpu as pltpu

F32 = jnp.float32
BF16 = jnp.bfloat16

N_HEADS = 16
HEAD_DIM = 64
KV_HEADS_A = 4
KV_HEADS_B = 2
IDX_HEADS = 8
IDX_DIM = 64
TOPK_MAX = 256
WINDOW = 128
TILE = 128
N_BUCKETS = 32
MAX_DISTANCE = 128
LN_EPS = 1e-5
NEG = -1e30
INT_MIN = -(2 ** 31)
ATTN_SCALE = HEAD_DIM ** -0.5
LOG2E = math.log2(math.e)
IDX_SCALE = IDX_HEADS ** -0.5 * IDX_DIM ** -0.5
FAR_BUCKET = N_BUCKETS - 1
FAR_DISTANCE = 113

_NT = (((1,), (1,)), ((), ()))
_VMEM_LIMIT = 48 * 1024 * 1024
_VMEM_LIMIT_MIXER = 58 * 1024 * 1024


def _cparams(sem, vmem_limit=_VMEM_LIMIT):
    return pltpu.CompilerParams(dimension_semantics=sem, vmem_limit_bytes=vmem_limit)


def _bucket_np(dist):
    n = np.maximum(dist, 0)
    max_exact = N_BUCKETS // 2
    nf = np.maximum(n, max_exact).astype(np.float64)
    val = np.log(nf / max_exact) / math.log(MAX_DISTANCE / max_exact) * (N_BUCKETS - max_exact)
    frac = val - np.floor(val)
    interior = (n > max_exact) & (n < MAX_DISTANCE)
    assert not np.any(interior & ((frac < 1e-6) | (frac > 1 - 1e-6))), "bucket boundary too close to an integer"
    large = np.minimum(max_exact + val.astype(np.int32), N_BUCKETS - 1)
    out = np.where(n < max_exact, n, large).astype(np.int32)
    assert np.all(out[n >= FAR_DISTANCE] == FAR_BUCKET)
    return out


def _key_to_float(k):
    return lax.bitcast_convert_type(k ^ (lax.shift_right_arithmetic(k, 31) & 0x7FFFFFFF), F32)


def _kth_largest_search(count_ge, topk, shape):
    def bit_body(b, carry):
        t, cnt_t = carry
        cand = t ^ lax.shift_left(jnp.int32(1), 31 - b)
        cnt = count_ge(_key_to_float(cand))
        ok = cnt >= topk
        return jnp.where(ok, cand, t), jnp.where(ok, cnt, cnt_t)

    init = (jnp.full(shape, INT_MIN, jnp.int32), jnp.full(shape, 3e38, F32))
    t, cnt_t = lax.fori_loop(0, 32, bit_body, init)
    return _key_to_float(t), cnt_t


def _ln(y, g, b):
    mu = jnp.mean(y, axis=-1, keepdims=True)
    yc = y - mu
    var = jnp.mean(yc * yc, axis=-1, keepdims=True)
    return yc * lax.rsqrt(var + LN_EPS) * g + b


def _bias_kernel(table_ref, bk_ref, bkd_ref, toep2_ref, dec_ref):
    bk = bk_ref[...]
    bkd = bkd_ref[...]
    dec_rows = []
    for h in range(N_HEADS):
        acc = jnp.zeros(bk.shape, F32)
        accd = jnp.zeros(bkd.shape, F32)
        for b in range(N_BUCKETS):
            val = table_ref[b, h]
            acc = jnp.where(bk == b, val, acc)
            accd = jnp.where(bkd == b, val, accd)
        toep2_ref[h] = acc * LOG2E
        dec_rows.append(accd)
    for kind in range(3):
        dec_ref[kind] = jnp.concatenate([d[kind:kind + 1, :] for d in dec_rows], axis=0)


def _bias_tables(table):
    i = np.arange(TILE)[None, :]
    j = np.arange(2 * TILE)[:, None]
    bk = np.concatenate([np.full((TILE, TILE), FAR_BUCKET, np.int32), _bucket_np(i + TILE - j)], axis=0)
    bkd = np.zeros((8, TILE), np.int32)
    bkd[0] = _bucket_np(TILE - np.arange(TILE))
    bkd[1] = FAR_BUCKET
    bkd[2] = 0
    return pl.pallas_call(
        _bias_kernel,
        grid=(1,),
        in_specs=[pl.BlockSpec(memory_space=pltpu.SMEM),
                  pl.BlockSpec((3 * TILE, TILE), lambda h: (0, 0)),
                  pl.BlockSpec((8, TILE), lambda h: (0, 0))],
        out_specs=[pl.BlockSpec((N_HEADS, 3 * TILE, TILE), lambda h: (0, 0, 0)),
                   pl.BlockSpec((3, N_HEADS, TILE), lambda h: (0, 0, 0))],
        out_shape=[jax.ShapeDtypeStruct((N_HEADS, 3 * TILE, TILE), F32),
                   jax.ShapeDtypeStruct((3, N_HEADS, TILE), F32)],
        compiler_params=_cparams(("arbitrary",)),
        name="bias_tables",
    )(table, jnp.asarray(bk), jnp.asarray(bkd))


def _normalize_cols(y):
    mu = jnp.mean(y, axis=0, keepdims=True)
    yc = y - mu
    var = jnp.mean(yc * yc, axis=0, keepdims=True)
    return yc * lax.rsqrt(var + LN_EPS)


def _proj_kernel(nkv, has_idx, *refs):
    if has_idx:
        (x_ref, wqT_ref, wkT_ref, wvT_ref, wqiT_ref, wkiT_ref, wki_ref, wwiT_ref, g_ref, b_ref,
         gc_ref, bc_ref, qT_ref, kTp_ref, vTp_ref, kh_ref, vT_ref, qiT_ref, kiTp_ref, kib_ref, wiT_ref) = refs
    else:
        (x_ref, wqT_ref, wkT_ref, wvT_ref, qT_ref, kTp_ref, vTp_ref, kh_ref, vT_ref) = refs
    xb = x_ref[0].astype(BF16)
    npg = xb.shape[0] // TILE
    qT_ref[0] = lax.dot_general(wqT_ref[...], xb, _NT, preferred_element_type=F32).astype(BF16)
    kT = lax.dot_general(wkT_ref[...], xb, _NT, preferred_element_type=F32)
    vT = lax.dot_general(wvT_ref[...], xb, _NT, preferred_element_type=F32)
    vT_ref[0] = vT.astype(BF16)
    for j in range(npg):
        kTp_ref[0, j] = kT[:, j * TILE:(j + 1) * TILE]
        vTp_ref[0, j] = vT[:, j * TILE:(j + 1) * TILE]
    krows = lax.dot_general(xb, wkT_ref[...], _NT, preferred_element_type=F32)
    for g in range(nkv):
        kh_ref[0, g] = krows[:, g * HEAD_DIM:(g + 1) * HEAD_DIM].astype(BF16)
    if has_idx:
        qiT_ref[0] = lax.dot_general(wqiT_ref[...], xb, _NT, preferred_element_type=F32).astype(BF16)
        kiT = _normalize_cols(lax.dot_general(wkiT_ref[...], xb, _NT, preferred_element_type=F32))
        for j in range(npg):
            kiTp_ref[0, j] = kiT[:, j * TILE:(j + 1) * TILE] * gc_ref[...] + bc_ref[...]
        ki = _ln(jnp.dot(xb, wki_ref[...], preferred_element_type=F32), g_ref[...], b_ref[...])
        kib_ref[0] = ki.astype(BF16)
        wiT_ref[0] = lax.dot_general(wwiT_ref[...], xb, _NT, preferred_element_type=F32)


def _prompt_proj(x, wq, wk, wv, nkv, idx=None, q_scale=ATTN_SCALE, tm=1024):
    B, S, D = x.shape
    tm = min(tm, S)
    kvd = nkv * HEAD_DIM
    hq = N_HEADS * HEAD_DIM
    wqT = (wq * q_scale).T.astype(BF16)
    full2 = lambda a: pl.BlockSpec(a.shape, lambda b, m: (0,) * a.ndim)
    ins = [x, wqT, wk.T.astype(BF16), wv.T.astype(BF16)]
    in_specs = [pl.BlockSpec((1, tm, D), lambda b, m: (b, m, 0))] + [full2(a) for a in ins[1:]]
    out_shape = [jax.ShapeDtypeStruct((B, hq, S), BF16),
                 jax.ShapeDtypeStruct((B, S // TILE, kvd, TILE), F32),
                 jax.ShapeDtypeStruct((B, S // TILE, kvd, TILE), F32),
                 jax.ShapeDtypeStruct((B, nkv, S, HEAD_DIM), BF16),
                 jax.ShapeDtypeStruct((B, kvd, S), BF16)]
    out_specs = [pl.BlockSpec((1, hq, tm), lambda b, m: (b, 0, m)),
                 pl.BlockSpec((1, tm // TILE, kvd, TILE), lambda b, m: (b, m, 0, 0)),
                 pl.BlockSpec((1, tm // TILE, kvd, TILE), lambda b, m: (b, m, 0, 0)),
                 pl.BlockSpec((1, nkv, tm, HEAD_DIM), lambda b, m: (b, 0, m, 0)),
                 pl.BlockSpec((1, kvd, tm), lambda b, m: (b, 0, m))]
    if idx is not None:
        wqi, wki, wwi, g, bb = idx
        extra = [wqi.T.astype(BF16), wki.T.astype(BF16), wki.astype(BF16), wwi.T.astype(BF16),
                 g.reshape(1, -1), bb.reshape(1, -1),
                 jnp.broadcast_to(g.reshape(-1, 1), (IDX_DIM, TILE)), jnp.broadcast_to(bb.reshape(-1, 1), (IDX_DIM, TILE))]
        ins += extra
        in_specs += [full2(a) for a in extra]
        hi = IDX_HEADS * IDX_DIM
        out_shape += [jax.ShapeDtypeStruct((B, hi, S), BF16),
                      jax.ShapeDtypeStruct((B, S // TILE, IDX_DIM, TILE), F32),
                      jax.ShapeDtypeStruct((B, S, IDX_DIM), BF16),
                      jax.ShapeDtypeStruct((B, IDX_HEADS, S), F32)]
        out_specs += [pl.BlockSpec((1, hi, tm), lambda b, m: (b, 0, m)),
                      pl.BlockSpec((1, tm // TILE, IDX_DIM, TILE), lambda b, m: (b, m, 0, 0)),
                      pl.BlockSpec((1, tm, IDX_DIM), lambda b, m: (b, m, 0)),
                      pl.BlockSpec((1, IDX_HEADS, tm), lambda b, m: (b, 0, m))]
    return pl.pallas_call(
        functools.partial(_proj_kernel, nkv, idx is not None),
        grid=(B, S // tm),
        in_specs=in_specs,
        out_specs=out_specs,
        out_shape=out_shape,
        compiler_params=_cparams(("arbitrary", "arbitrary")),
        name="prompt_proj_idx" if idx is not None else "prompt_proj",
    )(*ins)


_KEY_UNROLL = 4
_SUM_ROWS = 16
_ATT_TILES = 4


def _dsa_prompt_block(topk, n, qiT_ref, wiT_ref, ki_ref, qT_ref, kh_ref, vT_ref, toep_ref, o_ref,
                      skey_ref, sel_ref, m_ref, acc_ref):
    ST = _KEY_UNROLL * TILE
    ns = n // _KEY_UNROLL + 1
    G = N_HEADS // KV_HEADS_A
    rows = lax.broadcasted_iota(jnp.int32, (ST, TILE), 0)
    lanes = lax.broadcasted_iota(jnp.int32, (ST, TILE), 1)
    qpos = n * TILE + lanes

    qi = qiT_ref[0]
    wi = wiT_ref[0]

    def score_body(u, carry):
        off = pl.multiple_of(u * ST, ST)
        kij = ki_ref[0, pl.ds(off, ST), :]
        acc = jnp.zeros((ST, TILE), F32)
        for h in range(0, IDX_HEADS, 2):
            qpair = jnp.concatenate([qi[IDX_DIM * h:IDX_DIM * (h + 1), :],
                                     qi[IDX_DIM * (h + 1):IDX_DIM * (h + 2), :]], axis=1)
            s = jnp.dot(kij, qpair, preferred_element_type=F32)
            acc = acc + jnp.maximum(s[:, :TILE], 0.0) * wi[h:h + 1, :]
            acc = acc + jnp.maximum(s[:, TILE:], 0.0) * wi[h + 1:h + 2, :]
        sc = acc * IDX_SCALE
        sc = jnp.where(off + rows <= qpos, sc, NEG)
        sc = jnp.where(sc == 0.0, 0.0, sc)
        skey_ref[pl.ds(off, ST), :] = sc
        return carry

    lax.fori_loop(0, ns, score_body, 0)

    def count(pred):
        def body(u, c):
            off = pl.multiple_of(u * ST, ST)
            x = jnp.where(pred(skey_ref[pl.ds(off, ST), :]), 1.0, 0.0).reshape(ST // 8, 8, TILE)
            while x.shape[0] > 1:
                half = x.shape[0] // 2
                x = x[:half] + x[half:]
            return c + x[0]
        c8 = lax.fori_loop(0, ns, body, jnp.zeros((8, TILE), F32))
        return jnp.sum(c8, axis=0, keepdims=True)

    t, cnt_t = _kth_largest_search(lambda c: count(lambda kt: kt >= c), topk, (1, TILE))
    exact_fit = jnp.max(jnp.abs(cnt_t - topk)) == 0.0

    @pl.when(exact_fit)
    def _():
        def sel_body(u, carry):
            off = pl.multiple_of(u * ST, ST)
            sel = jnp.where(skey_ref[pl.ds(off, ST), :] >= t, 1.0, 0.0)
            sel_ref[pl.ds(off, ST), :] = jnp.where(off + rows <= qpos, sel, 0.0)
            return carry

        lax.fori_loop(0, ns, sel_body, 0)

    @pl.when(jnp.logical_not(exact_fit))
    def _():
        room = topk - count(lambda kt: kt > t)
        ii = lax.broadcasted_iota(jnp.int32, (TILE, TILE), 0)
        jj = lax.broadcasted_iota(jnp.int32, (TILE, TILE), 1)
        lower = jnp.where(jj < ii, 1.0, 0.0).astype(BF16)
        ones8 = jnp.ones((8, TILE), BF16)

        def sel_body(u, carry):
            for k in range(_KEY_UNROLL):
                off = pl.multiple_of(u * ST + k * TILE, TILE)
                kt = skey_ref[pl.ds(off, TILE), :]
                eq = kt == t
                eqb = jnp.where(eq, 1.0, 0.0).astype(BF16)
                before = jnp.dot(lower, eqb, preferred_element_type=F32) + carry
                tot = jnp.dot(ones8, eqb, preferred_element_type=F32)[0:1]
                sel = jnp.where(kt > t, 1.0, jnp.where(eq, jnp.where(before < room, 1.0, 0.0), 0.0))
                sel_ref[pl.ds(off, TILE), :] = jnp.where(off + ii <= n * TILE + jj, sel, 0.0)
                carry = carry + tot
            return carry

        lax.fori_loop(0, ns, sel_body, jnp.zeros((1, TILE), F32))

    q = qT_ref[0]
    qgs = [jnp.concatenate([q[(g * G + r) * HEAD_DIM:(g * G + r + 1) * HEAD_DIM, :] for r in range(G)], axis=1)
           for g in range(KV_HEADS_A)]
    m_ref[...] = jnp.full(m_ref.shape, NEG, F32)
    acc_ref[...] = jnp.zeros(acc_ref.shape, F32)

    AT = _ATT_TILES * TILE
    ones_rows = jnp.ones((_SUM_ROWS, AT), BF16)

    def att_body(u, carry):
        off = pl.multiple_of(u * AT, AT)
        tiles = [u * _ATT_TILES + k for k in range(_ATT_TILES)]
        roffs = [pl.multiple_of(jnp.where(j == n, 2 * TILE, jnp.where(j == n - 1, TILE, 0)), TILE) for j in tiles]
        selv = sel_ref[pl.ds(off, AT), :] > 0.5
        ss = [jnp.dot(kh_ref[0, g, pl.ds(off, AT), :], qgs[g], preferred_element_type=F32)
              for g in range(KV_HEADS_A)]
        for g in range(KV_HEADS_A):
            s = ss[g]
            s = jnp.concatenate(
                [jnp.where(selv,
                           s[:, r * TILE:(r + 1) * TILE]
                           + jnp.concatenate([toep_ref[g * G + r, pl.ds(ro, TILE), :] for ro in roffs], axis=0),
                           NEG)
                 for r in range(G)], axis=1)
            m = m_ref[g]
            m_new = jnp.maximum(m, jnp.max(s, axis=0, keepdims=True))
            alpha = jnp.exp2(m - m_new)
            p = jnp.exp2(s - m_new[0:1])
            m_ref[g] = m_new
            vt = jnp.concatenate([vT_ref[0, g * HEAD_DIM:(g + 1) * HEAD_DIM, pl.ds(off, AT)], ones_rows], axis=0)
            acc_ref[g] = acc_ref[g] * alpha[0:1] + jnp.dot(vt, p.astype(BF16), preferred_element_type=F32)
        return carry

    lax.fori_loop(0, n // _ATT_TILES + 1, att_body, 0)
    for g in range(KV_HEADS_A):
        out = acc_ref[g, 0:HEAD_DIM, :] / acc_ref[g, HEAD_DIM:HEAD_DIM + 1, :]
        for r in range(G):
            h = g * G + r
            o_ref[0, h * HEAD_DIM:(h + 1) * HEAD_DIM, :] = out[:, r * TILE:(r + 1) * TILE].astype(o_ref.dtype)


def _swa_prompt_kernel(sink_ref, qT_ref, khp_ref, khc_ref, vTp_ref, vTc_ref, toep_ref, o_ref):
    n = pl.program_id(1)
    G = N_HEADS // KV_HEADS_B
    rows = lax.broadcasted_iota(jnp.int32, (2 * TILE, TILE), 0)
    lanes = lax.broadcasted_iota(jnp.int32, (2 * TILE, TILE), 1)
    dj = rows - lanes
    inwin = jnp.where(dj >= TILE - WINDOW + 1, jnp.where(dj <= TILE, 1.0, 0.0), 0.0)
    inwin = jnp.where(rows >= TILE, inwin, jnp.where(n > 0, inwin, 0.0))
    maskb = jnp.concatenate([inwin] * G, axis=1) > 0.5
    q = qT_ref[0]
    ones_rows = jnp.ones((_SUM_ROWS, 2 * TILE), BF16)
    for g in range(KV_HEADS_B):
        heads = [g * G + r for r in range(G)]
        qg = jnp.concatenate([q[h * HEAD_DIM:(h + 1) * HEAD_DIM, :] for h in heads], axis=1)
        kk = jnp.concatenate([khp_ref[0, g], khc_ref[0, g]], axis=0)
        s = jnp.dot(kk, qg, preferred_element_type=F32)
        bias = jnp.concatenate([toep_ref[h, TILE:3 * TILE, :] for h in heads], axis=1)
        s = jnp.where(maskb, s + bias, NEG)
        sink = jnp.concatenate([jnp.full((1, TILE), sink_ref[h] * LOG2E, F32) for h in heads], axis=1)
        m = jnp.maximum(jnp.max(s, axis=0, keepdims=True), sink)
        p = jnp.exp2(s - m)
        vv = jnp.concatenate([vTp_ref[0, g * HEAD_DIM:(g + 1) * HEAD_DIM, :],
                              vTc_ref[0, g * HEAD_DIM:(g + 1) * HEAD_DIM, :]], axis=1)
        pv = jnp.dot(jnp.concatenate([vv, ones_rows], axis=0), p.astype(BF16), preferred_element_type=F32)
        out = pv[0:HEAD_DIM, :] / (pv[HEAD_DIM:HEAD_DIM + 1, :] + jnp.exp2(sink - m))
        for r, h in enumerate(heads):
            o_ref[0, h * HEAD_DIM:(h + 1) * HEAD_DIM, :] = out[:, r * TILE:(r + 1) * TILE].astype(o_ref.dtype)


def _swa_prompt(qT, kh, vT, toep, sink):
    B, hq, S = qT.shape
    nb = S // TILE
    prev = lambda n: jnp.maximum(n - 1, 0)
    return pl.pallas_call(
        _swa_prompt_kernel,
        grid=(B, nb),
        in_specs=[pl.BlockSpec(memory_space=pltpu.SMEM),
                  pl.BlockSpec((1, hq, TILE), lambda b, n: (b, 0, n)),
                  pl.BlockSpec((1, KV_HEADS_B, TILE, HEAD_DIM), lambda b, n: (b, 0, prev(n), 0)),
                  pl.BlockSpec((1, KV_HEADS_B, TILE, HEAD_DIM), lambda b, n: (b, 0, n, 0)),
                  pl.BlockSpec((1, KV_HEADS_B * HEAD_DIM, TILE), lambda b, n: (b, 0, prev(n))),
                  pl.BlockSpec((1, KV_HEADS_B * HEAD_DIM, TILE), lambda b, n: (b, 0, n)),
                  pl.BlockSpec(toep.shape, lambda b, n: (0, 0, 0))],
        out_specs=pl.BlockSpec((1, hq, TILE), lambda b, n: (b, 0, n)),
        out_shape=jax.ShapeDtypeStruct((B, hq, S), BF16),
        compiler_params=_cparams(("arbitrary", "arbitrary")),
        name="swa_prompt",
    )(sink, qT, kh, kh, vT, vT, toep)


def _residual_kernel(alpha, x_ref, a_ref, wo_ref, g1_ref, b1_ref, w1_ref, w2_ref, g2_ref, b2_ref, o_ref,
                     y_ref, acc_ref):
    f = pl.program_id(1)

    @pl.when(f == 0)
    def _():
        y = alpha * x_ref[...] + jnp.dot(a_ref[...], wo_ref[...], preferred_element_type=F32)
        y_ref[...] = _ln(y, g1_ref[...], b1_ref[...])
        acc_ref[...] = jnp.zeros_like(acc_ref)

    h = jnp.maximum(jnp.dot(y_ref[...].astype(BF16), w1_ref[0].astype(BF16), preferred_element_type=F32), 0.0)
    acc_ref[...] += jnp.dot((h * h).astype(BF16), w2_ref[0].astype(BF16), preferred_element_type=F32)

    @pl.when(f == pl.num_programs(1) - 1)
    def _():
        o_ref[...] = _ln(alpha * y_ref[...] + acc_ref[...], g2_ref[...], b2_ref[...])


def _residual_blocks(x, a, wo, g1, b1, w1, w2, layer, g2, b2, alpha, tm=1024, tf=512):
    M, D = x.shape
    FF = w1.shape[2]
    tm = min(tm, M)
    row = lambda v: v.reshape(1, D)
    vec = pl.BlockSpec((1, D), lambda m, f: (0, 0))
    return pl.pallas_call(
        functools.partial(_residual_kernel, alpha),
        grid=(M // tm, FF // tf),
        in_specs=[pl.BlockSpec((tm, D), lambda m, f: (m, 0)),
                  pl.BlockSpec((tm, a.shape[1]), lambda m, f: (m, 0)),
                  pl.BlockSpec(wo.shape, lambda m, f: (0, 0)),
                  vec, vec,
                  pl.BlockSpec((1, D, tf), lambda m, f: (layer, 0, f)),
                  pl.BlockSpec((1, tf, D), lambda m, f: (layer, f, 0)),
                  vec, vec],
        out_specs=pl.BlockSpec((tm, D), lambda m, f: (m, 0)),
        out_shape=jax.ShapeDtypeStruct((M, D), F32),
        scratch_shapes=[pltpu.VMEM((tm, D), F32), pltpu.VMEM((tm, D), F32)],
        compiler_params=_cparams(("arbitrary", "arbitrary")),
        name="residual_blocks",
    )(x, a, wo, row(g1), row(b1), w1, w2, row(g2), row(b2))


def _matmul_kernel(x_ref, w_ref, o_ref):
    o_ref[...] = jnp.dot(x_ref[...].astype(BF16), w_ref[...], preferred_element_type=F32)


def _matmul(x, w):
    M, K = x.shape
    N = w.shape[1]
    return pl.pallas_call(
        _matmul_kernel,
        grid=(1,),
        in_specs=[pl.BlockSpec((M, K), lambda i: (0, 0)), pl.BlockSpec((K, N), lambda i: (0, 0))],
        out_specs=pl.BlockSpec((M, N), lambda i: (0, 0)),
        out_shape=jax.ShapeDtypeStruct((M, N), F32),
        compiler_params=_cparams(("arbitrary",)),
        name="sample_proj",
    )(x, w)


_SEQ_BLOCK = 16
_IDX_SLOTS = 4


def _dec_index_kernel(topk, n_pages, pt_ref, qi_ref, wi_ref, kiraw_ref, g_ref, b_ref, cidx_ref,
                      selb_ref, newsel_ref, kiln_ref, buf, sem, sc_ref, snew_ref):
    step = pl.program_id(0)
    nseq = pl.num_programs(0) * _SEQ_BLOCK
    L = n_pages * TILE

    def page_copy(seq, slot, p):
        return pltpu.make_async_copy(cidx_ref.at[pt_ref[seq, p]], buf.at[slot, p], sem.at[slot])

    def start(seq, slot):
        for p in range(n_pages):
            page_copy(seq, slot, p).start()

    def wait(seq, slot):
        for p in range(n_pages):
            page_copy(seq, slot, p).wait()

    ahead = _IDX_SLOTS - 1

    @pl.when(step == 0)
    def _():
        for s0 in range(ahead):
            start(s0, s0)

    kiln_ref[...] = _ln(kiraw_ref[...], g_ref[...], b_ref[...])

    def seq_body(i, carry):
        seq = step * _SEQ_BLOCK + i
        slot = i % _IDX_SLOTS

        @pl.when(seq + ahead < nseq)
        def _():
            start(seq + ahead, (i + ahead) % _IDX_SLOTS)

        wait(seq, slot)
        xk = jnp.concatenate([buf[slot, p] for p in range(n_pages)], axis=1).astype(BF16)
        qib = qi_ref[i].astype(BF16)
        s = jnp.dot(qib, xk, preferred_element_type=F32)
        w = wi_ref[i]
        row = jnp.sum(jnp.maximum(s, 0.0) * w, axis=0, keepdims=True) * IDX_SCALE
        sc_ref[pl.ds(i, 1), :] = jnp.where(row == 0.0, 0.0, row)
        kn = kiln_ref[pl.ds(i, 1), :].astype(BF16).astype(F32)
        sn = jnp.sum(qib.astype(F32) * kn, axis=1, keepdims=True)
        snew = jnp.sum(jnp.maximum(sn, 0.0) * w, axis=0, keepdims=True) * IDX_SCALE
        snew_ref[pl.ds(i, 1), :] = jnp.broadcast_to(jnp.where(snew == 0.0, 0.0, snew), (1, TILE))
        return carry

    lax.fori_loop(0, _SEQ_BLOCK, seq_body, 0)
    keys = sc_ref[...]
    knew = snew_ref[:, 0:1]

    def count(pred):
        return (jnp.sum(jnp.where(pred(keys), 1.0, 0.0), axis=1, keepdims=True)
                + jnp.where(pred(knew), 1.0, 0.0))

    t, cnt_t = _kth_largest_search(lambda c: count(lambda k: k >= c), topk, (_SEQ_BLOCK, 1))
    exact_fit = jnp.max(jnp.abs(cnt_t - topk)) == 0.0

    @pl.when(exact_fit)
    def _():
        selb_ref[...] = jnp.where(keys >= t, 1.0, 0.0)
        newsel_ref[...] = jnp.broadcast_to(jnp.where(knew >= t, 1.0, 0.0), (_SEQ_BLOCK, TILE))

    @pl.when(jnp.logical_not(exact_fit))
    def _():
        room = topk - count(lambda k: k > t)
        ii = lax.broadcasted_iota(jnp.int32, (TILE, 2 * TILE), 0)
        jj = lax.broadcasted_iota(jnp.int32, (TILE, 2 * TILE), 1)
        pref = jnp.where(jj >= TILE, 1.0, jnp.where(ii < jj, 1.0, 0.0)).astype(BF16)
        carry = jnp.zeros((_SEQ_BLOCK, TILE), F32)
        for jt in range(n_pages):
            kt = keys[:, jt * TILE:(jt + 1) * TILE]
            eq = kt == t
            res = jnp.dot(jnp.where(eq, 1.0, 0.0).astype(BF16), pref, preferred_element_type=F32)
            take = jnp.where(res[:, :TILE] + carry < room, 1.0, 0.0)
            selb_ref[:, jt * TILE:(jt + 1) * TILE] = jnp.where(kt > t, 1.0, jnp.where(eq, take, 0.0))
            carry = carry + res[:, TILE:]
        newsel = jnp.where(knew > t, 1.0, jnp.where(knew == t, jnp.where(carry[:, :1] < room, 1.0, 0.0), 0.0))
        newsel_ref[...] = jnp.broadcast_to(newsel, (_SEQ_BLOCK, TILE))


def _dec_index(page_table, qi, wi, kiraw, g, b, cidx, topk):
    DB, n_pages = page_table.shape
    L = n_pages * TILE
    sb = _SEQ_BLOCK
    grid_spec = pltpu.PrefetchScalarGridSpec(
        num_scalar_prefetch=1,
        grid=(DB // sb,),
        in_specs=[pl.BlockSpec((sb, IDX_HEADS, IDX_DIM), lambda s, pt: (s, 0, 0)),
                  pl.BlockSpec((sb, IDX_HEADS, 1), lambda s, pt: (s, 0, 0)),
                  pl.BlockSpec((sb, IDX_DIM), lambda s, pt: (s, 0)),
                  pl.BlockSpec((1, IDX_DIM), lambda s, pt: (0, 0)),
                  pl.BlockSpec((1, IDX_DIM), lambda s, pt: (0, 0)),
                  pl.BlockSpec(memory_space=pl.ANY)],
        out_specs=[pl.BlockSpec((sb, L), lambda s, pt: (s, 0)),
                   pl.BlockSpec((sb, TILE), lambda s, pt: (s, 0)),
                   pl.BlockSpec((sb, IDX_DIM), lambda s, pt: (s, 0))],
        scratch_shapes=[pltpu.VMEM((_IDX_SLOTS, n_pages, IDX_DIM, TILE), F32),
                        pltpu.SemaphoreType.DMA((_IDX_SLOTS,)),
                        pltpu.VMEM((sb, L), F32),
                        pltpu.VMEM((sb, TILE), F32)])
    return pl.pallas_call(
        functools.partial(_dec_index_kernel, topk, n_pages),
        grid_spec=grid_spec,
        out_shape=[jax.ShapeDtypeStruct((DB, L), F32),
                   jax.ShapeDtypeStruct((DB, TILE), F32),
                   jax.ShapeDtypeStruct((DB, IDX_DIM), F32)],
        compiler_params=_cparams(("arbitrary",)),
        name="dec_index",
    )(page_table, qi, wi, kiraw, g.reshape(1, -1), b.reshape(1, -1), cidx)


_PAGE_CHUNK = 8


def _decode_attention(n_pages, q_ref, selb_ref, newsel_ref, kn_ref, vn_ref, dec_ref, kbuf, vbuf, o_ref):
    nch = n_pages // _PAGE_CHUNK
    CL = _PAGE_CHUNK * TILE
    qb = q_ref[0]
    far = dec_ref[1]
    near = dec_ref[0]
    pages = lambda buf, c: jnp.concatenate(
        [buf[c * _PAGE_CHUNK + p] for p in range(_PAGE_CHUNK)], axis=1).astype(BF16)
    ss = []
    for c in range(nch):
        s = jnp.dot(qb, pages(kbuf, c), preferred_element_type=F32)
        bias = jnp.concatenate([far] * (_PAGE_CHUNK - 1) + [near if c + 1 == nch else far], axis=1)
        sel = selb_ref[0, :, c * CL:(c + 1) * CL] > 0.5
        ss.append(jnp.where(sel, s + bias, NEG))
    kn = kn_ref[0].astype(BF16).astype(F32)
    sn = jnp.sum(qb.astype(F32) * kn, axis=1, keepdims=True) + dec_ref[2][:, 0:1]
    sn = jnp.where(newsel_ref[0, :, 0:1] > 0.5, sn, NEG)
    m = jnp.maximum(jnp.max(functools.reduce(jnp.maximum, ss), axis=1, keepdims=True), sn)
    pn = jnp.exp(sn - m)
    ps = [jnp.exp(s - m) for s in ss]
    l = jnp.sum(functools.reduce(jnp.add, ps), axis=1, keepdims=True) + pn
    acc = pn * vn_ref[0]
    for c in range(nch):
        acc = acc + lax.dot_general(ps[c].astype(BF16), pages(vbuf, c), _NT, preferred_element_type=F32)
    o_ref[0] = acc / l


def _dsa_mixer_kernel(topk, n_pages, pt_ref,
                      qiT_ref, wiT_ref, ki_ref, qT_ref, kh_ref, vT_ref, toep_ref,
                      qd_ref, selb_ref, newsel_ref, kn_ref, vn_ref, dec_ref, ck_ref, cv_ref,
                      o_ref, od_ref, skey_ref, sel_ref, m_ref, acc_ref, kbuf, vbuf, sem):
    n = pl.program_id(1)
    nq = pl.num_programs(1)
    seq = pl.program_id(0) * nq + n
    nseq = pl.num_programs(0) * nq

    def copies(s_):
        out = []
        for p in range(n_pages):
            phys = pt_ref[s_, p]
            out.append(pltpu.make_async_copy(ck_ref.at[phys], kbuf.at[p], sem.at[0]))
            out.append(pltpu.make_async_copy(cv_ref.at[phys], vbuf.at[p], sem.at[1]))
        return out

    @pl.when(seq == 0)
    def _():
        for cp in copies(0):
            cp.start()

    for cp in copies(seq):
        cp.wait()
    _decode_attention(n_pages, qd_ref, selb_ref, newsel_ref, kn_ref, vn_ref, dec_ref, kbuf, vbuf, od_ref)

    @pl.when(seq + 1 < nseq)
    def _():
        for cp in copies(seq + 1):
            cp.start()

    _dsa_prompt_block(topk, n, qiT_ref, wiT_ref, ki_ref, qT_ref, kh_ref, vT_ref, toep_ref, o_ref,
                      skey_ref, sel_ref, m_ref, acc_ref)


def _dsa_mixer(page_table, qiT, wiT, kib, qT, kh, vT, toep, topk, qbd, selb, newsel, kn, vn, dec, ck, cv):
    B, hq, S = qT.shape
    nq = S // TILE
    DB, n_pages = page_table.shape
    L = n_pages * TILE
    kvd = KV_HEADS_A * HEAD_DIM
    assert nq % _KEY_UNROLL == 0 and n_pages % _PAGE_CHUNK == 0
    assert DB == B * nq
    cw = (N_HEADS // KV_HEADS_A) * TILE
    dseq = lambda b, n, pt: (b * nq + n, 0, 0)
    grid_spec = pltpu.PrefetchScalarGridSpec(
        num_scalar_prefetch=1,
        grid=(B, nq),
        in_specs=[pl.BlockSpec((1, qiT.shape[1], TILE), lambda b, n, pt: (b, 0, n)),
                  pl.BlockSpec((1, IDX_HEADS, TILE), lambda b, n, pt: (b, 0, n)),
                  pl.BlockSpec((1, S, IDX_DIM), lambda b, n, pt: (b, 0, 0)),
                  pl.BlockSpec((1, hq, TILE), lambda b, n, pt: (b, 0, n)),
                  pl.BlockSpec((1, KV_HEADS_A, S, HEAD_DIM), lambda b, n, pt: (b, 0, 0, 0)),
                  pl.BlockSpec((1, kvd, S), lambda b, n, pt: (b, 0, 0)),
                  pl.BlockSpec(toep.shape, lambda b, n, pt: (0, 0, 0)),
                  pl.BlockSpec((1, N_HEADS, kvd), dseq),
                  pl.BlockSpec((1, 1, L), dseq),
                  pl.BlockSpec((1, 1, TILE), dseq),
                  pl.BlockSpec((1, 1, kvd), dseq),
                  pl.BlockSpec((1, 1, kvd), dseq),
                  pl.BlockSpec(dec.shape, lambda b, n, pt: (0, 0, 0)),
                  pl.BlockSpec(memory_space=pl.ANY),
                  pl.BlockSpec(memory_space=pl.ANY)],
        out_specs=[pl.BlockSpec((1, hq, TILE), lambda b, n, pt: (b, 0, n)),
                   pl.BlockSpec((1, N_HEADS, kvd), dseq)],
        scratch_shapes=[pltpu.VMEM((S, TILE), F32), pltpu.VMEM((S, TILE), F32),
                        pltpu.VMEM((KV_HEADS_A, 8, cw), F32),
                        pltpu.VMEM((KV_HEADS_A, HEAD_DIM + _SUM_ROWS, cw), F32),
                        pltpu.VMEM((n_pages, kvd, TILE), F32),
                        pltpu.VMEM((n_pages, kvd, TILE), F32),
                        pltpu.SemaphoreType.DMA((2,))])
    return pl.pallas_call(
        functools.partial(_dsa_mixer_kernel, topk, n_pages),
        grid_spec=grid_spec,
        out_shape=[jax.ShapeDtypeStruct((B, hq, S), BF16),
                   jax.ShapeDtypeStruct((DB, N_HEADS, kvd), F32)],
        compiler_params=_cparams(("arbitrary", "arbitrary"), _VMEM_LIMIT_MIXER),
        name="dsa_mixer",
    )(page_table, qiT, wiT, kib, qT, kh, vT, toep,
      qbd, selb.reshape(DB, 1, L), newsel.reshape(DB, 1, TILE), kn.reshape(DB, 1, kvd), vn.reshape(DB, 1, kvd),
      dec, ck, cv)


def _swa_dec_kernel(q_ref, wk_ref, wv_ref, kn_ref, vn_ref, kcol_ref, vcol_ref, dec_ref, sink_ref,
                    o_ref, nwk_ref, nwv_ref):
    lane = lax.broadcasted_iota(jnp.int32, (N_HEADS, TILE), 1)
    newest = lax.broadcasted_iota(jnp.int32, wk_ref.shape[1:], 1) == WINDOW - 1
    bias = dec_ref[0]
    b0 = dec_ref[2][:, 0:1]
    sk = sink_ref[...]
    for i in range(_SEQ_BLOCK):
        nwk_ref[i] = jnp.where(newest, kcol_ref[i], pltpu.roll(wk_ref[i], WINDOW - 1, 1))
        nwv_ref[i] = jnp.where(newest, vcol_ref[i], pltpu.roll(wv_ref[i], WINDOW - 1, 1))
        qb = q_ref[i]
        s = jnp.dot(qb, wk_ref[i].astype(BF16), preferred_element_type=F32)
        s = jnp.where(lane >= 1, s + bias, NEG)
        kn = kn_ref[i].astype(BF16).astype(F32)
        sn = jnp.sum(qb.astype(F32) * kn, axis=1, keepdims=True) + b0
        m = jnp.maximum(jnp.maximum(jnp.max(s, axis=1, keepdims=True), sn), sk)
        p = jnp.exp(s - m)
        pn = jnp.exp(sn - m)
        l = jnp.sum(p, axis=1, keepdims=True) + pn + jnp.exp(sk - m)
        out = lax.dot_general(p.astype(BF16), wv_ref[i].astype(BF16), _NT, preferred_element_type=F32) + pn * vn_ref[i]
        o_ref[i] = out / l


def _swa_dec(qbd, wk, wv, kn, vn, dec, sink):
    DB = qbd.shape[0]
    kvd = KV_HEADS_B * HEAD_DIM
    sb = _SEQ_BLOCK
    return pl.pallas_call(
        _swa_dec_kernel,
        grid=(DB // sb,),
        in_specs=[pl.BlockSpec((sb, N_HEADS, kvd), lambda s: (s, 0, 0)),
                  pl.BlockSpec((sb, kvd, WINDOW), lambda s: (s, 0, 0)),
                  pl.BlockSpec((sb, kvd, WINDOW), lambda s: (s, 0, 0)),
                  pl.BlockSpec((sb, 1, kvd), lambda s: (s, 0, 0)),
                  pl.BlockSpec((sb, 1, kvd), lambda s: (s, 0, 0)),
                  pl.BlockSpec((sb, kvd, 1), lambda s: (s, 0, 0)),
                  pl.BlockSpec((sb, kvd, 1), lambda s: (s, 0, 0)),
                  pl.BlockSpec(dec.shape, lambda s: (0, 0, 0)),
                  pl.BlockSpec((N_HEADS, 1), lambda s: (0, 0))],
        out_specs=[pl.BlockSpec((sb, N_HEADS, kvd), lambda s: (s, 0, 0)),
                   pl.BlockSpec((sb, kvd, WINDOW), lambda s: (s, 0, 0)),
                   pl.BlockSpec((sb, kvd, WINDOW), lambda s: (s, 0, 0))],
        out_shape=[jax.ShapeDtypeStruct((DB, N_HEADS, kvd), F32),
                   jax.ShapeDtypeStruct((DB, kvd, WINDOW), F32),
                   jax.ShapeDtypeStruct((DB, kvd, WINDOW), F32)],
        compiler_params=_cparams(("arbitrary",)),
        name="swa_dec",
    )(qbd, wk, wv, kn.reshape(DB, 1, kvd), vn.reshape(DB, 1, kvd), kn.reshape(DB, kvd, 1), vn.reshape(DB, kvd, 1),
      dec, sink.reshape(N_HEADS, 1))


def _block_diag_q(q, nkv):
    DB = q.shape[0]
    G = N_HEADS // nkv
    qh = (q * ATTN_SCALE).reshape(DB, N_HEADS, 1, HEAD_DIM)
    own = (np.arange(N_HEADS)[:, None] // G == np.arange(nkv)[None, :]).astype(np.float32)
    return (qh * own[None, :, :, None]).reshape(DB, N_HEADS, nkv * HEAD_DIM).astype(BF16)


def _own_group(o, nkv):
    DB = o.shape[0]
    G = N_HEADS // nkv
    o5 = o.reshape(DB, nkv, G, nkv, HEAD_DIM)
    return jnp.stack([o5[:, g, :, g, :] for g in range(nkv)], axis=1).reshape(DB, N_HEADS * HEAD_DIM)


def _pad_cols(w, mult=TILE):
    pad = (-w.shape[1]) % mult
    return jnp.pad(w, ((0, 0), (0, pad))) if pad else w


def _slots_minor(a):
    n, t, h, d = a.shape
    return a.transpose(0, 2, 3, 1).reshape(n, h * d, t)


def _slots_major(a, heads):
    n, hd, t = a.shape
    return a.reshape(n, heads, hd // heads, t).transpose(0, 3, 1, 2)


def kernel(x_prompt, x_sample, cache_k, cache_v, cache_idx_k, state_win_k, state_win_v, page_table, rel_bias_table, a_w_in, a_w_out, a_idx_ln_g, a_idx_ln_b, b_w_in, b_w_out, b_sink, ln1_g, ln1_b, ln2_g, ln2_b, mlp_w1, mlp_w2):
    B, S, D = x_prompt.shape
    DB, DS, _ = x_sample.shape
    depth = ln1_g.shape[0]
    assert depth == 2 and DS == 1 and S % TILE == 0 and DB % _SEQ_BLOCK == 0
    assert state_win_k.shape[2] == WINDOW and cache_k.shape[2] == TILE
    alpha = (2 * depth) ** 0.25
    n_pages = page_table.shape[1]
    past = n_pages * TILE
    hq = N_HEADS * HEAD_DIM
    kva = KV_HEADS_A * HEAD_DIM
    kvb = KV_HEADS_B * HEAD_DIM
    hi = IDX_HEADS * IDX_DIM

    toep2, dec = _bias_tables(rel_bias_table)

    wa = a_w_in[0]
    c0, c1, c2, c3, c4 = hq, hq + kva, hq + 2 * kva, hq + 2 * kva + hi, hq + 2 * kva + hi + IDX_DIM
    wq, wk, wv, wqi, wki, wwi = wa[:, :c0], wa[:, c0:c1], wa[:, c1:c2], wa[:, c2:c3], wa[:, c3:c4], wa[:, c4:]
    a_wo = a_w_out[0].astype(BF16)

    qT, kTp, vTp, kh, vT, qiT, kiTp, kib, wiT = _prompt_proj(
        x_prompt, wq, wk, wv, KV_HEADS_A, idx=(wqi, wki, wwi, a_idx_ln_g[0], a_idx_ln_b[0]),
        q_scale=ATTN_SCALE * LOG2E)
    xs = x_sample.reshape(DB, D)
    proj = _matmul(xs, _pad_cols(wa).astype(BF16))
    q_s, k_s, v_s = proj[:, :c0], proj[:, c0:c1], proj[:, c1:c2]
    qi_s, kiraw_s, wi_s = proj[:, c2:c3], proj[:, c3:c4], proj[:, c4:c4 + IDX_HEADS]
    selb, newsel, ki_s = _dec_index(
        page_table, qi_s.reshape(DB, IDX_HEADS, IDX_DIM), wi_s.reshape(DB, IDX_HEADS, 1), kiraw_s,
        a_idx_ln_g[0], a_idx_ln_b[0], cache_idx_k[0].transpose(0, 2, 1), min(TOPK_MAX, (past + DS) // 4))
    oT, o_s = _dsa_mixer(page_table, qiT, wiT, kib, qT, kh, vT, toep2, min(TOPK_MAX, S // 4),
                         _block_diag_q(q_s, KV_HEADS_A), selb, newsel, k_s, v_s, dec,
                         _slots_minor(cache_k[0]), _slots_minor(cache_v[0]))

    attn = oT.transpose(0, 2, 1).reshape(B * S, hq)
    y_p = _residual_blocks(x_prompt.reshape(B * S, D), attn, a_wo, ln1_g[0], ln1_b[0],
                           mlp_w1, mlp_w2, 0, ln2_g[0], ln2_b[0], alpha)

    y_s = _residual_blocks(xs, _own_group(o_s, KV_HEADS_A).astype(BF16), a_wo, ln1_g[0], ln1_b[0],
                           mlp_w1, mlp_w2, 0, ln2_g[0], ln2_b[0], alpha)

    wb = b_w_in[0]
    bq, bk, bv = wb[:, :hq], wb[:, hq:hq + kvb], wb[:, hq + kvb:]
    b_wo = b_w_out[0].astype(BF16)

    qT1, k1Tp, v1Tp, kh1, vT1 = _prompt_proj(y_p.reshape(B, S, D), bq, bk, bv, KV_HEADS_B,
                                             q_scale=ATTN_SCALE * LOG2E)
    oT1 = _swa_prompt(qT1, kh1, vT1, toep2, b_sink[0])
    attn1 = oT1.transpose(0, 2, 1).reshape(B * S, hq)
    y_p = _residual_blocks(y_p, attn1, b_wo, ln1_g[1], ln1_b[1], mlp_w1, mlp_w2, 1, ln2_g[1], ln2_b[1], alpha)

    proj1 = _matmul(y_s, wb.astype(BF16))
    q1_s, k1_s, v1_s = proj1[:, :hq], proj1[:, hq:hq + kvb], proj1[:, hq + kvb:]
    win_k = _slots_minor(state_win_k[0])
    win_v = _slots_minor(state_win_v[0])
    o1_s, new_wk_s, new_wv_s = _swa_dec(_block_diag_q(q1_s, KV_HEADS_B), win_k, win_v, k1_s, v1_s, dec, b_sink[0])
    y_s = _residual_blocks(y_s, _own_group(o1_s, KV_HEADS_B).astype(BF16), b_wo, ln1_g[1], ln1_b[1],
                           mlp_w1, mlp_w2, 1, ln2_g[1], ln2_b[1], alpha)

    npg = S // TILE
    assert min(WINDOW, S) == TILE
    pages = lambda a, heads: _slots_major(a.reshape(B * npg, -1, TILE), heads).reshape(1, B, npg, TILE, heads, HEAD_DIM)
    return (y_p.reshape(B, S, D),
            y_s.reshape(DB, DS, D),
            pages(kTp, KV_HEADS_A),
            pages(vTp, KV_HEADS_A),
            kiTp.transpose(0, 1, 3, 2).reshape(1, B, npg, TILE, IDX_DIM),
            k_s.reshape(1, DB, DS, KV_HEADS_A, HEAD_DIM),
            v_s.reshape(1, DB, DS, KV_HEADS_A, HEAD_DIM),
            ki_s.reshape(1, DB, DS, IDX_DIM),
            _slots_major(k1Tp[:, npg - 1], KV_HEADS_B)[None],
            _slots_major(v1Tp[:, npg - 1], KV_HEADS_B)[None],
            _slots_major(new_wk_s, KV_HEADS_B)[None],
            _slots_major(new_wv_s, KV_HEADS_B)[None])
```
